```python
import jax, jax.numpy as jnp
from jax import lax
import numpy as np

D_MODEL = 1024
BATCH = 8
SEQ = 2048
DEPTH = 2
DEC_BATCH = 128
DEC_SEQ = 1
PAST_LEN = 16384
PAGE_SIZE = 128

N_EVEN = (DEPTH + 1) // 2
N_ODD = DEPTH // 2
CONV_CH = D_MODEL // 2
CONV_K = 31
DN_HEAD_DIM = 128
DN_HEADS = (D_MODEL // 2) // DN_HEAD_DIM
DN_WIDTH = DN_HEADS * DN_HEAD_DIM
QKV_CH = 3 * DN_WIDTH
SC_K = 4
DN_CHUNK = 64
EVEN_IN = 3 * CONV_CH + QKV_CH + DN_WIDTH + 2 * DN_HEADS
EVEN_MIX = CONV_CH + DN_WIDTH
EVEN_SPLITS = (CONV_CH, 2 * CONV_CH, 3 * CONV_CH, 3 * CONV_CH + QKV_CH,
               3 * CONV_CH + QKV_CH + DN_WIDTH, 3 * CONV_CH + QKV_CH + DN_WIDTH + DN_HEADS)
POOL_WINDOWS = (2, 4, 8, 16)
POOL_WIDTH = D_MODEL
POOL_GROUP = POOL_WIDTH // len(POOL_WINDOWS)
POOL_BUF = max(POOL_WINDOWS) - 1
N_MEM = 256
XA_HEADS = 4
XA_HEAD_DIM = D_MODEL // XA_HEADS
XA_WIDTH = XA_HEADS * XA_HEAD_DIM
EPS = 1e-6

kernel_name = 'hybrid_conformer_deltanet_pool_decoder_step'


def rms_norm(x, g):
    xf = x.astype(jnp.float32)
    y = xf * lax.rsqrt(jnp.mean(xf * xf, axis=-1, keepdims=True) + EPS)
    return y.astype(x.dtype) * g.astype(x.dtype)


def layer_norm(x, g, b):
    xf = x.astype(jnp.float32)
    xc = xf - jnp.mean(xf, axis=-1, keepdims=True)
    y = xc * lax.rsqrt(jnp.mean(xc * xc, axis=-1, keepdims=True) + EPS)
    return y.astype(x.dtype) * g.astype(x.dtype) + b.astype(x.dtype)


def l2_normalize(x):
    return x * lax.rsqrt(jnp.sum(x * x, axis=-1, keepdims=True) + EPS)


def causal_depthwise_conv(x, buf, w):
    k_len, ch = w.shape
    xp = jnp.concatenate([buf.astype(x.dtype), x], axis=1)
    y = lax.conv_general_dilated(xp, w.astype(x.dtype)[:, None, :], window_strides=(1,), padding='VALID',
                                 dimension_numbers=('NWC', 'WIO', 'NWC'), feature_group_count=ch)
    return y, xp[:, xp.shape[1] - (k_len - 1):]


def gated_delta_chunked(q, k, v, g, beta, s0):
    b, t, h, dk = q.shape
    dv = v.shape[-1]
    c = DN_CHUNK
    n = t // c

    def blocks(a):
        return jnp.moveaxis(a.reshape((b, n, c, h) + a.shape[3:]), 3, 1)

    q, k, v, g, beta = blocks(q * dk ** -0.5), blocks(k), blocks(v), blocks(g), blocks(beta)
    gc = jnp.cumsum(g, axis=-1)
    incl = jnp.tril(jnp.ones((c, c), dtype=bool))
    strict = jnp.tril(jnp.ones((c, c), dtype=bool), -1)
    diff = gc[..., :, None] - gc[..., None, :]
    decay = jnp.where(incl, jnp.exp(jnp.where(incl, diff, 0.0)), 0.0)
    kb = k * beta[..., None]
    a_low = jnp.where(strict, jnp.einsum('bhnid,bhnjd->bhnij', kb, k) * decay, 0.0)
    rhs = jnp.concatenate([v * beta[..., None], kb * jnp.exp(gc)[..., None]], axis=-1)
    sol = lax.linalg.triangular_solve(a_low + jnp.eye(c, dtype=a_low.dtype), rhs, left_side=True,
                                      lower=True, unit_diagonal=True)
    u, w = sol[..., :dv], sol[..., dv:]
    qk = jnp.where(incl, jnp.einsum('bhnid,bhnjd->bhnij', q, k) * decay, 0.0)
    q_dec = q * jnp.exp(gc)[..., None]
    k_dec = k * jnp.exp(gc[..., -1:] - gc)[..., None]
    g_last = jnp.exp(gc[..., -1])

    def step(s, xs):
        u_n, w_n, qd_n, kd_n, qk_n, gl_n = xs
        v_new = u_n - jnp.einsum('bhcd,bhde->bhce', w_n, s)
        o_n = jnp.einsum('bhcd,bhde->bhce', qd_n, s) + jnp.einsum('bhcj,bhje->bhce', qk_n, v_new)
        s = s * gl_n[..., None, None] + jnp.einsum('bhcd,bhce->bhde', kd_n, v_new)
        return s, o_n

    xs = tuple(jnp.moveaxis(a, 2, 0) for a in (u, w, q_dec, k_dec, qk, g_last))
    s_fin, o = lax.scan(step, s0, xs)
    o = jnp.transpose(o, (1, 0, 3, 2, 4)).reshape(b, t, h, dv)
    return o, s_fin


def gated_delta_recurrent(q, k, v, g, beta, s0):
    q = q * q.shape[-1] ** -0.5

    def step(s, xs):
        q_t, k_t, v_t, g_t, b_t = xs
        s = s * jnp.exp(g_t)[..., None, None]
        v_new = (v_t - jnp.einsum('bhd,bhde->bhe', k_t, s)) * b_t[..., None]
        s = s + jnp.einsum('bhd,bhe->bhde', k_t, v_new)
        return s, jnp.einsum('bhd,bhde->bhe', q_t, s)

    xs = tuple(jnp.moveaxis(a, 1, 0) for a in (q, k, v, g, beta))
    s_fin, o = lax.scan(step, s0, xs)
    return jnp.moveaxis(o, 0, 1), s_fin


def even_mixer(h, conv_buf, qkv_buf, s0, w_in, dw_w, dw_b, ln_g, ln_b, sc_w, a_log, dt_bias, dn_g, w_out,
               recurrent):
    b, t, _ = h.shape
    p = h @ w_in
    glu_val, glu_gate, gate_a, qkv, z, beta_l, a_l = jnp.split(p, EVEN_SPLITS, axis=-1)
    c, conv_buf_new = causal_depthwise_conv(glu_val * jax.nn.sigmoid(glu_gate), conv_buf, dw_w)
    c = jax.nn.silu(layer_norm(c + dw_b.astype(c.dtype), ln_g, ln_b))
    a_out = c * jax.nn.silu(gate_a)
    qkv_c, qkv_buf_new = causal_depthwise_conv(qkv, qkv_buf, sc_w)
    qkv_c = jax.nn.silu(qkv_c).astype(jnp.float32).reshape(b, t, 3, DN_HEADS, DN_HEAD_DIM)
    q = l2_normalize(qkv_c[:, :, 0])
    k = l2_normalize(qkv_c[:, :, 1])
    v = qkv_c[:, :, 2]
    beta = jax.nn.sigmoid(beta_l.astype(jnp.float32))
    g = -jnp.exp(a_log.astype(jnp.float32)) * jax.nn.softplus(a_l.astype(jnp.float32) + dt_bias.astype(jnp.float32))
    delta = gated_delta_recurrent if recurrent else gated_delta_chunked
    o, s_new = delta(q, k, v, g, beta, s0.astype(jnp.float32))
    o = rms_norm(o, dn_g) * jax.nn.silu(z.astype(jnp.float32).reshape(b, t, DN_HEADS, DN_HEAD_DIM))
    b_out = o.reshape(b, t, DN_WIDTH).astype(h.dtype)
    y = jnp.concatenate([a_out, b_out], axis=-1) @ w_out
    return y, conv_buf_new, qkv_buf_new, s_new.astype(s0.dtype)


def causal_multiscale_pool(u, buf, start_pos):
    b, t, ch = u.shape
    up = jnp.concatenate([buf.astype(u.dtype), u], axis=1)
    cs = jnp.cumsum(up.astype(jnp.float32), axis=1)
    cs = jnp.concatenate([jnp.zeros((b, 1, ch), jnp.float32), cs], axis=1)
    pos = start_pos + jnp.arange(t)
    end = cs[:, POOL_BUF + 1:POOL_BUF + 1 + t]
    means = []
    for gi, win in enumerate(POOL_WINDOWS):
        sl = slice(gi * POOL_GROUP, (gi + 1) * POOL_GROUP)
        begin = cs[:, POOL_BUF + 1 - win:POOL_BUF + 1 - win + t, sl]
        cnt = jnp.minimum(pos + 1, win).astype(jnp.float32)
        means.append((end[..., sl] - begin) / cnt[None, :, None])
    pooled = jnp.concatenate(means, axis=-1) - u.astype(jnp.float32)
    return pooled.astype(u.dtype), up[:, up.shape[1] - POOL_BUF:]


def odd_mixer(h, pool_buf, w_in, w_pool, b_pool, scale, w_out, start_pos):
    b, t, _ = h.shape
    u, gate = jnp.split(h @ w_in, 2, axis=-1)
    pooled, buf_new = causal_multiscale_pool(u, pool_buf, start_pos)
    z = jnp.einsum('btgc,gcd->btgd', pooled.reshape(b, t, len(POOL_WINDOWS), POOL_GROUP), w_pool) + b_pool
    z = z.reshape(b, t, POOL_WIDTH) * scale * jax.nn.silu(gate)
    return z @ w_out, buf_new


def memory_cross_attention(h, mk, mv, wq, wo):
    b, t, _ = h.shape
    q = (h @ wq).reshape(b, t, XA_HEADS, XA_HEAD_DIM)
    s = jnp.einsum('bthd,bmhd->bhtm', q, mk.astype(q.dtype)).astype(jnp.float32) * XA_HEAD_DIM ** -0.5
    pr = jax.nn.softmax(s, axis=-1).astype(q.dtype)
    o = jnp.einsum('bhtm,bmhd->bthd', pr, mv.astype(q.dtype)).reshape(b, t, XA_WIDTH)
    return o @ wo


def trunk(x, conv_bufs, qkv_bufs, dn_states, pool_bufs, mem_k, mem_v, prm, start_pos, recurrent):
    new_conv, new_qkv, new_dn, new_pool = [], [], [], []
    for l in range(DEPTH):
        h = rms_norm(x, prm['norm_mix'][l])
        if l % 2 == 0:
            e = l // 2
            y, cb, qb, st = even_mixer(h, conv_bufs[e], qkv_bufs[e], dn_states[e], prm['w_in_even'][e],
                                       prm['dw_w'][e], prm['dw_b'][e], prm['ln_a_g'][e], prm['ln_a_b'][e],
                                       prm['sc_w'][e], prm['a_log'][e], prm['dt_bias'][e],
                                       prm['dn_norm_g'][e], prm['w_out_even'][e], recurrent)
            new_conv.append(cb)
            new_qkv.append(qb)
            new_dn.append(st)
        else:
            o = l // 2
            y, pb = odd_mixer(h, pool_bufs[o], prm['w_in_odd'][o], prm['w_pool'][o], prm['b_pool'][o],
                              prm['pool_scale'][o], prm['w_out_odd'][o], start_pos)
            new_pool.append(pb)
        x = x + y
        h = rms_norm(x, prm['norm_xattn'][l])
        x = x + memory_cross_attention(h, mem_k[l], mem_v[l], prm['w_xq'][l], prm['w_xo'][l])
    y = rms_norm(x, prm['norm_final'])
    return y, jnp.stack(new_conv), jnp.stack(new_qkv), jnp.stack(new_dn), jnp.stack(new_pool)


def setup_inputs(seed: int = 0) -> dict:
    key = jax.random.key(seed)
    ks = jax.random.split(key, 40)

    def nrm(k, shape, scale):
        return jax.random.normal(k, shape, jnp.float32) * scale

    dt = jnp.exp(jax.random.uniform(ks[20], (N_EVEN, DN_HEADS), jnp.float32, np.log(1e-3), np.log(1e-1)))
    return {
        'x_prompt': nrm(ks[0], (BATCH, SEQ, D_MODEL), 1.0),
        'x_sample': nrm(ks[1], (DEC_BATCH, DEC_SEQ, D_MODEL), 1.0),
        'state_conv_a': nrm(ks[2], (N_EVEN, DEC_BATCH, CONV_K - 1, CONV_CH), 0.5),
        'state_qkv_conv': nrm(ks[3], (N_EVEN, DEC_BATCH, SC_K - 1, QKV_CH), 1.0),
        'state_delta': nrm(ks[4], (N_EVEN, DEC_BATCH, DN_HEADS, DN_HEAD_DIM, DN_HEAD_DIM), 0.1),
        'state_pool': nrm(ks[5], (N_ODD, DEC_BATCH, POOL_BUF, POOL_WIDTH), 1.0),
        'cache_mem_k': nrm(ks[6], (DEPTH, DEC_BATCH, N_MEM, XA_HEADS, XA_HEAD_DIM), 1.0),
        'cache_mem_v': nrm(ks[7], (DEPTH, DEC_BATCH, N_MEM, XA_HEADS, XA_HEAD_DIM), 1.0),
        'mem_prompt': nrm(ks[8], (BATCH, N_MEM, D_MODEL), 1.0),
        'norm_mix': 1.0 + nrm(ks[9], (DEPTH, D_MODEL), 0.02),
        'norm_xattn': 1.0 + nrm(ks[10], (DEPTH, D_MODEL), 0.02),
        'norm_final': 1.0 + nrm(ks[11], (D_MODEL,), 0.02),
        'w_in_even': nrm(ks[12], (N_EVEN, D_MODEL, EVEN_IN), D_MODEL ** -0.5),
        'w_out_even': nrm(ks[13], (N_EVEN, EVEN_MIX, D_MODEL), EVEN_MIX ** -0.5),
        'dw_w': nrm(ks[14], (N_EVEN, CONV_K, CONV_CH), CONV_K ** -0.5),
        'dw_b': nrm(ks[15], (N_EVEN, CONV_CH), 0.02),
        'ln_a_g': 1.0 + nrm(ks[16], (N_EVEN, CONV_CH), 0.02),
        'ln_a_b': nrm(ks[17], (N_EVEN, CONV_CH), 0.02),
        'sc_w': nrm(ks[18], (N_EVEN, SC_K, QKV_CH), SC_K ** -0.5),
        'a_log': jnp.log(jax.random.uniform(ks[19], (N_EVEN, DN_HEADS), jnp.float32, 1.0, 16.0)),
        'dt_bias': dt + jnp.log(-jnp.expm1(-dt)),
        'dn_norm_g': 1.0 + nrm(ks[21], (N_EVEN, DN_HEAD_DIM), 0.02),
        'w_in_odd': nrm(ks[22], (N_ODD, D_MODEL, 2 * POOL_WIDTH), D_MODEL ** -0.5),
        'w_pool': nrm(ks[23], (N_ODD, len(POOL_WINDOWS), POOL_GROUP, POOL_GROUP), POOL_GROUP ** -0.5),
        'b_pool': nrm(ks[24], (N_ODD, len(POOL_WINDOWS), POOL_GROUP), 0.02),
        'pool_scale': 1.0 + nrm(ks[25], (N_ODD, POOL_WIDTH), 0.02),
        'w_out_odd': nrm(ks[26], (N_ODD, POOL_WIDTH, D_MODEL), POOL_WIDTH ** -0.5),
        'w_xq': nrm(ks[27], (DEPTH, D_MODEL, XA_WIDTH), D_MODEL ** -0.5),
        'w_xk': nrm(ks[28], (DEPTH, D_MODEL, XA_WIDTH), D_MODEL ** -0.5),
        'w_xv': nrm(ks[29], (DEPTH, D_MODEL, XA_WIDTH), D_MODEL ** -0.5),
        'w_xo': nrm(ks[30], (DEPTH, XA_WIDTH, D_MODEL), XA_WIDTH ** -0.5),
    }


def reference(x_prompt, x_sample, state_conv_a, state_qkv_conv, state_delta, state_pool, cache_mem_k,
              cache_mem_v, mem_prompt, norm_mix, norm_xattn, norm_final, w_in_even, w_out_even, dw_w, dw_b,
              ln_a_g, ln_a_b, sc_w, a_log, dt_bias, dn_norm_g, w_in_odd, w_pool, b_pool, pool_scale,
              w_out_odd, w_xq, w_xk, w_xv, w_xo):
    prm = {'norm_mix': norm_mix, 'norm_xattn': norm_xattn, 'norm_final': norm_final,
           'w_in_even': w_in_even, 'w_out_even': w_out_even, 'dw_w': dw_w, 'dw_b': dw_b,
           'ln_a_g': ln_a_g, 'ln_a_b': ln_a_b, 'sc_w': sc_w, 'a_log': a_log, 'dt_bias': dt_bias,
           'dn_norm_g': dn_norm_g, 'w_in_odd': w_in_odd, 'w_pool': w_pool, 'b_pool': b_pool,
           'pool_scale': pool_scale, 'w_out_odd': w_out_odd, 'w_xq': w_xq, 'w_xo': w_xo}
    bp = x_prompt.shape[0]
    new_mem_k_p = jnp.einsum('bmd,lde->lbme', mem_prompt, w_xk).reshape(DEPTH, bp, N_MEM, XA_HEADS, XA_HEAD_DIM)
    new_mem_v_p = jnp.einsum('bmd,lde->lbme', mem_prompt, w_xv).reshape(DEPTH, bp, N_MEM, XA_HEADS, XA_HEAD_DIM)
    conv0 = jnp.zeros((N_EVEN, bp, CONV_K - 1, CONV_CH), x_prompt.dtype)
    qkv0 = jnp.zeros((N_EVEN, bp, SC_K - 1, QKV_CH), x_prompt.dtype)
    dn0 = jnp.zeros((N_EVEN, bp, DN_HEADS, DN_HEAD_DIM, DN_HEAD_DIM), state_delta.dtype)
    pool0 = jnp.zeros((N_ODD, bp, POOL_BUF, POOL_WIDTH), x_prompt.dtype)
    y_prompt, new_conv_a_p, new_qkv_conv_p, new_delta_p, new_pool_p = trunk(
        x_prompt, conv0, qkv0, dn0, pool0, new_mem_k_p, new_mem_v_p, prm, 0, False)
    y_sample, new_conv_a_s, new_qkv_conv_s, new_delta_s, new_pool_s = trunk(
        x_sample, state_conv_a, state_qkv_conv, state_delta, state_pool, cache_mem_k, cache_mem_v, prm,
        PAST_LEN, True)
    return (y_prompt, y_sample, new_conv_a_p, new_qkv_conv_p, new_delta_p, new_pool_p, new_mem_k_p,
            new_mem_v_p, new_conv_a_s, new_qkv_conv_s, new_delta_s, new_pool_s)
```

```python
import functools

import jax
import jax.numpy as jnp
from jax import lax
from jax.experimental import pallas as pl
from jax.experimental.pallas import tpu as pltpu

F32 = jnp.float32
BF16 = jnp.bfloat16

D_MODEL = 1024
DEPTH = 2
PAST_LEN = 16384
CONV_CH = 512
CONV_K = 31
DN_HEAD_DIM = 128
DN_HEADS = 4
DN_WIDTH = 512
QKV_CH = 1536
SC_K = 4
DN_CHUNK = 64
EVEN_IN = 3592
POOL_WINDOWS = (2, 4, 8, 16)
POOL_GROUP = 256
POOL_BUF = 15
N_MEM = 256
XA_HEADS = 4
XA_HEAD_DIM = 256
EPS = 1e-6

LANES = 128
SUBLANES = 8
VMEM_LIMIT_BYTES = 56 * 1024 * 1024

COL_GLU_VAL = 0
COL_GLU_GATE = 512
COL_GATE_A = 1024
COL_QKV = 1536
COL_Z = 3072
COL_TAIL = 3584
EVEN_IN_PAD = COL_TAIL + LANES

PROMPT_TILE = 256
CONV_HIST = 32
SC_HIST = 8
POOL_HIST = 16
CONV_ROW_BLOCK = 32
SAMPLE_ROW_BLOCK = 32
DELTA_ROW_BLOCK = 8
XATTN_ROW_BLOCK = 8


def _params(n_axes):
    return pltpu.CompilerParams(dimension_semantics=("arbitrary",) * n_axes,
                                vmem_limit_bytes=VMEM_LIMIT_BYTES)


def _dot(a, b):
    return jnp.dot(a.astype(BF16), b.astype(BF16), preferred_element_type=F32)


def _dot_nt(a, b):
    return lax.dot_general(a.astype(BF16), b.astype(BF16), (((1,), (1,)), ((), ())),
                           preferred_element_type=F32)


def _split3(x):
    x1 = x.astype(BF16)
    r1 = x - x1.astype(F32)
    x2 = r1.astype(BF16)
    x3 = (r1 - x2.astype(F32)).astype(BF16)
    return x1, x2, x3


def _dot_hi(a, b):
    a1, a2, a3 = _split3(a)
    b1, b2, b3 = _split3(b)
    d = functools.partial(jnp.dot, preferred_element_type=F32)
    return (d(a1, b1) + (d(a1, b2) + d(a2, b1)) + (d(a2, b2) + d(a1, b3) + d(a3, b1)))


def _dot_exact_lhs(a_bf, b):
    b1, b2, b3 = _split3(b)
    d = functools.partial(jnp.dot, preferred_element_type=F32)
    return d(a_bf, b1) + d(a_bf, b2) + d(a_bf, b3)


def _sigmoid(x):
    return 1.0 / (1.0 + jnp.exp(-x))


def _silu(x):
    return x * _sigmoid(x)


def _softplus(x):
    return jnp.maximum(x, 0.0) + jnp.log(1.0 + jnp.exp(-jnp.abs(x)))


def _rms(x, g):
    return x * lax.rsqrt(jnp.mean(x * x, axis=-1, keepdims=True) + EPS) * g


def _layer_norm(x, g, b):
    xc = x - jnp.mean(x, axis=-1, keepdims=True)
    return xc * lax.rsqrt(jnp.mean(xc * xc, axis=-1, keepdims=True) + EPS) * g + b


def _l2n(x):
    return x * lax.rsqrt(jnp.sum(x * x, axis=-1, keepdims=True) + EPS)


def _gate_params(tail, alog, dtb):
    lane = lax.broadcasted_iota(jnp.int32, tail.shape, 1)
    beta = _sigmoid(tail)
    g = -jnp.exp(alog) * _softplus(tail + dtb)
    return jnp.where(lane < DN_HEADS, beta, g)


def _softmax_lanes(s):
    m = jnp.max(s, axis=-1, keepdims=True)
    e = jnp.exp(s - m)
    return e / jnp.sum(e, axis=-1, keepdims=True)


def _mem_kv_kernel(x_ref, wk_ref, wv_ref, k_ref, v_ref, kb_ref, vb_ref):
    x = x_ref[...]
    k = _dot(x, wk_ref[0])
    v = _dot(x, wv_ref[0])
    k_ref[0] = k
    v_ref[0] = v
    kb_ref[0] = k.astype(BF16)
    vb_ref[0] = v.astype(BF16)


def _mem_kv(mem2d, wk, wv):
    rows = mem2d.shape[0]
    tile = 512
    f32_out = jax.ShapeDtypeStruct((DEPTH, rows, D_MODEL), F32)
    bf_out = jax.ShapeDtypeStruct((DEPTH, rows, D_MODEL), BF16)
    w_spec = pl.BlockSpec((1, D_MODEL, D_MODEL), lambda l, i: (l, 0, 0))
    o_spec = pl.BlockSpec((1, tile, D_MODEL), lambda l, i: (l, i, 0))
    return pl.pallas_call(
        _mem_kv_kernel,
        grid=(DEPTH, rows // tile),
        in_specs=[pl.BlockSpec((tile, D_MODEL), lambda l, i: (i, 0)), w_spec, w_spec],
        out_specs=[o_spec, o_spec, o_spec, o_spec],
        out_shape=[f32_out, f32_out, bf_out, bf_out],
        compiler_params=_params(2),
        name="mem_kv",
    )(mem2d, wk, wv)


def _even_pre_kernel(x_ref, g_ref, w_ref, dww_ref, dwb_ref, lng_ref, lnb_ref, scw_ref, alog_ref, dtb_ref,
                     aout_ref, q_ref, k_ref, v_ref, gb_ref, z_ref, cst_ref, qst_ref,
                     cbuf, qbuf):
    tt = PROMPT_TILE
    j = pl.program_id(1)
    last = pl.num_programs(1) - 1
    h = _rms(x_ref[0], g_ref[...]).astype(BF16)

    glu = (_dot(h, w_ref[:, COL_GLU_VAL:COL_GLU_VAL + CONV_CH])
           * _sigmoid(_dot(h, w_ref[:, COL_GLU_GATE:COL_GLU_GATE + CONV_CH])))

    @pl.when(j == 0)
    def _():
        cbuf[0:CONV_HIST, :] = jnp.zeros((CONV_HIST, CONV_CH), F32)
        qbuf[0:SC_HIST, :] = jnp.zeros((SC_HIST, QKV_CH), F32)

    @pl.when(j > 0)
    def _():
        cbuf[0:CONV_HIST, :] = cbuf[tt:tt + CONV_HIST, :]
        qbuf[0:SC_HIST, :] = qbuf[tt:tt + SC_HIST, :]

    cbuf[CONV_HIST:CONV_HIST + tt, :] = glu
    gate_a = _silu(_dot(h, w_ref[:, COL_GATE_A:COL_GATE_A + CONV_CH]))
    off = CONV_HIST - (CONV_K - 1)
    rb = CONV_ROW_BLOCK
    for r in range(tt // rb):
        acc = dww_ref[0:1, :] * cbuf[r * rb + off:r * rb + off + rb, :]
        for kk in range(1, CONV_K):
            acc = acc + dww_ref[kk:kk + 1, :] * cbuf[r * rb + off + kk:r * rb + off + kk + rb, :]
        c = _silu(_layer_norm(acc + dwb_ref[...], lng_ref[...], lnb_ref[...]))
        aout_ref[0, r * rb:(r + 1) * rb, :] = (c * gate_a[r * rb:(r + 1) * rb, :]).astype(BF16)

    qbuf[SC_HIST:SC_HIST + tt, :] = _dot(h, w_ref[:, COL_QKV:COL_QKV + QKV_CH])
    soff = SC_HIST - (SC_K - 1)
    for r in range(tt // rb):
        acc = scw_ref[0:1, :] * qbuf[r * rb + soff:r * rb + soff + rb, :]
        for kk in range(1, SC_K):
            acc = acc + scw_ref[kk:kk + 1, :] * qbuf[r * rb + soff + kk:r * rb + soff + kk + rb, :]
        acc = _silu(acc)
        for hd in range(DN_HEADS):
            lo = hd * DN_HEAD_DIM
            q_ref[0, r * rb:(r + 1) * rb, lo:lo + DN_HEAD_DIM] = (
                _l2n(acc[:, lo:lo + DN_HEAD_DIM]) * (DN_HEAD_DIM ** -0.5))
            k_ref[0, r * rb:(r + 1) * rb, lo:lo + DN_HEAD_DIM] = _l2n(
                acc[:, DN_WIDTH + lo:DN_WIDTH + lo + DN_HEAD_DIM])
        v_ref[0, r * rb:(r + 1) * rb, :] = acc[:, 2 * DN_WIDTH:]

    z_ref[0] = _dot(h, w_ref[:, COL_Z:COL_Z + DN_WIDTH])
    gb_ref[0] = _gate_params(_dot(h, w_ref[:, COL_TAIL:COL_TAIL + LANES]), alog_ref[...], dtb_ref[...])

    @pl.when(j == last)
    def _():
        cst_ref[0, 0] = cbuf[CONV_HIST + tt - (CONV_K - 1):CONV_HIST + tt, :]
        qst_ref[0, 0] = qbuf[SC_HIST + tt - (SC_K - 1):SC_HIST + tt, :]


def _even_pre(x, g, w, dww, dwb, lng, lnb, scw, alog, dtb):
    b, t, _ = x.shape
    tt = PROMPT_TILE
    const = lambda shape: pl.BlockSpec(shape, lambda i, j: (0,) * len(shape))
    tile = lambda c: pl.BlockSpec((1, tt, c), lambda i, j: (i, j, 0))
    return pl.pallas_call(
        _even_pre_kernel,
        grid=(b, t // tt),
        in_specs=[tile(D_MODEL), const((1, D_MODEL)), const((D_MODEL, EVEN_IN_PAD)), const((CONV_K, CONV_CH)),
                  const((1, CONV_CH)), const((1, CONV_CH)), const((1, CONV_CH)), const((SC_K, QKV_CH)),
                  const((1, LANES)), const((1, LANES))],
        out_specs=[tile(CONV_CH), tile(DN_WIDTH), tile(DN_WIDTH), tile(DN_WIDTH), tile(LANES), tile(DN_WIDTH),
                   pl.BlockSpec((1, 1, CONV_K - 1, CONV_CH), lambda i, j: (0, i, 0, 0)),
                   pl.BlockSpec((1, 1, SC_K - 1, QKV_CH), lambda i, j: (0, i, 0, 0))],
        out_shape=[jax.ShapeDtypeStruct((b, t, CONV_CH), BF16),
                   jax.ShapeDtypeStruct((b, t, DN_WIDTH), F32),
                   jax.ShapeDtypeStruct((b, t, DN_WIDTH), F32),
                   jax.ShapeDtypeStruct((b, t, DN_WIDTH), F32),
                   jax.ShapeDtypeStruct((b, t, LANES), F32),
                   jax.ShapeDtypeStruct((b, t, DN_WIDTH), F32),
                   jax.ShapeDtypeStruct((1, b, CONV_K - 1, CONV_CH), F32),
                   jax.ShapeDtypeStruct((1, b, SC_K - 1, QKV_CH), F32)],
        scratch_shapes=[pltpu.VMEM((CONV_HIST + tt, CONV_CH), F32),
                        pltpu.VMEM((SC_HIST + tt, QKV_CH), F32)],
        compiler_params=_params(2),
        name="even_pre",
    )(x, g, w, dww, dwb, lng, lnb, scw, alog, dtb)


def _delta_chunk_kernel(q_ref, k_ref, v_ref, gb_ref, z_ref, dng_ref, bout_ref, sout_ref, s_ref):
    tt = PROMPT_TILE
    c = DN_CHUNK
    j = pl.program_id(1)
    last = pl.num_programs(1) - 1

    @pl.when(j == 0)
    def _():
        s_ref[...] = jnp.zeros(s_ref.shape, F32)

    row = lax.broadcasted_iota(jnp.int32, (c, c), 0)
    col = lax.broadcasted_iota(jnp.int32, (c, c), 1)
    incl = row >= col
    strict = row > col
    tri = jnp.where(incl, 1.0, 0.0).astype(BF16)

    def chunk(n, carry):
        r0 = pl.multiple_of(n * c, c)
        gb = gb_ref[0, pl.ds(r0, c), :]
        for hd in range(DN_HEADS):
            lo = hd * DN_HEAD_DIM
            beta = jnp.broadcast_to(gb[:, hd:hd + 1], (c, DN_HEAD_DIM))
            g = jnp.broadcast_to(gb[:, DN_HEADS + hd:DN_HEADS + hd + 1], (c, DN_HEAD_DIM))
            gc = _dot_exact_lhs(tri, g)
            gc_row = gc.T[0:c, :]
            diff = gc[:, 0:c] - gc_row
            decay = jnp.where(incl, jnp.exp(jnp.where(incl, diff, 0.0)), 0.0)
            egc = jnp.exp(gc)
            gl = gc[c - 1:c, :]
            qh = q_ref[0, pl.ds(r0, c), lo:lo + DN_HEAD_DIM]
            kh = k_ref[0, pl.ds(r0, c), lo:lo + DN_HEAD_DIM]
            vh = v_ref[0, pl.ds(r0, c), lo:lo + DN_HEAD_DIM]
            kb = kh * beta
            a = jnp.where(strict, _dot_nt(kb, kh) * decay, 0.0)
            x = jnp.concatenate([vh * beta, kb * egc], axis=1)
            p = -a
            x = x + _dot_hi(p, x)
            for _ in range(5):
                p = _dot_hi(p, p)
                x = x + _dot_hi(p, x)
            u = x[:, 0:DN_HEAD_DIM]
            w = x[:, DN_HEAD_DIM:]
            qk = jnp.where(incl, _dot_nt(qh, kh) * decay, 0.0)
            s = s_ref[hd]
            v_new = u - _dot(w, s)
            o = _dot(qh * egc, s) + _dot(qk, v_new)
            k_dec = kh * jnp.exp(gl - gc)
            s_ref[hd] = s * jnp.exp(gl) + _dot(k_dec.T, v_new)
            zh = z_ref[0, pl.ds(r0, c), lo:lo + DN_HEAD_DIM]
            bout_ref[0, pl.ds(r0, c), lo:lo + DN_HEAD_DIM] = (_rms(o, dng_ref[...]) * _silu(zh)).astype(BF16)
        return carry

    lax.fori_loop(0, tt // c, chunk, 0)

    @pl.when(j == last)
    def _():
        sout_ref[0, 0] = s_ref[...]


def _delta_chunked(q, k, v, gb, z, dng):
    b, t, _ = q.shape
    tt = PROMPT_TILE
    tile = lambda c: pl.BlockSpec((1, tt, c), lambda i, j: (i, j, 0))
    return pl.pallas_call(
        _delta_chunk_kernel,
        grid=(b, t // tt),
        in_specs=[tile(DN_WIDTH), tile(DN_WIDTH), tile(DN_WIDTH), tile(LANES), tile(DN_WIDTH),
                  pl.BlockSpec((1, DN_HEAD_DIM), lambda i, j: (0, 0))],
        out_specs=[tile(DN_WIDTH),
                   pl.BlockSpec((1, 1, DN_HEADS, DN_HEAD_DIM, DN_HEAD_DIM), lambda i, j: (0, i, 0, 0, 0))],
        out_shape=[jax.ShapeDtypeStruct((b, t, DN_WIDTH), BF16),
                   jax.ShapeDtypeStruct((1, b, DN_HEADS, DN_HEAD_DIM, DN_HEAD_DIM), F32)],
        scratch_shapes=[pltpu.VMEM((DN_HEADS, DN_HEAD_DIM, DN_HEAD_DIM), F32)],
        compiler_params=_params(2),
        name="delta_chunked",
    )(q, k, v, gb, z, dng)


def _xattn_tile(x1, g, wq_ref, wo_ref, mk_ref, mv_ref):
    h2 = _rms(x1, g).astype(BF16)
    q = jnp.dot(h2, wq_ref[...], preferred_element_type=F32)
    outs = []
    for hd in range(XA_HEADS):
        lo = hd * XA_HEAD_DIM
        s = _dot_nt(q[:, lo:lo + XA_HEAD_DIM], mk_ref[:, lo:lo + XA_HEAD_DIM]) * (XA_HEAD_DIM ** -0.5)
        outs.append(_dot(_softmax_lanes(s), mv_ref[:, lo:lo + XA_HEAD_DIM]).astype(BF16))
    o = jnp.concatenate(outs, axis=1)
    return x1 + jnp.dot(o, wo_ref[...], preferred_element_type=F32)


def _even_post_kernel(a_ref, b_ref, x_ref, wout_ref, gx_ref, wq_ref, wo_ref, mk_ref, mv_ref, o_ref):
    mix = jnp.concatenate([a_ref[0], b_ref[0]], axis=1)
    x1 = x_ref[0] + jnp.dot(mix, wout_ref[...], preferred_element_type=F32)
    o_ref[0] = _xattn_tile(x1, gx_ref[...], wq_ref, wo_ref, mk_ref, mv_ref)


def _even_post(a, bo, x, wout, gx, wq, wo, mk, mv):
    b, t, _ = x.shape
    tt = PROMPT_TILE
    const = lambda shape: pl.BlockSpec(shape, lambda i, j: (0,) * len(shape))
    tile = lambda c: pl.BlockSpec((1, tt, c), lambda i, j: (i, j, 0))
    mem = pl.BlockSpec((N_MEM, D_MODEL), lambda i, j: (i, 0))
    return pl.pallas_call(
        _even_post_kernel,
        grid=(b, t // tt),
        in_specs=[tile(CONV_CH), tile(DN_WIDTH), tile(D_MODEL), const((D_MODEL, D_MODEL)), const((1, D_MODEL)),
                  const((D_MODEL, D_MODEL)), const((D_MODEL, D_MODEL)), mem, mem],
        out_specs=tile(D_MODEL),
        out_shape=jax.ShapeDtypeStruct((b, t, D_MODEL), F32),
        compiler_params=_params(2),
        name="even_post",
    )(a, bo, x, wout, gx, wq, wo, mk, mv)


def _pool_group_linear(pooled, wp_ref, bp_ref):
    outs = []
    for gi in range(len(POOL_WINDOWS)):
        lo = gi * POOL_GROUP
        outs.append(_dot(pooled[:, lo:lo + POOL_GROUP], wp_ref[gi]) + bp_ref[gi:gi + 1, :])
    return jnp.concatenate(outs, axis=1)


def _odd_kernel(x_ref, g_ref, win_ref, wp_ref, bp_ref, sc_ref, wout_ref, gx_ref, wq_ref, wo_ref, mk_ref, mv_ref,
                gf_ref, y_ref, pst_ref, pbuf):
    tt = PROMPT_TILE
    j = pl.program_id(1)
    last = pl.num_programs(1) - 1
    x = x_ref[0]
    h = _rms(x, g_ref[...]).astype(BF16)
    u = jnp.dot(h, win_ref[:, 0:D_MODEL], preferred_element_type=F32)
    gate = _silu(jnp.dot(h, win_ref[:, D_MODEL:], preferred_element_type=F32))

    @pl.when(j == 0)
    def _():
        pbuf[0:POOL_HIST, :] = jnp.zeros((POOL_HIST, D_MODEL), F32)

    @pl.when(j > 0)
    def _():
        pbuf[0:POOL_HIST, :] = pbuf[tt:tt + POOL_HIST, :]

    pbuf[POOL_HIST:POOL_HIST + tt, :] = u

    pos = j * tt + lax.broadcasted_iota(jnp.int32, (tt, 1), 0)
    means = []
    for gi, win in enumerate(POOL_WINDOWS):
        lo = gi * POOL_GROUP
        acc = u[:, lo:lo + POOL_GROUP]
        for d in range(1, win):
            acc = acc + pbuf[POOL_HIST - d:POOL_HIST - d + tt, lo:lo + POOL_GROUP]
        cnt = jnp.minimum(pos + 1, win).astype(F32)
        means.append(acc / cnt)
    pooled = jnp.concatenate(means, axis=1) - u
    z = _pool_group_linear(pooled, wp_ref, bp_ref) * sc_ref[...] * gate
    x1 = x + jnp.dot(z.astype(BF16), wout_ref[...], preferred_element_type=F32)
    x2 = _xattn_tile(x1, gx_ref[...], wq_ref, wo_ref, mk_ref, mv_ref)
    y_ref[0] = _rms(x2, gf_ref[...])

    @pl.when(j == last)
    def _():
        pst_ref[0, 0] = pbuf[POOL_HIST + tt - POOL_BUF:POOL_HIST + tt, :]


def _odd(x, g, win, wp, bp, sc, wout, gx, wq, wo, mk, mv, gf):
    b, t, _ = x.shape
    tt = PROMPT_TILE
    const = lambda shape: pl.BlockSpec(shape, lambda i, j: (0,) * len(shape))
    tile = lambda c: pl.BlockSpec((1, tt, c), lambda i, j: (i, j, 0))
    mem = pl.BlockSpec((N_MEM, D_MODEL), lambda i, j: (i, 0))
    ngrp = len(POOL_WINDOWS)
    return pl.pallas_call(
        _odd_kernel,
        grid=(b, t // tt),
        in_specs=[tile(D_MODEL), const((1, D_MODEL)), const((D_MODEL, 2 * D_MODEL)),
                  const((ngrp, POOL_GROUP, POOL_GROUP)), const((ngrp, POOL_GROUP)), const((1, D_MODEL)),
                  const((D_MODEL, D_MODEL)), const((1, D_MODEL)), const((D_MODEL, D_MODEL)),
                  const((D_MODEL, D_MODEL)), mem, mem, const((1, D_MODEL))],
        out_specs=[tile(D_MODEL), pl.BlockSpec((1, 1, POOL_BUF, D_MODEL), lambda i, j: (0, i, 0, 0))],
        out_shape=[jax.ShapeDtypeStruct((b, t, D_MODEL), F32),
                   jax.ShapeDtypeStruct((1, b, POOL_BUF, D_MODEL), F32)],
        scratch_shapes=[pltpu.VMEM((POOL_HIST + tt, D_MODEL), F32)],
        compiler_params=_params(2),
        name="odd_layer",
    )(x, g, win, wp, bp, sc, wout, gx, wq, wo, mk, mv, gf)


def _s_even_pre_kernel(x_ref, g_ref, w_ref, dww_ref, dwb_ref, lng_ref, lnb_ref, scw_ref, alog_ref, dtb_ref,
                       cin_ref, qin_ref,
                       aout_ref, q_ref, k_ref, v_ref, gb_ref, z_ref, cout_ref, qout_ref):
    h = _rms(x_ref[...], g_ref[...]).astype(BF16)
    glu = (_dot(h, w_ref[:, COL_GLU_VAL:COL_GLU_VAL + CONV_CH])
           * _sigmoid(_dot(h, w_ref[:, COL_GLU_GATE:COL_GLU_GATE + CONV_CH])))
    nb = CONV_K - 1
    acc = dww_ref[nb:nb + 1, :] * glu
    for kk in range(nb):
        acc = acc + dww_ref[kk:kk + 1, :] * cin_ref[:, kk * CONV_CH:(kk + 1) * CONV_CH]
    cout_ref[:, 0:(nb - 1) * CONV_CH] = cin_ref[:, CONV_CH:nb * CONV_CH]
    cout_ref[:, (nb - 1) * CONV_CH:nb * CONV_CH] = glu
    c = _silu(_layer_norm(acc + dwb_ref[...], lng_ref[...], lnb_ref[...]))
    aout_ref[...] = c * _silu(_dot(h, w_ref[:, COL_GATE_A:COL_GATE_A + CONV_CH]))

    qkv = _dot(h, w_ref[:, COL_QKV:COL_QKV + QKV_CH])
    ns = SC_K - 1
    acc = scw_ref[ns:ns + 1, :] * qkv
    for kk in range(ns):
        acc = acc + scw_ref[kk:kk + 1, :] * qin_ref[:, kk * QKV_CH:(kk + 1) * QKV_CH]
    qout_ref[:, 0:(ns - 1) * QKV_CH] = qin_ref[:, QKV_CH:ns * QKV_CH]
    qout_ref[:, (ns - 1) * QKV_CH:ns * QKV_CH] = qkv
    acc = _silu(acc)
    for hd in range(DN_HEADS):
        lo = hd * DN_HEAD_DIM
        q_ref[:, lo:lo + DN_HEAD_DIM] = _l2n(acc[:, lo:lo + DN_HEAD_DIM]) * (DN_HEAD_DIM ** -0.5)
        k_ref[:, lo:lo + DN_HEAD_DIM] = _l2n(acc[:, DN_WIDTH + lo:DN_WIDTH + lo + DN_HEAD_DIM])
    v_ref[...] = acc[:, 2 * DN_WIDTH:]
    z_ref[...] = _dot(h, w_ref[:, COL_Z:COL_Z + DN_WIDTH])
    gb_ref[...] = _gate_params(_dot(h, w_ref[:, COL_TAIL:COL_TAIL + LANES]), alog_ref[...], dtb_ref[...])


def _s_even_pre(x, g, w, dww, dwb, lng, lnb, scw, alog, dtb, cin, qin):
    n = x.shape[0]
    rbk = SAMPLE_ROW_BLOCK
    const = lambda shape: pl.BlockSpec(shape, lambda i: (0,) * len(shape))
    rows = lambda c: pl.BlockSpec((rbk, c), lambda i: (i, 0))
    cw = (CONV_K - 1) * CONV_CH
    qw = (SC_K - 1) * QKV_CH
    return pl.pallas_call(
        _s_even_pre_kernel,
        grid=(n // rbk,),
        in_specs=[rows(D_MODEL), const((1, D_MODEL)), const((D_MODEL, EVEN_IN_PAD)), const((CONV_K, CONV_CH)),
                  const((1, CONV_CH)), const((1, CONV_CH)), const((1, CONV_CH)), const((SC_K, QKV_CH)),
                  const((1, LANES)), const((1, LANES)), rows(cw), rows(qw)],
        out_specs=[rows(CONV_CH), rows(DN_WIDTH), rows(DN_WIDTH), rows(DN_WIDTH), rows(LANES), rows(DN_WIDTH),
                   rows(cw), rows(qw)],
        out_shape=[jax.ShapeDtypeStruct((n, CONV_CH), F32),
                   jax.ShapeDtypeStruct((n, DN_WIDTH), F32),
                   jax.ShapeDtypeStruct((n, DN_WIDTH), F32),
                   jax.ShapeDtypeStruct((n, DN_WIDTH), F32),
                   jax.ShapeDtypeStruct((n, LANES), F32),
                   jax.ShapeDtypeStruct((n, DN_WIDTH), F32),
                   jax.ShapeDtypeStruct((n, cw), F32),
                   jax.ShapeDtypeStruct((n, qw), F32)],
        compiler_params=_params(1),
        name="s_even_pre",
    )(x, g, w, dww, dwb, lng, lnb, scw, alog, dtb, cin, qin)


def _s_delta_kernel(q_ref, k_ref, v_ref, gb_ref, z_ref, dng_ref, sin_ref, bout_ref, sout_ref):
    dh = DN_HEAD_DIM
    for i in range(DELTA_ROW_BLOCK):
        for hd in range(DN_HEADS):
            lo = hd * dh
            kcol = jnp.broadcast_to(k_ref[i:i + 1, lo:lo + dh], (dh, dh)).T
            qcol = jnp.broadcast_to(q_ref[i:i + 1, lo:lo + dh], (dh, dh)).T
            beta = gb_ref[i:i + 1, hd:hd + 1]
            g = gb_ref[i:i + 1, DN_HEADS + hd:DN_HEADS + hd + 1]
            s = sin_ref[0, i, hd] * jnp.exp(g)
            v_new = (v_ref[i:i + 1, lo:lo + dh] - jnp.sum(kcol * s, axis=0, keepdims=True)) * beta
            s = s + kcol * v_new
            sout_ref[0, i, hd] = s
            o = jnp.sum(qcol * s, axis=0, keepdims=True)
            bout_ref[i:i + 1, lo:lo + dh] = _rms(o, dng_ref[...]) * _silu(z_ref[i:i + 1, lo:lo + dh])


def _s_delta(q, k, v, gb, z, dng, s_in):
    n = q.shape[0]
    rbk = DELTA_ROW_BLOCK
    rows = lambda c: pl.BlockSpec((rbk, c), lambda i: (i, 0))
    st = pl.BlockSpec((1, rbk, DN_HEADS, DN_HEAD_DIM, DN_HEAD_DIM), lambda i: (0, i, 0, 0, 0))
    return pl.pallas_call(
        _s_delta_kernel,
        grid=(n // rbk,),
        in_specs=[rows(DN_WIDTH), rows(DN_WIDTH), rows(DN_WIDTH), rows(LANES), rows(DN_WIDTH),
                  pl.BlockSpec((1, DN_HEAD_DIM), lambda i: (0, 0)), st],
        out_specs=[rows(DN_WIDTH), st],
        out_shape=[jax.ShapeDtypeStruct((n, DN_WIDTH), F32),
                   jax.ShapeDtypeStruct(s_in.shape, F32)],
        compiler_params=_params(1),
        name="s_delta",
    )(q, k, v, gb, z, dng, s_in)


def _s_mix_out_kernel(a_ref, b_ref, x_ref, wout_ref, gx_ref, wq_ref, x1_ref, q_ref):
    mix = jnp.concatenate([a_ref[...], b_ref[...]], axis=1).astype(BF16)
    x1 = x_ref[...] + jnp.dot(mix, wout_ref[...], preferred_element_type=F32)
    x1_ref[...] = x1
    q_ref[...] = jnp.dot(_rms(x1, gx_ref[...]).astype(BF16), wq_ref[...], preferred_element_type=F32)


def _s_mix_out(a, bo, x, wout, gx, wq):
    n = x.shape[0]
    full = lambda arr: pl.BlockSpec(arr.shape, lambda: (0,) * arr.ndim)
    args = (a, bo, x, wout, gx, wq)
    out = jax.ShapeDtypeStruct((n, D_MODEL), F32)
    return pl.pallas_call(
        _s_mix_out_kernel,
        in_specs=[full(v) for v in args],
        out_specs=[pl.BlockSpec((n, D_MODEL), lambda: (0, 0))] * 2,
        out_shape=[out, out],
        compiler_params=pltpu.CompilerParams(vmem_limit_bytes=VMEM_LIMIT_BYTES),
        name="s_mix_out",
    )(*args)


def _s_xattn_kernel(q_ref, mk_ref, mv_ref, o_ref):
    for i in range(XATTN_ROW_BLOCK):
        prod = mk_ref[0, i] * q_ref[i:i + 1, :]
        for hd in range(XA_HEADS):
            lo = hd * XA_HEAD_DIM
            s = jnp.sum(prod[:, lo:lo + XA_HEAD_DIM], axis=1, keepdims=True) * (XA_HEAD_DIM ** -0.5)
            e = jnp.exp(s - jnp.max(s, axis=0, keepdims=True))
            pr = e / jnp.sum(e, axis=0, keepdims=True)
            o_ref[i:i + 1, lo:lo + XA_HEAD_DIM] = jnp.sum(pr * mv_ref[0, i, :, lo:lo + XA_HEAD_DIM],
                                                          axis=0, keepdims=True)


def _s_xattn(q, mk, mv, layer):
    n = q.shape[0]
    rbk = XATTN_ROW_BLOCK
    mem = pl.BlockSpec((1, rbk, N_MEM, D_MODEL), lambda i: (layer, i, 0, 0))
    rows = pl.BlockSpec((rbk, D_MODEL), lambda i: (i, 0))
    return pl.pallas_call(
        _s_xattn_kernel,
        grid=(n // rbk,),
        in_specs=[rows, mem, mem],
        out_specs=rows,
        out_shape=jax.ShapeDtypeStruct((n, D_MODEL), F32),
        compiler_params=_params(1),
        name="s_xattn",
    )(q, mk, mv)


def _s_odd_kernel(x1_ref, o_ref, wo_ref, g_ref, win_ref, wp_ref, bp_ref, sc_ref, wout_ref, gx_ref, wq_ref, pin_ref,
                  x1o_ref, q_ref, pout_ref):
    x = x1_ref[...] + _dot(o_ref[...], wo_ref[...])
    h = _rms(x, g_ref[...]).astype(BF16)
    u = jnp.dot(h, win_ref[:, 0:D_MODEL], preferred_element_type=F32)
    gate = _silu(jnp.dot(h, win_ref[:, D_MODEL:], preferred_element_type=F32))
    means = []
    for gi, win in enumerate(POOL_WINDOWS):
        lo = gi * POOL_GROUP
        acc = u[:, lo:lo + POOL_GROUP]
        for d in range(1, win):
            src = (POOL_BUF - d) * D_MODEL + lo
            acc = acc + pin_ref[:, src:src + POOL_GROUP]
        means.append(acc / float(min(PAST_LEN + 1, win)))
    pooled = jnp.concatenate(means, axis=1) - u
    pout_ref[:, 0:(POOL_BUF - 1) * D_MODEL] = pin_ref[:, D_MODEL:POOL_BUF * D_MODEL]
    pout_ref[:, (POOL_BUF - 1) * D_MODEL:POOL_BUF * D_MODEL] = u
    z = _pool_group_linear(pooled, wp_ref, bp_ref) * sc_ref[...] * gate
    x1 = x + jnp.dot(z.astype(BF16), wout_ref[...], preferred_element_type=F32)
    x1o_ref[...] = x1
    q_ref[...] = jnp.dot(_rms(x1, gx_ref[...]).astype(BF16), wq_ref[...], preferred_element_type=F32)


def _s_odd(x1, o, wo, g, win, wp, bp, sc, wout, gx, wq, pin):
    n = x1.shape[0]
    rbk = SAMPLE_ROW_BLOCK
    ngrp = len(POOL_WINDOWS)
    const = lambda shape: pl.BlockSpec(shape, lambda i: (0,) * len(shape))
    rows = lambda c: pl.BlockSpec((rbk, c), lambda i: (i, 0))
    pw = POOL_BUF * D_MODEL
    out = jax.ShapeDtypeStruct((n, D_MODEL), F32)
    return pl.pallas_call(
        _s_odd_kernel,
        grid=(n // rbk,),
        in_specs=[rows(D_MODEL), rows(D_MODEL), const((D_MODEL, D_MODEL)), const((1, D_MODEL)),
                  const((D_MODEL, 2 * D_MODEL)), const((ngrp, POOL_GROUP, POOL_GROUP)), const((ngrp, POOL_GROUP)),
                  const((1, D_MODEL)), const((D_MODEL, D_MODEL)), const((1, D_MODEL)), const((D_MODEL, D_MODEL)),
                  rows(pw)],
        out_specs=[rows(D_MODEL), rows(D_MODEL), rows(pw)],
        out_shape=[out, out, jax.ShapeDtypeStruct((n, pw), F32)],
        compiler_params=_params(1),
        name="s_odd",
    )(x1, o, wo, g, win, wp, bp, sc, wout, gx, wq, pin)


def _s_final_kernel(x1_ref, o_ref, wo_ref, gf_ref, y_ref):
    y_ref[...] = _rms(x1_ref[...] + _dot(o_ref[...], wo_ref[...]), gf_ref[...])


def _s_final(x1, o, wo, gf):
    n = x1.shape[0]
    full = lambda arr: pl.BlockSpec(arr.shape, lambda: (0,) * arr.ndim)
    args = (x1, o, wo, gf)
    return pl.pallas_call(
        _s_final_kernel,
        in_specs=[full(v) for v in args],
        out_specs=pl.BlockSpec((n, D_MODEL), lambda: (0, 0)),
        out_shape=jax.ShapeDtypeStruct((n, D_MODEL), F32),
        compiler_params=pltpu.CompilerParams(vmem_limit_bytes=VMEM_LIMIT_BYTES),
        name="s_final",
    )(*args)


def _lane_pad(vec, offset):
    return jnp.pad(vec.astype(F32), (offset, LANES - offset - vec.shape[0])).reshape(1, LANES)


def kernel(x_prompt, x_sample, state_conv_a, state_qkv_conv, state_delta, state_pool, cache_mem_k, cache_mem_v, mem_prompt, norm_mix, norm_xattn, norm_final, w_in_even, w_out_even, dw_w, dw_b, ln_a_g, ln_a_b, sc_w, a_log, dt_bias, dn_norm_g, w_in_odd, w_pool, b_pool, pool_scale, w_out_odd, w_xq, w_xk, w_xv, w_xo):
    bp, t, _ = x_prompt.shape
    ns = x_sample.shape[0]
    row = lambda v: v.reshape(1, -1)

    w_in0 = jnp.pad(w_in_even[0], ((0, 0), (0, EVEN_IN_PAD - EVEN_IN))).astype(BF16)
    w_out0 = w_out_even[0].astype(BF16)
    w_in1 = w_in_odd[0].astype(BF16)
    w_pool1 = w_pool[0].astype(BF16)
    w_out1 = w_out_odd[0].astype(BF16)
    wq = w_xq.astype(BF16)
    wk = w_xk.astype(BF16)
    wv = w_xv.astype(BF16)
    wo = w_xo.astype(BF16)
    alog = _lane_pad(a_log[0], DN_HEADS)
    dtb = _lane_pad(dt_bias[0], DN_HEADS)
    even_small = (dw_w[0], row(dw_b[0]), row(ln_a_g[0]), row(ln_a_b[0]), sc_w[0], alog, dtb)
    dng = row(dn_norm_g[0])

    mk_f, mv_f, mk_b, mv_b = _mem_kv(mem_prompt.reshape(bp * N_MEM, D_MODEL), wk, wv)
    new_mem_k_p = mk_f.reshape(DEPTH, bp, N_MEM, XA_HEADS, XA_HEAD_DIM)
    new_mem_v_p = mv_f.reshape(DEPTH, bp, N_MEM, XA_HEADS, XA_HEAD_DIM)

    a_out, q, k, v, gb, z, new_conv_a_p, new_qkv_conv_p = _even_pre(x_prompt, row(norm_mix[0]), w_in0, *even_small)
    b_out, new_delta_p = _delta_chunked(q, k, v, gb, z, dng)
    x2 = _even_post(a_out, b_out, x_prompt, w_out0, row(norm_xattn[0]), wq[0], wo[0], mk_b[0], mv_b[0])
    y_prompt, new_pool_p = _odd(x2, row(norm_mix[1]), w_in1, w_pool1, b_pool[0], row(pool_scale[0]), w_out1,
                                row(norm_xattn[1]), wq[1], wo[1], mk_b[1], mv_b[1], row(norm_final))

    xs = x_sample.reshape(ns, D_MODEL)
    cin = state_conv_a.reshape(ns, (CONV_K - 1) * CONV_CH)
    qin = state_qkv_conv.reshape(ns, (SC_K - 1) * QKV_CH)
    pin = state_pool.reshape(ns, POOL_BUF * D_MODEL)
    cmk = cache_mem_k.reshape(DEPTH, ns, N_MEM, D_MODEL)
    cmv = cache_mem_v.reshape(DEPTH, ns, N_MEM, D_MODEL)

    sa, sq, sk, sv, sgb, sz, cout, qout = _s_even_pre(xs, row(norm_mix[0]), w_in0, *even_small, cin, qin)
    sb, new_delta_s = _s_delta(sq, sk, sv, sgb, sz, dng, state_delta)
    x1, xq = _s_mix_out(sa, sb, xs, w_out0, row(norm_xattn[0]), wq[0])
    o0 = _s_xattn(xq, cmk, cmv, 0)
    x1, xq, pout = _s_odd(x1, o0, wo[0], row(norm_mix[1]), w_in1, w_pool1, b_pool[0], row(pool_scale[0]), w_out1,
                          row(norm_xattn[1]), wq[1], pin)
    o1 = _s_xattn(xq, cmk, cmv, 1)
    y_sample = _s_final(x1, o1, wo[1], row(norm_final)).reshape(ns, 1, D_MODEL)

    new_conv_a_s = cout.reshape(1, ns, CONV_K - 1, CONV_CH)
    new_qkv_conv_s = qout.reshape(1, ns, SC_K - 1, QKV_CH)
    new_pool_s = pout.reshape(1, ns, POOL_BUF, D_MODEL)
    return (y_prompt, y_sample, new_conv_a_p, new_qkv_conv_p, new_delta_p, new_pool_p, new_mem_k_p,
            new_mem_v_p, new_conv_a_s, new_qkv_conv_s, new_delta_s, new_pool_s)
```

```python
import functools

import jax
import jax.numpy as jnp
from jax import lax
from jax.experimental import pallas as pl
from jax.experimental.pallas import tpu as pltpu

F32 = jnp.float32
BF16 = jnp.bfloat16

D_MODEL = 1024
DEPTH = 2
PAST_LEN = 16384
CONV_CH = 512
CONV_K = 31
DN_HEAD_DIM = 128
DN_HEADS = 4
DN_WIDTH = 512
QKV_CH = 1536
SC_K = 4
DN_CHUNK = 64
EVEN_IN = 3592
POOL_WINDOWS = (2, 4, 8, 16)
POOL_GROUP = 256
POOL_BUF = 15
N_MEM = 256
XA_HEADS = 4
XA_HEAD_DIM = 256
EPS = 1e-6

LANES = 128
SUBLANES = 8
VMEM_LIMIT_BYTES = 56 * 1024 * 1024

COL_GLU_VAL = 0
COL_GLU_GATE = 512
COL_GATE_A = 1024
COL_QKV = 1536
COL_Z = 3072
COL_TAIL = 3584
EVEN_IN_PAD = COL_TAIL + LANES

PROMPT_TILE = 256
CONV_HIST = 32
SC_HIST = 8
POOL_HIST = 16
CONV_ROW_BLOCK = 32
SAMPLE_ROW_BLOCK = 32
DELTA_ROW_BLOCK = 8
XATTN_ROW_BLOCK = 2


def _params(n_axes):
    return pltpu.CompilerParams(dimension_semantics=("arbitrary",) * n_axes,
                                vmem_limit_bytes=VMEM_LIMIT_BYTES)


def _dot(a, b):
    return jnp.dot(a.astype(BF16), b.astype(BF16), preferred_element_type=F32)


def _dot_nt(a, b):
    return lax.dot_general(a.astype(BF16), b.astype(BF16), (((1,), (1,)), ((), ())),
                           preferred_element_type=F32)


def _split3(x):
    x1 = x.astype(BF16)
    r1 = x - x1.astype(F32)
    x2 = r1.astype(BF16)
    x3 = (r1 - x2.astype(F32)).astype(BF16)
    return x1, x2, x3


def _split2(x):
    x1 = x.astype(BF16)
    return x1, (x - x1.astype(F32)).astype(BF16)


def _dot_hi(a, b):
    a1, a2 = _split2(a)
    b1, b2 = _split2(b)
    d = functools.partial(jnp.dot, preferred_element_type=F32)
    return d(a1, b1) + (d(a1, b2) + d(a2, b1))


def _dot_exact_lhs(a_bf, b):
    b1, b2, b3 = _split3(b)
    d = functools.partial(jnp.dot, preferred_element_type=F32)
    return d(a_bf, b1) + d(a_bf, b2) + d(a_bf, b3)


def _sigmoid(x):
    return 1.0 / (1.0 + jnp.exp(-x))


def _silu(x):
    return x * _sigmoid(x)


def _softplus(x):
    return jnp.maximum(x, 0.0) + jnp.log(1.0 + jnp.exp(-jnp.abs(x)))


def _rms(x, g):
    return x * lax.rsqrt(jnp.mean(x * x, axis=-1, keepdims=True) + EPS) * g


def _layer_norm(x, g, b):
    xc = x - jnp.mean(x, axis=-1, keepdims=True)
    return xc * lax.rsqrt(jnp.mean(xc * xc, axis=-1, keepdims=True) + EPS) * g + b


def _l2n(x):
    return x * lax.rsqrt(jnp.sum(x * x, axis=-1, keepdims=True) + EPS)


def _gate_params(tail, alog, dtb):
    lane = lax.broadcasted_iota(jnp.int32, tail.shape, 1)
    beta = _sigmoid(tail)
    g = -jnp.exp(alog) * _softplus(tail + dtb)
    return jnp.where(lane < DN_HEADS, beta, g)


def _softmax_lanes(s):
    m = jnp.max(s, axis=-1, keepdims=True)
    e = jnp.exp(s - m)
    return e / jnp.sum(e, axis=-1, keepdims=True)


def _mem_kv_kernel(x_ref, wk_ref, wv_ref, k_ref, v_ref, kb_ref, vb_ref):
    x = x_ref[...]
    k = _dot(x, wk_ref[0])
    v = _dot(x, wv_ref[0])
    k_ref[0] = k
    v_ref[0] = v
    kb_ref[0] = k.astype(BF16)
    vb_ref[0] = v.astype(BF16)


def _mem_kv(mem2d, wk, wv):
    rows = mem2d.shape[0]
    tile = 512
    f32_out = jax.ShapeDtypeStruct((DEPTH, rows, D_MODEL), F32)
    bf_out = jax.ShapeDtypeStruct((DEPTH, rows, D_MODEL), BF16)
    w_spec = pl.BlockSpec((1, D_MODEL, D_MODEL), lambda l, i: (l, 0, 0))
    o_spec = pl.BlockSpec((1, tile, D_MODEL), lambda l, i: (l, i, 0))
    return pl.pallas_call(
        _mem_kv_kernel,
        grid=(DEPTH, rows // tile),
        in_specs=[pl.BlockSpec((tile, D_MODEL), lambda l, i: (i, 0)), w_spec, w_spec],
        out_specs=[o_spec, o_spec, o_spec, o_spec],
        out_shape=[f32_out, f32_out, bf_out, bf_out],
        compiler_params=_params(2),
        name="mem_kv",
    )(mem2d, wk, wv)


def _even_pre_kernel(x_ref, g_ref, w_ref, dww_ref, dwb_ref, lng_ref, lnb_ref, scw_ref, alog_ref, dtb_ref,
                     aout_ref, q_ref, k_ref, v_ref, gb_ref, z_ref, cst_ref, qst_ref,
                     cbuf, qbuf):
    tt = PROMPT_TILE
    j = pl.program_id(1)
    last = pl.num_programs(1) - 1
    h = _rms(x_ref[0], g_ref[...]).astype(BF16)

    glu = (_dot(h, w_ref[:, COL_GLU_VAL:COL_GLU_VAL + CONV_CH])
           * _sigmoid(_dot(h, w_ref[:, COL_GLU_GATE:COL_GLU_GATE + CONV_CH])))

    @pl.when(j == 0)
    def _():
        cbuf[0:CONV_HIST, :] = jnp.zeros((CONV_HIST, CONV_CH), F32)
        qbuf[0:SC_HIST, :] = jnp.zeros((SC_HIST, QKV_CH), F32)

    @pl.when(j > 0)
    def _():
        cbuf[0:CONV_HIST, :] = cbuf[tt:tt + CONV_HIST, :]
        qbuf[0:SC_HIST, :] = qbuf[tt:tt + SC_HIST, :]

    cbuf[CONV_HIST:CONV_HIST + tt, :] = glu
    gate_a = _silu(_dot(h, w_ref[:, COL_GATE_A:COL_GATE_A + CONV_CH]))
    off = CONV_HIST - (CONV_K - 1)
    rb = CONV_ROW_BLOCK
    for r in range(tt // rb):
        acc = dww_ref[0:1, :] * cbuf[r * rb + off:r * rb + off + rb, :]
        for kk in range(1, CONV_K):
            acc = acc + dww_ref[kk:kk + 1, :] * cbuf[r * rb + off + kk:r * rb + off + kk + rb, :]
        c = _silu(_layer_norm(acc + dwb_ref[...], lng_ref[...], lnb_ref[...]))
        aout_ref[0, r * rb:(r + 1) * rb, :] = (c * gate_a[r * rb:(r + 1) * rb, :]).astype(BF16)

    qbuf[SC_HIST:SC_HIST + tt, :] = _dot(h, w_ref[:, COL_QKV:COL_QKV + QKV_CH])
    soff = SC_HIST - (SC_K - 1)
    for r in range(tt // rb):
        acc = scw_ref[0:1, :] * qbuf[r * rb + soff:r * rb + soff + rb, :]
        for kk in range(1, SC_K):
            acc = acc + scw_ref[kk:kk + 1, :] * qbuf[r * rb + soff + kk:r * rb + soff + kk + rb, :]
        acc = _silu(acc)
        for hd in range(DN_HEADS):
            lo = hd * DN_HEAD_DIM
            q_ref[0, r * rb:(r + 1) * rb, lo:lo + DN_HEAD_DIM] = (
                _l2n(acc[:, lo:lo + DN_HEAD_DIM]) * (DN_HEAD_DIM ** -0.5))
            k_ref[0, r * rb:(r + 1) * rb, lo:lo + DN_HEAD_DIM] = _l2n(
                acc[:, DN_WIDTH + lo:DN_WIDTH + lo + DN_HEAD_DIM])
        v_ref[0, r * rb:(r + 1) * rb, :] = acc[:, 2 * DN_WIDTH:]

    z_ref[0] = _dot(h, w_ref[:, COL_Z:COL_Z + DN_WIDTH])
    gb_ref[0] = _gate_params(_dot(h, w_ref[:, COL_TAIL:COL_TAIL + LANES]), alog_ref[...], dtb_ref[...])

    @pl.when(j == last)
    def _():
        cst_ref[0, 0] = cbuf[CONV_HIST + tt - (CONV_K - 1):CONV_HIST + tt, :]
        qst_ref[0, 0] = qbuf[SC_HIST + tt - (SC_K - 1):SC_HIST + tt, :]


def _even_pre(x, g, w, dww, dwb, lng, lnb, scw, alog, dtb):
    b, t, _ = x.shape
    tt = PROMPT_TILE
    const = lambda shape: pl.BlockSpec(shape, lambda i, j: (0,) * len(shape))
    tile = lambda c: pl.BlockSpec((1, tt, c), lambda i, j: (i, j, 0))
    return pl.pallas_call(
        _even_pre_kernel,
        grid=(b, t // tt),
        in_specs=[tile(D_MODEL), const((1, D_MODEL)), const((D_MODEL, EVEN_IN_PAD)), const((CONV_K, CONV_CH)),
                  const((1, CONV_CH)), const((1, CONV_CH)), const((1, CONV_CH)), const((SC_K, QKV_CH)),
                  const((1, LANES)), const((1, LANES))],
        out_specs=[tile(CONV_CH), tile(DN_WIDTH), tile(DN_WIDTH), tile(DN_WIDTH), tile(LANES), tile(DN_WIDTH),
                   pl.BlockSpec((1, 1, CONV_K - 1, CONV_CH), lambda i, j: (0, i, 0, 0)),
                   pl.BlockSpec((1, 1, SC_K - 1, QKV_CH), lambda i, j: (0, i, 0, 0))],
        out_shape=[jax.ShapeDtypeStruct((b, t, CONV_CH), BF16),
                   jax.ShapeDtypeStruct((b, t, DN_WIDTH), F32),
                   jax.ShapeDtypeStruct((b, t, DN_WIDTH), F32),
                   jax.ShapeDtypeStruct((b, t, DN_WIDTH), F32),
                   jax.ShapeDtypeStruct((b, t, LANES), F32),
                   jax.ShapeDtypeStruct((b, t, DN_WIDTH), F32),
                   jax.ShapeDtypeStruct((1, b, CONV_K - 1, CONV_CH), F32),
                   jax.ShapeDtypeStruct((1, b, SC_K - 1, QKV_CH), F32)],
        scratch_shapes=[pltpu.VMEM((CONV_HIST + tt, CONV_CH), F32),
                        pltpu.VMEM((SC_HIST + tt, QKV_CH), F32)],
        compiler_params=_params(2),
        name="even_pre",
    )(x, g, w, dww, dwb, lng, lnb, scw, alog, dtb)


def _delta_chunk_kernel(q_ref, k_ref, v_ref, gb_ref, z_ref, dng_ref, bout_ref, sout_ref, s_ref):
    tt = PROMPT_TILE
    c = DN_CHUNK
    j = pl.program_id(1)
    last = pl.num_programs(1) - 1

    @pl.when(j == 0)
    def _():
        s_ref[...] = jnp.zeros(s_ref.shape, F32)

    row = lax.broadcasted_iota(jnp.int32, (c, c), 0)
    col = lax.broadcasted_iota(jnp.int32, (c, c), 1)
    incl = row >= col
    strict = row > col
    tri = jnp.where(incl, 1.0, 0.0).astype(BF16)

    def chunk(n, carry):
        r0 = pl.multiple_of(n * c, c)
        gb = gb_ref[0, pl.ds(r0, c), :]
        for hd in range(DN_HEADS):
            lo = hd * DN_HEAD_DIM
            beta = jnp.broadcast_to(gb[:, hd:hd + 1], (c, DN_HEAD_DIM))
            g = jnp.broadcast_to(gb[:, DN_HEADS + hd:DN_HEADS + hd + 1], (c, DN_HEAD_DIM))
            gc = _dot_exact_lhs(tri, g)
            gc_row = gc.T[0:c, :]
            diff = gc[:, 0:c] - gc_row
            decay = jnp.where(incl, jnp.exp(jnp.where(incl, diff, 0.0)), 0.0)
            egc = jnp.exp(gc)
            gl = gc[c - 1:c, :]
            qh = q_ref[0, pl.ds(r0, c), lo:lo + DN_HEAD_DIM]
            kh = k_ref[0, pl.ds(r0, c), lo:lo + DN_HEAD_DIM]
            vh = v_ref[0, pl.ds(r0, c), lo:lo + DN_HEAD_DIM]
            kb = kh * beta
            a = jnp.where(strict, _dot_nt(kb, kh) * decay, 0.0)
            x = jnp.concatenate([vh * beta, kb * egc], axis=1)
            p = -a
            x = x + _dot_hi(p, x)
            for _ in range(5):
                p = _dot(p, p)
                x = x + _dot(p, x)
            u = x[:, 0:DN_HEAD_DIM]
            w = x[:, DN_HEAD_DIM:]
            qk = jnp.where(incl, _dot_nt(qh, kh) * decay, 0.0)
            s = s_ref[hd]
            v_new = u - _dot(w, s)
            o = _dot(qh * egc, s) + _dot(qk, v_new)
            k_dec = kh * jnp.exp(gl - gc)
            s_ref[hd] = s * jnp.exp(gl) + _dot(k_dec.T, v_new)
            zh = z_ref[0, pl.ds(r0, c), lo:lo + DN_HEAD_DIM]
            bout_ref[0, pl.ds(r0, c), lo:lo + DN_HEAD_DIM] = (_rms(o, dng_ref[...]) * _silu(zh)).astype(BF16)
        return carry

    lax.fori_loop(0, tt // c, chunk, 0)

    @pl.when(j == last)
    def _():
        sout_ref[0, 0] = s_ref[...]


def _delta_chunked(q, k, v, gb, z, dng):
    b, t, _ = q.shape
    tt = PROMPT_TILE
    tile = lambda c: pl.BlockSpec((1, tt, c), lambda i, j: (i, j, 0))
    return pl.pallas_call(
        _delta_chunk_kernel,
        grid=(b, t // tt),
        in_specs=[tile(DN_WIDTH), tile(DN_WIDTH), tile(DN_WIDTH), tile(LANES), tile(DN_WIDTH),
                  pl.BlockSpec((1, DN_HEAD_DIM), lambda i, j: (0, 0))],
        out_specs=[tile(DN_WIDTH),
                   pl.BlockSpec((1, 1, DN_HEADS, DN_HEAD_DIM, DN_HEAD_DIM), lambda i, j: (0, i, 0, 0, 0))],
        out_shape=[jax.ShapeDtypeStruct((b, t, DN_WIDTH), BF16),
                   jax.ShapeDtypeStruct((1, b, DN_HEADS, DN_HEAD_DIM, DN_HEAD_DIM), F32)],
        scratch_shapes=[pltpu.VMEM((DN_HEADS, DN_HEAD_DIM, DN_HEAD_DIM), F32)],
        compiler_params=_params(2),
        name="delta_chunked",
    )(q, k, v, gb, z, dng)


def _xattn_tile(x1, g, wq_ref, wo_ref, mk_ref, mv_ref):
    h2 = _rms(x1, g).astype(BF16)
    q = jnp.dot(h2, wq_ref[...], preferred_element_type=F32)
    outs = []
    for hd in range(XA_HEADS):
        lo = hd * XA_HEAD_DIM
        s = _dot_nt(q[:, lo:lo + XA_HEAD_DIM], mk_ref[:, lo:lo + XA_HEAD_DIM]) * (XA_HEAD_DIM ** -0.5)
        outs.append(_dot(_softmax_lanes(s), mv_ref[:, lo:lo + XA_HEAD_DIM]).astype(BF16))
    o = jnp.concatenate(outs, axis=1)
    return x1 + jnp.dot(o, wo_ref[...], preferred_element_type=F32)


def _even_post_kernel(a_ref, b_ref, x_ref, wout_ref, gx_ref, wq_ref, wo_ref, mk_ref, mv_ref, o_ref):
    mix = jnp.concatenate([a_ref[0], b_ref[0]], axis=1)
    x1 = x_ref[0] + jnp.dot(mix, wout_ref[...], preferred_element_type=F32)
    o_ref[0] = _xattn_tile(x1, gx_ref[...], wq_ref, wo_ref, mk_ref, mv_ref)


def _even_post(a, bo, x, wout, gx, wq, wo, mk, mv):
    b, t, _ = x.shape
    tt = PROMPT_TILE
    const = lambda shape: pl.BlockSpec(shape, lambda i, j: (0,) * len(shape))
    tile = lambda c: pl.BlockSpec((1, tt, c), lambda i, j: (i, j, 0))
    mem = pl.BlockSpec((N_MEM, D_MODEL), lambda i, j: (i, 0))
    return pl.pallas_call(
        _even_post_kernel,
        grid=(b, t // tt),
        in_specs=[tile(CONV_CH), tile(DN_WIDTH), tile(D_MODEL), const((D_MODEL, D_MODEL)), const((1, D_MODEL)),
                  const((D_MODEL, D_MODEL)), const((D_MODEL, D_MODEL)), mem, mem],
        out_specs=tile(D_MODEL),
        out_shape=jax.ShapeDtypeStruct((b, t, D_MODEL), F32),
        compiler_params=_params(2),
        name="even_post",
    )(a, bo, x, wout, gx, wq, wo, mk, mv)


def _pool_group_linear(pooled, wp_ref, bp_ref):
    outs = []
    for gi in range(len(POOL_WINDOWS)):
        lo = gi * POOL_GROUP
        outs.append(_dot(pooled[:, lo:lo + POOL_GROUP], wp_ref[gi]) + bp_ref[gi:gi + 1, :])
    return jnp.concatenate(outs, axis=1)


def _odd_kernel(x_ref, g_ref, win_ref, wp_ref, bp_ref, sc_ref, wout_ref, gx_ref, wq_ref, wo_ref, mk_ref, mv_ref,
                gf_ref, y_ref, pst_ref, pbuf):
    tt = PROMPT_TILE
    j = pl.program_id(1)
    last = pl.num_programs(1) - 1
    x = x_ref[0]
    h = _rms(x, g_ref[...]).astype(BF16)
    u = jnp.dot(h, win_ref[:, 0:D_MODEL], preferred_element_type=F32)
    gate = _silu(jnp.dot(h, win_ref[:, D_MODEL:], preferred_element_type=F32))

    @pl.when(j == 0)
    def _():
        pbuf[0:POOL_HIST, :] = jnp.zeros((POOL_HIST, D_MODEL), F32)

    @pl.when(j > 0)
    def _():
        pbuf[0:POOL_HIST, :] = pbuf[tt:tt + POOL_HIST, :]

    pbuf[POOL_HIST:POOL_HIST + tt, :] = u

    pos = j * tt + lax.broadcasted_iota(jnp.int32, (tt, 1), 0)
    means = []
    for gi, win in enumerate(POOL_WINDOWS):
        lo = gi * POOL_GROUP
        acc = u[:, lo:lo + POOL_GROUP]
        for d in range(1, win):
            acc = acc + pbuf[POOL_HIST - d:POOL_HIST - d + tt, lo:lo + POOL_GROUP]
        cnt = jnp.minimum(pos + 1, win).astype(F32)
        means.append(acc / cnt)
    pooled = jnp.concatenate(means, axis=1) - u
    z = _pool_group_linear(pooled, wp_ref, bp_ref) * sc_ref[...] * gate
    x1 = x + jnp.dot(z.astype(BF16), wout_ref[...], preferred_element_type=F32)
    x2 = _xattn_tile(x1, gx_ref[...], wq_ref, wo_ref, mk_ref, mv_ref)
    y_ref[0] = _rms(x2, gf_ref[...])

    @pl.when(j == last)
    def _():
        pst_ref[0, 0] = pbuf[POOL_HIST + tt - POOL_BUF:POOL_HIST + tt, :]


def _odd(x, g, win, wp, bp, sc, wout, gx, wq, wo, mk, mv, gf):
    b, t, _ = x.shape
    tt = PROMPT_TILE
    const = lambda shape: pl.BlockSpec(shape, lambda i, j: (0,) * len(shape))
    tile = lambda c: pl.BlockSpec((1, tt, c), lambda i, j: (i, j, 0))
    mem = pl.BlockSpec((N_MEM, D_MODEL), lambda i, j: (i, 0))
    ngrp = len(POOL_WINDOWS)
    return pl.pallas_call(
        _odd_kernel,
        grid=(b, t // tt),
        in_specs=[tile(D_MODEL), const((1, D_MODEL)), const((D_MODEL, 2 * D_MODEL)),
                  const((ngrp, POOL_GROUP, POOL_GROUP)), const((ngrp, POOL_GROUP)), const((1, D_MODEL)),
                  const((D_MODEL, D_MODEL)), const((1, D_MODEL)), const((D_MODEL, D_MODEL)),
                  const((D_MODEL, D_MODEL)), mem, mem, const((1, D_MODEL))],
        out_specs=[tile(D_MODEL), pl.BlockSpec((1, 1, POOL_BUF, D_MODEL), lambda i, j: (0, i, 0, 0))],
        out_shape=[jax.ShapeDtypeStruct((b, t, D_MODEL), F32),
                   jax.ShapeDtypeStruct((1, b, POOL_BUF, D_MODEL), F32)],
        scratch_shapes=[pltpu.VMEM((POOL_HIST + tt, D_MODEL), F32)],
        compiler_params=_params(2),
        name="odd_layer",
    )(x, g, win, wp, bp, sc, wout, gx, wq, wo, mk, mv, gf)


def _s_even_pre_kernel(x_ref, g_ref, w_ref, dww_ref, dwb_ref, lng_ref, lnb_ref, scw_ref, alog_ref, dtb_ref,
                       cin_ref, qin_ref,
                       aout_ref, q_ref, k_ref, v_ref, gb_ref, z_ref, cout_ref, qout_ref):
    h = _rms(x_ref[...], g_ref[...]).astype(BF16)
    glu = (_dot(h, w_ref[:, COL_GLU_VAL:COL_GLU_VAL + CONV_CH])
           * _sigmoid(_dot(h, w_ref[:, COL_GLU_GATE:COL_GLU_GATE + CONV_CH])))
    nb = CONV_K - 1
    acc = dww_ref[nb:nb + 1, :] * glu
    for kk in range(nb):
        acc = acc + dww_ref[kk:kk + 1, :] * cin_ref[:, kk * CONV_CH:(kk + 1) * CONV_CH]
    cout_ref[:, 0:(nb - 1) * CONV_CH] = cin_ref[:, CONV_CH:nb * CONV_CH]
    cout_ref[:, (nb - 1) * CONV_CH:nb * CONV_CH] = glu
    c = _silu(_layer_norm(acc + dwb_ref[...], lng_ref[...], lnb_ref[...]))
    aout_ref[...] = c * _silu(_dot(h, w_ref[:, COL_GATE_A:COL_GATE_A + CONV_CH]))

    qkv = _dot(h, w_ref[:, COL_QKV:COL_QKV + QKV_CH])
    ns = SC_K - 1
    acc = scw_ref[ns:ns + 1, :] * qkv
    for kk in range(ns):
        acc = acc + scw_ref[kk:kk + 1, :] * qin_ref[:, kk * QKV_CH:(kk + 1) * QKV_CH]
    qout_ref[:, 0:(ns - 1) * QKV_CH] = qin_ref[:, QKV_CH:ns * QKV_CH]
    qout_ref[:, (ns - 1) * QKV_CH:ns * QKV_CH] = qkv
    acc = _silu(acc)
    for hd in range(DN_HEADS):
        lo = hd * DN_HEAD_DIM
        q_ref[:, lo:lo + DN_HEAD_DIM] = _l2n(acc[:, lo:lo + DN_HEAD_DIM]) * (DN_HEAD_DIM ** -0.5)
        k_ref[:, lo:lo + DN_HEAD_DIM] = _l2n(acc[:, DN_WIDTH + lo:DN_WIDTH + lo + DN_HEAD_DIM])
    v_ref[...] = acc[:, 2 * DN_WIDTH:]
    z_ref[...] = _dot(h, w_ref[:, COL_Z:COL_Z + DN_WIDTH])
    gb_ref[...] = _gate_params(_dot(h, w_ref[:, COL_TAIL:COL_TAIL + LANES]), alog_ref[...], dtb_ref[...])


def _s_even_pre(x, g, w, dww, dwb, lng, lnb, scw, alog, dtb, cin, qin):
    n = x.shape[0]
    rbk = SAMPLE_ROW_BLOCK
    const = lambda shape: pl.BlockSpec(shape, lambda i: (0,) * len(shape))
    rows = lambda c: pl.BlockSpec((rbk, c), lambda i: (i, 0))
    cw = (CONV_K - 1) * CONV_CH
    qw = (SC_K - 1) * QKV_CH
    return pl.pallas_call(
        _s_even_pre_kernel,
        grid=(n // rbk,),
        in_specs=[rows(D_MODEL), const((1, D_MODEL)), const((D_MODEL, EVEN_IN_PAD)), const((CONV_K, CONV_CH)),
                  const((1, CONV_CH)), const((1, CONV_CH)), const((1, CONV_CH)), const((SC_K, QKV_CH)),
                  const((1, LANES)), const((1, LANES)), rows(cw), rows(qw)],
        out_specs=[rows(CONV_CH), rows(DN_WIDTH), rows(DN_WIDTH), rows(DN_WIDTH), rows(LANES), rows(DN_WIDTH),
                   rows(cw), rows(qw)],
        out_shape=[jax.ShapeDtypeStruct((n, CONV_CH), F32),
                   jax.ShapeDtypeStruct((n, DN_WIDTH), F32),
                   jax.ShapeDtypeStruct((n, DN_WIDTH), F32),
                   jax.ShapeDtypeStruct((n, DN_WIDTH), F32),
                   jax.ShapeDtypeStruct((n, LANES), F32),
                   jax.ShapeDtypeStruct((n, DN_WIDTH), F32),
                   jax.ShapeDtypeStruct((n, cw), F32),
                   jax.ShapeDtypeStruct((n, qw), F32)],
        compiler_params=_params(1),
        name="s_even_pre",
    )(x, g, w, dww, dwb, lng, lnb, scw, alog, dtb, cin, qin)


def _s_delta_kernel(q_ref, k_ref, v_ref, gb_ref, z_ref, dng_ref, sin_ref, bout_ref, sout_ref):
    dh = DN_HEAD_DIM
    for i in range(DELTA_ROW_BLOCK):
        for hd in range(DN_HEADS):
            lo = hd * dh
            kcol = jnp.broadcast_to(k_ref[i:i + 1, lo:lo + dh], (dh, dh)).T
            qcol = jnp.broadcast_to(q_ref[i:i + 1, lo:lo + dh], (dh, dh)).T
            beta = gb_ref[i:i + 1, hd:hd + 1]
            g = gb_ref[i:i + 1, DN_HEADS + hd:DN_HEADS + hd + 1]
            s = sin_ref[0, i, hd] * jnp.exp(g)
            v_new = (v_ref[i:i + 1, lo:lo + dh] - jnp.sum(kcol * s, axis=0, keepdims=True)) * beta
            s = s + kcol * v_new
            sout_ref[0, i, hd] = s
            o = jnp.sum(qcol * s, axis=0, keepdims=True)
            bout_ref[i:i + 1, lo:lo + dh] = _rms(o, dng_ref[...]) * _silu(z_ref[i:i + 1, lo:lo + dh])


def _s_delta(q, k, v, gb, z, dng, s_in):
    n = q.shape[0]
    rbk = DELTA_ROW_BLOCK
    rows = lambda c: pl.BlockSpec((rbk, c), lambda i: (i, 0))
    st = pl.BlockSpec((1, rbk, DN_HEADS, DN_HEAD_DIM, DN_HEAD_DIM), lambda i: (0, i, 0, 0, 0))
    return pl.pallas_call(
        _s_delta_kernel,
        grid=(n // rbk,),
        in_specs=[rows(DN_WIDTH), rows(DN_WIDTH), rows(DN_WIDTH), rows(LANES), rows(DN_WIDTH),
                  pl.BlockSpec((1, DN_HEAD_DIM), lambda i: (0, 0)), st],
        out_specs=[rows(DN_WIDTH), st],
        out_shape=[jax.ShapeDtypeStruct((n, DN_WIDTH), F32),
                   jax.ShapeDtypeStruct(s_in.shape, F32)],
        compiler_params=_params(1),
        name="s_delta",
    )(q, k, v, gb, z, dng, s_in)


def _s_mix_out_kernel(a_ref, b_ref, x_ref, wout_ref, gx_ref, wq_ref, x1_ref, q_ref):
    mix = jnp.concatenate([a_ref[...], b_ref[...]], axis=1).astype(BF16)
    x1 = x_ref[...] + jnp.dot(mix, wout_ref[...], preferred_element_type=F32)
    x1_ref[...] = x1
    q_ref[...] = jnp.dot(_rms(x1, gx_ref[...]).astype(BF16), wq_ref[...], preferred_element_type=F32)


def _s_mix_out(a, bo, x, wout, gx, wq):
    n = x.shape[0]
    full = lambda arr: pl.BlockSpec(arr.shape, lambda: (0,) * arr.ndim)
    args = (a, bo, x, wout, gx, wq)
    out = jax.ShapeDtypeStruct((n, D_MODEL), F32)
    return pl.pallas_call(
        _s_mix_out_kernel,
        in_specs=[full(v) for v in args],
        out_specs=[pl.BlockSpec((n, D_MODEL), lambda: (0, 0))] * 2,
        out_shape=[out, out],
        compiler_params=pltpu.CompilerParams(vmem_limit_bytes=VMEM_LIMIT_BYTES),
        name="s_mix_out",
    )(*args)


def _s_xattn_kernel(q_ref, mk_ref, mv_ref, o_ref):
    for i in range(XATTN_ROW_BLOCK):
        s = jnp.sum(mk_ref[0, i] * q_ref[i], axis=-1, keepdims=True) * (XA_HEAD_DIM ** -0.5)
        e = jnp.exp(s - jnp.max(s, axis=0, keepdims=True))
        pr = e / jnp.sum(e, axis=0, keepdims=True)
        o_ref[i] = jnp.sum(pr * mv_ref[0, i], axis=0)


def _s_xattn(q, mk, mv, layer):
    n = q.shape[0]
    rbk = XATTN_ROW_BLOCK
    mem = pl.BlockSpec((1, rbk, N_MEM, XA_HEADS, XA_HEAD_DIM), lambda i: (layer, i, 0, 0, 0))
    rows = pl.BlockSpec((rbk, XA_HEADS, XA_HEAD_DIM), lambda i: (i, 0, 0))
    return pl.pallas_call(
        _s_xattn_kernel,
        grid=(n // rbk,),
        in_specs=[rows, mem, mem],
        out_specs=rows,
        out_shape=jax.ShapeDtypeStruct((n, XA_HEADS, XA_HEAD_DIM), F32),
        compiler_params=_params(1),
        name="s_xattn",
    )(q.reshape(n, XA_HEADS, XA_HEAD_DIM), mk, mv).reshape(n, D_MODEL)


def _s_odd_kernel(x1_ref, o_ref, wo_ref, g_ref, win_ref, wp_ref, bp_ref, sc_ref, wout_ref, gx_ref, wq_ref, pin_ref,
                  x1o_ref, q_ref, pout_ref):
    x = x1_ref[...] + _dot(o_ref[...], wo_ref[...])
    h = _rms(x, g_ref[...]).astype(BF16)
    u = jnp.dot(h, win_ref[:, 0:D_MODEL], preferred_element_type=F32)
    gate = _silu(jnp.dot(h, win_ref[:, D_MODEL:], preferred_element_type=F32))
    means = []
    for gi, win in enumerate(POOL_WINDOWS):
        lo = gi * POOL_GROUP
        acc = u[:, lo:lo + POOL_GROUP]
        for d in range(1, win):
            src = (POOL_BUF - d) * D_MODEL + lo
            acc = acc + pin_ref[:, src:src + POOL_GROUP]
        means.append(acc / float(min(PAST_LEN + 1, win)))
    pooled = jnp.concatenate(means, axis=1) - u
    pout_ref[:, 0:(POOL_BUF - 1) * D_MODEL] = pin_ref[:, D_MODEL:POOL_BUF * D_MODEL]
    pout_ref[:, (POOL_BUF - 1) * D_MODEL:POOL_BUF * D_MODEL] = u
    z = _pool_group_linear(pooled, wp_ref, bp_ref) * sc_ref[...] * gate
    x1 = x + jnp.dot(z.astype(BF16), wout_ref[...], preferred_element_type=F32)
    x1o_ref[...] = x1
    q_ref[...] = jnp.dot(_rms(x1, gx_ref[...]).astype(BF16), wq_ref[...], preferred_element_type=F32)


def _s_odd(x1, o, wo, g, win, wp, bp, sc, wout, gx, wq, pin):
    n = x1.shape[0]
    rbk = SAMPLE_ROW_BLOCK
    ngrp = len(POOL_WINDOWS)
    const = lambda shape: pl.BlockSpec(shape, lambda i: (0,) * len(shape))
    rows = lambda c: pl.BlockSpec((rbk, c), lambda i: (i, 0))
    pw = POOL_BUF * D_MODEL
    out = jax.ShapeDtypeStruct((n, D_MODEL), F32)
    return pl.pallas_call(
        _s_odd_kernel,
        grid=(n // rbk,),
        in_specs=[rows(D_MODEL), rows(D_MODEL), const((D_MODEL, D_MODEL)), const((1, D_MODEL)),
                  const((D_MODEL, 2 * D_MODEL)), const((ngrp, POOL_GROUP, POOL_GROUP)), const((ngrp, POOL_GROUP)),
                  const((1, D_MODEL)), const((D_MODEL, D_MODEL)), const((1, D_MODEL)), const((D_MODEL, D_MODEL)),
                  rows(pw)],
        out_specs=[rows(D_MODEL), rows(D_MODEL), rows(pw)],
        out_shape=[out, out, jax.ShapeDtypeStruct((n, pw), F32)],
        compiler_params=_params(1),
        name="s_odd",
    )(x1, o, wo, g, win, wp, bp, sc, wout, gx, wq, pin)


def _s_final_kernel(x1_ref, o_ref, wo_ref, gf_ref, y_ref):
    y_ref[...] = _rms(x1_ref[...] + _dot(o_ref[...], wo_ref[...]), gf_ref[...])


def _s_final(x1, o, wo, gf):
    n = x1.shape[0]
    full = lambda arr: pl.BlockSpec(arr.shape, lambda: (0,) * arr.ndim)
    args = (x1, o, wo, gf)
    return pl.pallas_call(
        _s_final_kernel,
        in_specs=[full(v) for v in args],
        out_specs=pl.BlockSpec((n, D_MODEL), lambda: (0, 0)),
        out_shape=jax.ShapeDtypeStruct((n, D_MODEL), F32),
        compiler_params=pltpu.CompilerParams(vmem_limit_bytes=VMEM_LIMIT_BYTES),
        name="s_final",
    )(*args)


def _lane_pad(vec, offset):
    return jnp.pad(vec.astype(F32), (offset, LANES - offset - vec.shape[0])).reshape(1, LANES)


def kernel(x_prompt, x_sample, state_conv_a, state_qkv_conv, state_delta, state_pool, cache_mem_k, cache_mem_v, mem_prompt, norm_mix, norm_xattn, norm_final, w_in_even, w_out_even, dw_w, dw_b, ln_a_g, ln_a_b, sc_w, a_log, dt_bias, dn_norm_g, w_in_odd, w_pool, b_pool, pool_scale, w_out_odd, w_xq, w_xk, w_xv, w_xo):
    bp, t, _ = x_prompt.shape
    ns = x_sample.shape[0]
    row = lambda v: v.reshape(1, -1)

    w_in0 = jnp.pad(w_in_even[0], ((0, 0), (0, EVEN_IN_PAD - EVEN_IN))).astype(BF16)
    w_out0 = w_out_even[0].astype(BF16)
    w_in1 = w_in_odd[0].astype(BF16)
    w_pool1 = w_pool[0].astype(BF16)
    w_out1 = w_out_odd[0].astype(BF16)
    wq = w_xq.astype(BF16)
    wk = w_xk.astype(BF16)
    wv = w_xv.astype(BF16)
    wo = w_xo.astype(BF16)
    alog = _lane_pad(a_log[0], DN_HEADS)
    dtb = _lane_pad(dt_bias[0], DN_HEADS)
    even_small = (dw_w[0], row(dw_b[0]), row(ln_a_g[0]), row(ln_a_b[0]), sc_w[0], alog, dtb)
    dng = row(dn_norm_g[0])

    mk_f, mv_f, mk_b, mv_b = _mem_kv(mem_prompt.reshape(bp * N_MEM, D_MODEL), wk, wv)
    new_mem_k_p = mk_f.reshape(DEPTH, bp, N_MEM, XA_HEADS, XA_HEAD_DIM)
    new_mem_v_p = mv_f.reshape(DEPTH, bp, N_MEM, XA_HEADS, XA_HEAD_DIM)

    a_out, q, k, v, gb, z, new_conv_a_p, new_qkv_conv_p = _even_pre(x_prompt, row(norm_mix[0]), w_in0, *even_small)
    b_out, new_delta_p = _delta_chunked(q, k, v, gb, z, dng)
    x2 = _even_post(a_out, b_out, x_prompt, w_out0, row(norm_xattn[0]), wq[0], wo[0], mk_b[0], mv_b[0])
    y_prompt, new_pool_p = _odd(x2, row(norm_mix[1]), w_in1, w_pool1, b_pool[0], row(pool_scale[0]), w_out1,
                                row(norm_xattn[1]), wq[1], wo[1], mk_b[1], mv_b[1], row(norm_final))

    xs = x_sample.reshape(ns, D_MODEL)
    cin = state_conv_a.reshape(ns, (CONV_K - 1) * CONV_CH)
    qin = state_qkv_conv.reshape(ns, (SC_K - 1) * QKV_CH)
    pin = state_pool.reshape(ns, POOL_BUF * D_MODEL)

    sa, sq, sk, sv, sgb, sz, cout, qout = _s_even_pre(xs, row(norm_mix[0]), w_in0, *even_small, cin, qin)
    sb, new_delta_s = _s_delta(sq, sk, sv, sgb, sz, dng, state_delta)
    x1, xq = _s_mix_out(sa, sb, xs, w_out0, row(norm_xattn[0]), wq[0])
    o0 = _s_xattn(xq, cache_mem_k, cache_mem_v, 0)
    x1, xq, pout = _s_odd(x1, o0, wo[0], row(norm_mix[1]), w_in1, w_pool1, b_pool[0], row(pool_scale[0]), w_out1,
                          row(norm_xattn[1]), wq[1], pin)
    o1 = _s_xattn(xq, cache_mem_k, cache_mem_v, 1)
    y_sample = _s_final(x1, o1, wo[1], row(norm_final)).reshape(ns, 1, D_MODEL)

    new_conv_a_s = cout.reshape(1, ns, CONV_K - 1, CONV_CH)
    new_qkv_conv_s = qout.reshape(1, ns, SC_K - 1, QKV_CH)
    new_pool_s = pout.reshape(1, ns, POOL_BUF, D_MODEL)
    return (y_prompt, y_sample, new_conv_a_p, new_qkv_conv_p, new_delta_p, new_pool_p, new_mem_k_p,
            new_mem_v_p, new_conv_a_s, new_qkv_conv_s, new_delta_s, new_pool_s)
```

```python
import functools

import jax
import jax.numpy as jnp
from jax import lax
from jax.experimental import pallas as pl
from jax.experimental.pallas import tpu as pltpu

F32 = jnp.float32
BF16 = jnp.bfloat16

D_MODEL = 1024
DEPTH = 2
PAST_LEN = 16384
CONV_CH = 512
CONV_K = 31
DN_HEAD_DIM = 128
DN_HEADS = 4
DN_WIDTH = 512
QKV_CH = 1536
SC_K = 4
DN_CHUNK = 64
EVEN_IN = 3592
POOL_WINDOWS = (2, 4, 8, 16)
POOL_GROUP = 256
POOL_BUF = 15
N_MEM = 256
XA_HEADS = 4
XA_HEAD_DIM = 256
EPS = 1e-6

LANES = 128
SUBLANES = 8
VMEM_LIMIT_BYTES = 56 * 1024 * 1024

COL_GLU_VAL = 0
COL_GLU_GATE = 512
COL_GATE_A = 1024
COL_QKV = 1536
COL_Z = 3072
COL_TAIL = 3584
EVEN_IN_PAD = COL_TAIL + LANES

PROMPT_TILE = 256
CONV_HIST = 32
SC_HIST = 8
POOL_HIST = 16
CONV_ROW_BLOCK = 64
CONV_COL_BLOCK = 256
SAMPLE_ROW_BLOCK = 32
DELTA_ROW_BLOCK = 8
XATTN_ROW_BLOCK = 4


def _params(n_axes):
    return pltpu.CompilerParams(dimension_semantics=("arbitrary",) * n_axes,
                                vmem_limit_bytes=VMEM_LIMIT_BYTES)


def _dot(a, b):
    return jnp.dot(a.astype(BF16), b.astype(BF16), preferred_element_type=F32)


def _dot_nt(a, b):
    return lax.dot_general(a.astype(BF16), b.astype(BF16), (((1,), (1,)), ((), ())),
                           preferred_element_type=F32)


def _split3(x):
    x1 = x.astype(BF16)
    r1 = x - x1.astype(F32)
    x2 = r1.astype(BF16)
    x3 = (r1 - x2.astype(F32)).astype(BF16)
    return x1, x2, x3


def _split2(x):
    x1 = x.astype(BF16)
    return x1, (x - x1.astype(F32)).astype(BF16)


def _dot_hi(a, b):
    a1, a2 = _split2(a)
    b1, b2 = _split2(b)
    d = functools.partial(jnp.dot, preferred_element_type=F32)
    return d(a1, b1) + (d(a1, b2) + d(a2, b1))


def _bmm(a, b):
    return lax.dot_general(a.astype(BF16), b.astype(BF16), (((2,), (1,)), ((0,), (0,))),
                           preferred_element_type=F32)


def _bmm_nt(a, b):
    return lax.dot_general(a.astype(BF16), b.astype(BF16), (((2,), (2,)), ((0,), (0,))),
                           preferred_element_type=F32)


def _bmm_tn(a, b):
    return lax.dot_general(a.astype(BF16), b.astype(BF16), (((1,), (1,)), ((0,), (0,))),
                           preferred_element_type=F32)


def _bmm_hi(a, b):
    a1, a2 = _split2(a)
    b1, b2 = _split2(b)
    return _bmm(a1, b1) + (_bmm(a1, b2) + _bmm(a2, b1))


def _dot_exact_lhs(a_bf, b):
    b1, b2, b3 = _split3(b)
    d = functools.partial(jnp.dot, preferred_element_type=F32)
    return d(a_bf, b1) + d(a_bf, b2) + d(a_bf, b3)


def _sigmoid(x):
    return 1.0 / (1.0 + jnp.exp(-x))


def _silu(x):
    return x * _sigmoid(x)


def _softplus(x):
    return jnp.maximum(x, 0.0) + jnp.log(1.0 + jnp.exp(-jnp.abs(x)))


def _rms(x, g):
    return x * lax.rsqrt(jnp.mean(x * x, axis=-1, keepdims=True) + EPS) * g


def _layer_norm(x, g, b):
    xc = x - jnp.mean(x, axis=-1, keepdims=True)
    return xc * lax.rsqrt(jnp.mean(xc * xc, axis=-1, keepdims=True) + EPS) * g + b


def _l2n(x):
    return x * lax.rsqrt(jnp.sum(x * x, axis=-1, keepdims=True) + EPS)


def _gate_params(tail, alog, dtb):
    lane = lax.broadcasted_iota(jnp.int32, tail.shape, 1)
    beta = _sigmoid(tail)
    g = -jnp.exp(alog) * _softplus(tail + dtb)
    return jnp.where(lane < DN_HEADS, beta, g)


def _softmax_lanes(s):
    m = jnp.max(s, axis=-1, keepdims=True)
    e = jnp.exp(s - m)
    return e / jnp.sum(e, axis=-1, keepdims=True)


def _mem_kv_kernel(x_ref, wk_ref, wv_ref, k_ref, v_ref, kb_ref, vb_ref):
    x = x_ref[...]
    k = _dot(x, wk_ref[0])
    v = _dot(x, wv_ref[0])
    k_ref[0] = k
    v_ref[0] = v
    kb_ref[0] = k.astype(BF16)
    vb_ref[0] = v.astype(BF16)


def _mem_kv(mem2d, wk, wv):
    rows = mem2d.shape[0]
    tile = 512
    f32_out = jax.ShapeDtypeStruct((DEPTH, rows, D_MODEL), F32)
    bf_out = jax.ShapeDtypeStruct((DEPTH, rows, D_MODEL), BF16)
    w_spec = pl.BlockSpec((1, D_MODEL, D_MODEL), lambda l, i: (l, 0, 0))
    o_spec = pl.BlockSpec((1, tile, D_MODEL), lambda l, i: (l, i, 0))
    return pl.pallas_call(
        _mem_kv_kernel,
        grid=(DEPTH, rows // tile),
        in_specs=[pl.BlockSpec((tile, D_MODEL), lambda l, i: (i, 0)), w_spec, w_spec],
        out_specs=[o_spec, o_spec, o_spec, o_spec],
        out_shape=[f32_out, f32_out, bf_out, bf_out],
        compiler_params=_params(2),
        name="p_mem_kv",
    )(mem2d, wk, wv)


def _causal_conv(buf_ref, w_ref, n_taps, hist, r0, width):
    rb = CONV_ROW_BLOCK
    off = hist - (n_taps - 1)
    cols = []
    for c0 in range(0, width, CONV_COL_BLOCK):
        cs = slice(c0, c0 + CONV_COL_BLOCK)
        total = None
        for res in range(min(SUBLANES, n_taps)):
            base = (off + res) // SUBLANES * SUBLANES
            shift = off + res - base
            span = rb + (SUBLANES if shift else 0)
            part = None
            for kk in range(res, n_taps, SUBLANES):
                lo = r0 + base + kk - res
                term = w_ref[kk:kk + 1, cs] * buf_ref[lo:lo + span, cs]
                part = term if part is None else part + term
            part = part[shift:shift + rb, :]
            total = part if total is None else total + part
        cols.append(total)
    return jnp.concatenate(cols, axis=1)


def _even_pre_kernel(x_ref, g_ref, w_ref, dww_ref, dwb_ref, lng_ref, lnb_ref, scw_ref, alog_ref, dtb_ref,
                     aout_ref, q_ref, k_ref, v_ref, gb_ref, z_ref, cst_ref, qst_ref,
                     cbuf, qbuf):
    tt = PROMPT_TILE
    j = pl.program_id(1)
    last = pl.num_programs(1) - 1
    h = _rms(x_ref[0], g_ref[...]).astype(BF16)

    glu = (_dot(h, w_ref[:, COL_GLU_VAL:COL_GLU_VAL + CONV_CH])
           * _sigmoid(_dot(h, w_ref[:, COL_GLU_GATE:COL_GLU_GATE + CONV_CH])))

    @pl.when(j == 0)
    def _():
        cbuf[0:CONV_HIST, :] = jnp.zeros((CONV_HIST, CONV_CH), F32)
        qbuf[0:SC_HIST, :] = jnp.zeros((SC_HIST, QKV_CH), F32)

    @pl.when(j > 0)
    def _():
        cbuf[0:CONV_HIST, :] = cbuf[tt:tt + CONV_HIST, :]
        qbuf[0:SC_HIST, :] = qbuf[tt:tt + SC_HIST, :]

    cbuf[CONV_HIST:CONV_HIST + tt, :] = glu
    gate_a = _silu(_dot(h, w_ref[:, COL_GATE_A:COL_GATE_A + CONV_CH]))
    for r0 in range(0, tt, CONV_ROW_BLOCK):
        acc = _causal_conv(cbuf, dww_ref, CONV_K, CONV_HIST, r0, CONV_CH)
        c = _silu(_layer_norm(acc + dwb_ref[...], lng_ref[...], lnb_ref[...]))
        aout_ref[0, r0:r0 + CONV_ROW_BLOCK, :] = (c * gate_a[r0:r0 + CONV_ROW_BLOCK, :]).astype(BF16)

    qbuf[SC_HIST:SC_HIST + tt, :] = _dot(h, w_ref[:, COL_QKV:COL_QKV + QKV_CH])
    for r0 in range(0, tt, CONV_ROW_BLOCK):
        rows = slice(r0, r0 + CONV_ROW_BLOCK)
        acc = _silu(_causal_conv(qbuf, scw_ref, SC_K, SC_HIST, r0, QKV_CH))
        for hd in range(DN_HEADS):
            lo = hd * DN_HEAD_DIM
            q_ref[0, rows, lo:lo + DN_HEAD_DIM] = _l2n(acc[:, lo:lo + DN_HEAD_DIM]) * (DN_HEAD_DIM ** -0.5)
            k_ref[0, rows, lo:lo + DN_HEAD_DIM] = _l2n(acc[:, DN_WIDTH + lo:DN_WIDTH + lo + DN_HEAD_DIM])
        v_ref[0, rows, :] = acc[:, 2 * DN_WIDTH:]

    z_ref[0] = _dot(h, w_ref[:, COL_Z:COL_Z + DN_WIDTH])
    gb_ref[0] = _gate_params(_dot(h, w_ref[:, COL_TAIL:COL_TAIL + LANES]), alog_ref[...], dtb_ref[...])

    @pl.when(j == last)
    def _():
        cst_ref[0, 0] = cbuf[CONV_HIST + tt - (CONV_K - 1):CONV_HIST + tt, :]
        qst_ref[0, 0] = qbuf[SC_HIST + tt - (SC_K - 1):SC_HIST + tt, :]


def _even_pre(x, g, w, dww, dwb, lng, lnb, scw, alog, dtb):
    b, t, _ = x.shape
    tt = PROMPT_TILE
    const = lambda shape: pl.BlockSpec(shape, lambda i, j: (0,) * len(shape))
    tile = lambda c: pl.BlockSpec((1, tt, c), lambda i, j: (i, j, 0))
    return pl.pallas_call(
        _even_pre_kernel,
        grid=(b, t // tt),
        in_specs=[tile(D_MODEL), const((1, D_MODEL)), const((D_MODEL, EVEN_IN_PAD)), const((CONV_K, CONV_CH)),
                  const((1, CONV_CH)), const((1, CONV_CH)), const((1, CONV_CH)), const((SC_K, QKV_CH)),
                  const((1, LANES)), const((1, LANES))],
        out_specs=[tile(CONV_CH), tile(DN_WIDTH), tile(DN_WIDTH), tile(DN_WIDTH), tile(LANES), tile(DN_WIDTH),
                   pl.BlockSpec((1, 1, CONV_K - 1, CONV_CH), lambda i, j: (0, i, 0, 0)),
                   pl.BlockSpec((1, 1, SC_K - 1, QKV_CH), lambda i, j: (0, i, 0, 0))],
        out_shape=[jax.ShapeDtypeStruct((b, t, CONV_CH), BF16),
                   jax.ShapeDtypeStruct((b, t, DN_WIDTH), F32),
                   jax.ShapeDtypeStruct((b, t, DN_WIDTH), F32),
                   jax.ShapeDtypeStruct((b, t, DN_WIDTH), F32),
                   jax.ShapeDtypeStruct((b, t, LANES), F32),
                   jax.ShapeDtypeStruct((b, t, DN_WIDTH), F32),
                   jax.ShapeDtypeStruct((1, b, CONV_K - 1, CONV_CH), F32),
                   jax.ShapeDtypeStruct((1, b, SC_K - 1, QKV_CH), F32)],
        scratch_shapes=[pltpu.VMEM((CONV_HIST + tt, CONV_CH), F32),
                        pltpu.VMEM((SC_HIST + tt, QKV_CH), F32)],
        compiler_params=_params(2),
        name="p_even_pre",
    )(x, g, w, dww, dwb, lng, lnb, scw, alog, dtb)


def _delta_prep_kernel(q_ref, k_ref, v_ref, gb_ref, u_ref, wq_ref, kd_ref, qk_ref):
    tt = PROMPT_TILE
    c = DN_CHUNK
    dh = DN_HEAD_DIM
    nch = tt // c
    row = lax.broadcasted_iota(jnp.int32, (c, c), 0)
    col = lax.broadcasted_iota(jnp.int32, (c, c), 1)
    incl = (row >= col)[None]
    strict = (row > col)[None]
    eye = jnp.where(row == col, 1.0, 0.0)[None]
    trow = lax.broadcasted_iota(jnp.int32, (tt, tt), 0)
    tcol = lax.broadcasted_iota(jnp.int32, (tt, tt), 1)
    same_chunk = jnp.right_shift(trow, 6) == jnp.right_shift(tcol, 6)
    tri = jnp.where((trow >= tcol) & same_chunk, 1.0, 0.0).astype(BF16)

    def lane_rep(col0):
        return jnp.concatenate([jnp.broadcast_to(gb[:, col0 + hd:col0 + hd + 1], (tt, dh))
                                for hd in range(DN_HEADS)], axis=1)

    def chunks(x):
        return jnp.concatenate([x[:, hd * dh:(hd + 1) * dh].reshape(nch, c, dh) for hd in range(DN_HEADS)], axis=0)

    gb = gb_ref[0]
    gc4 = _dot_exact_lhs(tri, lane_rep(DN_HEADS))
    gc = chunks(gc4)
    gc_rows = []
    for hd in range(DN_HEADS):
        gc_t = gc4[:, hd * dh:(hd + 1) * dh].T
        gc_rows += [gc_t[0:c, n * c:(n + 1) * c][None] for n in range(nch)]
    gc_row = jnp.concatenate(gc_rows, axis=0)
    beta = chunks(lane_rep(0))
    q = chunks(q_ref[0])
    k = chunks(k_ref[0])
    v = chunks(v_ref[0])
    egc = jnp.exp(gc)
    kb = k * beta
    decay = jnp.where(incl, jnp.exp(jnp.where(incl, gc[:, :, 0:c] - gc_row, 0.0)), 0.0)
    a = jnp.where(strict, _bmm_nt(kb, k) * decay, 0.0)
    rhs = jnp.concatenate([v * beta, kb * egc], axis=2)
    p = -a
    t_inv = eye + p
    for _ in range(5):
        p = _bmm(p, p)
        t_inv = t_inv + _bmm(p, t_inv)
    x0 = _bmm(t_inv, rhs)
    rho = rhs - x0 - _bmm_hi(a, x0)
    x = x0 + _bmm(t_inv, rho)
    k_dec = k * jnp.exp(gc[:, c - 1:c, :] - gc)
    q_dec = q * egc
    qk = jnp.where(incl, _bmm_nt(q, k) * decay, 0.0)
    for hd in range(DN_HEADS):
        lo = hd * dh
        hs = slice(hd * nch, (hd + 1) * nch)
        u_ref[0, :, lo:lo + dh] = x[hs, :, 0:dh].reshape(tt, dh)
        wq_ref[0, :, 0:c, lo:lo + dh] = x[hs, :, dh:].astype(BF16)
        wq_ref[0, :, c:2 * c, lo:lo + dh] = q_dec[hs].astype(BF16)
        kd_ref[0, :, lo:lo + dh] = k_dec[hs].reshape(tt, dh).astype(BF16)
        qk_ref[0, hd] = qk[hs].reshape(tt, c).astype(BF16)


def _delta_prep(q, k, v, gb):
    b, t, _ = q.shape
    tt = PROMPT_TILE
    c = DN_CHUNK
    tile = lambda ch: pl.BlockSpec((1, tt, ch), lambda i, j: (i, j, 0))
    return pl.pallas_call(
        _delta_prep_kernel,
        grid=(b, t // tt),
        in_specs=[tile(DN_WIDTH), tile(DN_WIDTH), tile(DN_WIDTH), tile(LANES)],
        out_specs=[tile(DN_WIDTH),
                   pl.BlockSpec((1, tt // c, 2 * c, DN_WIDTH), lambda i, j: (i, j, 0, 0)),
                   tile(DN_WIDTH),
                   pl.BlockSpec((1, DN_HEADS, tt, c), lambda i, j: (i, 0, j, 0))],
        out_shape=[jax.ShapeDtypeStruct((b, t, DN_WIDTH), F32),
                   jax.ShapeDtypeStruct((b, t // c, 2 * c, DN_WIDTH), BF16),
                   jax.ShapeDtypeStruct((b, t, DN_WIDTH), BF16),
                   jax.ShapeDtypeStruct((b, DN_HEADS, t, c), BF16)],
        compiler_params=_params(2),
        name="p_delta_prep",
    )(q, k, v, gb)


def _delta_scan_kernel(u_ref, wq_ref, kd_ref, qk_ref, gb_ref, z_ref, dng_ref, bout_ref, sout_ref, s_ref):
    c = DN_CHUNK
    dh = DN_HEAD_DIM
    nb = u_ref.shape[0]
    n = pl.program_id(0)
    last = pl.num_programs(0) - 1

    @pl.when(n == 0)
    def _():
        s_ref[...] = jnp.zeros(s_ref.shape, F32)

    def heads(ref):
        return jnp.concatenate([ref[:, :, hd * dh:(hd + 1) * dh] for hd in range(DN_HEADS)], axis=0)

    g_tot = jnp.sum(gb_ref[...], axis=1, keepdims=True)
    g_last = jnp.exp(jnp.concatenate(
        [jnp.broadcast_to(g_tot[:, :, DN_HEADS + hd:DN_HEADS + hd + 1], (nb, 1, dh)) for hd in range(DN_HEADS)],
        axis=0))
    wq = jnp.concatenate([wq_ref[:, 0, :, hd * dh:(hd + 1) * dh] for hd in range(DN_HEADS)], axis=0)
    qk = jnp.concatenate([qk_ref[:, hd] for hd in range(DN_HEADS)], axis=0)
    s = s_ref[...]
    ws = _bmm(wq, s)
    v_new = heads(u_ref) - ws[:, 0:c]
    o = ws[:, c:] + _bmm(qk, v_new)
    s_ref[...] = s * g_last + _bmm_tn(heads(kd_ref), v_new)
    out = (_rms(o, dng_ref[...]) * _silu(heads(z_ref))).astype(BF16)
    for hd in range(DN_HEADS):
        bout_ref[:, :, hd * dh:(hd + 1) * dh] = out[hd * nb:(hd + 1) * nb]

    @pl.when(n == last)
    def _():
        for hd in range(DN_HEADS):
            sout_ref[0, :, hd] = s_ref[hd * nb:(hd + 1) * nb]


def _delta_scan(u, wq, kd, qk, gb, z, dng):
    b, t, _ = u.shape
    c = DN_CHUNK
    rows = lambda ch: pl.BlockSpec((b, c, ch), lambda n: (0, n, 0))
    return pl.pallas_call(
        _delta_scan_kernel,
        grid=(t // c,),
        in_specs=[rows(DN_WIDTH),
                  pl.BlockSpec((b, 1, 2 * c, DN_WIDTH), lambda n: (0, n, 0, 0)),
                  rows(DN_WIDTH),
                  pl.BlockSpec((b, DN_HEADS, c, c), lambda n: (0, 0, n, 0)),
                  rows(LANES), rows(DN_WIDTH),
                  pl.BlockSpec((1, DN_HEAD_DIM), lambda n: (0, 0))],
        out_specs=[rows(DN_WIDTH),
                   pl.BlockSpec((1, b, DN_HEADS, DN_HEAD_DIM, DN_HEAD_DIM), lambda n: (0, 0, 0, 0, 0))],
        out_shape=[jax.ShapeDtypeStruct((b, t, DN_WIDTH), BF16),
                   jax.ShapeDtypeStruct((1, b, DN_HEADS, DN_HEAD_DIM, DN_HEAD_DIM), F32)],
        scratch_shapes=[pltpu.VMEM((DN_HEADS * b, DN_HEAD_DIM, DN_HEAD_DIM), F32)],
        compiler_params=_params(1),
        name="p_delta_scan",
    )(u, wq, kd, qk, gb, z, dng)


def _xattn_tile(x1, g, wq_ref, wo_ref, mk_ref, mv_ref):
    h2 = _rms(x1, g).astype(BF16)
    q = jnp.dot(h2, wq_ref[...], preferred_element_type=F32)
    outs = []
    for hd in range(XA_HEADS):
        lo = hd * XA_HEAD_DIM
        s = _dot_nt(q[:, lo:lo + XA_HEAD_DIM], mk_ref[:, lo:lo + XA_HEAD_DIM]) * (XA_HEAD_DIM ** -0.5)
        outs.append(_dot(_softmax_lanes(s), mv_ref[:, lo:lo + XA_HEAD_DIM]).astype(BF16))
    o = jnp.concatenate(outs, axis=1)
    return x1 + jnp.dot(o, wo_ref[...], preferred_element_type=F32)


def _even_post_kernel(a_ref, b_ref, x_ref, wout_ref, gx_ref, wq_ref, wo_ref, mk_ref, mv_ref, o_ref):
    mix = jnp.concatenate([a_ref[0], b_ref[0]], axis=1)
    x1 = x_ref[0] + jnp.dot(mix, wout_ref[...], preferred_element_type=F32)
    o_ref[0] = _xattn_tile(x1, gx_ref[...], wq_ref, wo_ref, mk_ref, mv_ref)


def _even_post(a, bo, x, wout, gx, wq, wo, mk, mv):
    b, t, _ = x.shape
    tt = PROMPT_TILE
    const = lambda shape: pl.BlockSpec(shape, lambda i, j: (0,) * len(shape))
    tile = lambda c: pl.BlockSpec((1, tt, c), lambda i, j: (i, j, 0))
    mem = pl.BlockSpec((N_MEM, D_MODEL), lambda i, j: (i, 0))
    return pl.pallas_call(
        _even_post_kernel,
        grid=(b, t // tt),
        in_specs=[tile(CONV_CH), tile(DN_WIDTH), tile(D_MODEL), const((D_MODEL, D_MODEL)), const((1, D_MODEL)),
                  const((D_MODEL, D_MODEL)), const((D_MODEL, D_MODEL)), mem, mem],
        out_specs=tile(D_MODEL),
        out_shape=jax.ShapeDtypeStruct((b, t, D_MODEL), F32),
        compiler_params=_params(2),
        name="p_even_post",
    )(a, bo, x, wout, gx, wq, wo, mk, mv)


def _pool_group_linear(pooled, wp_ref, bp_ref):
    outs = []
    for gi in range(len(POOL_WINDOWS)):
        lo = gi * POOL_GROUP
        outs.append(_dot(pooled[:, lo:lo + POOL_GROUP], wp_ref[gi]) + bp_ref[gi:gi + 1, :])
    return jnp.concatenate(outs, axis=1)


def _odd_kernel(x_ref, g_ref, win_ref, wp_ref, bp_ref, sc_ref, wout_ref, gx_ref, wq_ref, wo_ref, mk_ref, mv_ref,
                gf_ref, y_ref, pst_ref, pbuf):
    tt = PROMPT_TILE
    j = pl.program_id(1)
    last = pl.num_programs(1) - 1
    x = x_ref[0]
    h = _rms(x, g_ref[...]).astype(BF16)
    u = jnp.dot(h, win_ref[:, 0:D_MODEL], preferred_element_type=F32)
    gate = _silu(jnp.dot(h, win_ref[:, D_MODEL:], preferred_element_type=F32))

    @pl.when(j == 0)
    def _():
        pbuf[0:POOL_HIST, :] = jnp.zeros((POOL_HIST, D_MODEL), F32)

    @pl.when(j > 0)
    def _():
        pbuf[0:POOL_HIST, :] = pbuf[tt:tt + POOL_HIST, :]

    pbuf[POOL_HIST:POOL_HIST + tt, :] = u

    pos = j * tt + lax.broadcasted_iota(jnp.int32, (tt, 1), 0)
    means = []
    for gi, win in enumerate(POOL_WINDOWS):
        lo = gi * POOL_GROUP
        acc = u[:, lo:lo + POOL_GROUP]
        for d in range(1, win):
            acc = acc + pbuf[POOL_HIST - d:POOL_HIST - d + tt, lo:lo + POOL_GROUP]
        cnt = jnp.minimum(pos + 1, win).astype(F32)
        means.append(acc / cnt)
    pooled = jnp.concatenate(means, axis=1) - u
    z = _pool_group_linear(pooled, wp_ref, bp_ref) * sc_ref[...] * gate
    x1 = x + jnp.dot(z.astype(BF16), wout_ref[...], preferred_element_type=F32)
    x2 = _xattn_tile(x1, gx_ref[...], wq_ref, wo_ref, mk_ref, mv_ref)
    y_ref[0] = _rms(x2, gf_ref[...])

    @pl.when(j == last)
    def _():
        pst_ref[0, 0] = pbuf[POOL_HIST + tt - POOL_BUF:POOL_HIST + tt, :]


def _odd(x, g, win, wp, bp, sc, wout, gx, wq, wo, mk, mv, gf):
    b, t, _ = x.shape
    tt = PROMPT_TILE
    const = lambda shape: pl.BlockSpec(shape, lambda i, j: (0,) * len(shape))
    tile = lambda c: pl.BlockSpec((1, tt, c), lambda i, j: (i, j, 0))
    mem = pl.BlockSpec((N_MEM, D_MODEL), lambda i, j: (i, 0))
    ngrp = len(POOL_WINDOWS)
    return pl.pallas_call(
        _odd_kernel,
        grid=(b, t // tt),
        in_specs=[tile(D_MODEL), const((1, D_MODEL)), const((D_MODEL, 2 * D_MODEL)),
                  const((ngrp, POOL_GROUP, POOL_GROUP)), const((ngrp, POOL_GROUP)), const((1, D_MODEL)),
                  const((D_MODEL, D_MODEL)), const((1, D_MODEL)), const((D_MODEL, D_MODEL)),
                  const((D_MODEL, D_MODEL)), mem, mem, const((1, D_MODEL))],
        out_specs=[tile(D_MODEL), pl.BlockSpec((1, 1, POOL_BUF, D_MODEL), lambda i, j: (0, i, 0, 0))],
        out_shape=[jax.ShapeDtypeStruct((b, t, D_MODEL), F32),
                   jax.ShapeDtypeStruct((1, b, POOL_BUF, D_MODEL), F32)],
        scratch_shapes=[pltpu.VMEM((POOL_HIST + tt, D_MODEL), F32)],
        compiler_params=_params(2),
        name="p_odd_layer",
    )(x, g, win, wp, bp, sc, wout, gx, wq, wo, mk, mv, gf)


def _s_even_pre_kernel(x_ref, g_ref, w_ref, dww_ref, dwb_ref, lng_ref, lnb_ref, scw_ref, alog_ref, dtb_ref,
                       cin_ref, qin_ref,
                       aout_ref, q_ref, k_ref, v_ref, gb_ref, z_ref, cout_ref, qout_ref):
    h = _rms(x_ref[...], g_ref[...]).astype(BF16)
    glu = (_dot(h, w_ref[:, COL_GLU_VAL:COL_GLU_VAL + CONV_CH])
           * _sigmoid(_dot(h, w_ref[:, COL_GLU_GATE:COL_GLU_GATE + CONV_CH])))
    nb = CONV_K - 1
    acc = dww_ref[nb:nb + 1, :] * glu
    for kk in range(nb):
        acc = acc + dww_ref[kk:kk + 1, :] * cin_ref[:, kk * CONV_CH:(kk + 1) * CONV_CH]
    cout_ref[:, 0:(nb - 1) * CONV_CH] = cin_ref[:, CONV_CH:nb * CONV_CH]
    cout_ref[:, (nb - 1) * CONV_CH:nb * CONV_CH] = glu
    c = _silu(_layer_norm(acc + dwb_ref[...], lng_ref[...], lnb_ref[...]))
    aout_ref[...] = c * _silu(_dot(h, w_ref[:, COL_GATE_A:COL_GATE_A + CONV_CH]))

    qkv = _dot(h, w_ref[:, COL_QKV:COL_QKV + QKV_CH])
    ns = SC_K - 1
    acc = scw_ref[ns:ns + 1, :] * qkv
    for kk in range(ns):
        acc = acc + scw_ref[kk:kk + 1, :] * qin_ref[:, kk * QKV_CH:(kk + 1) * QKV_CH]
    qout_ref[:, 0:(ns - 1) * QKV_CH] = qin_ref[:, QKV_CH:ns * QKV_CH]
    qout_ref[:, (ns - 1) * QKV_CH:ns * QKV_CH] = qkv
    acc = _silu(acc)
    for hd in range(DN_HEADS):
        lo = hd * DN_HEAD_DIM
        q_ref[:, lo:lo + DN_HEAD_DIM] = _l2n(acc[:, lo:lo + DN_HEAD_DIM]) * (DN_HEAD_DIM ** -0.5)
        k_ref[:, lo:lo + DN_HEAD_DIM] = _l2n(acc[:, DN_WIDTH + lo:DN_WIDTH + lo + DN_HEAD_DIM])
    v_ref[...] = acc[:, 2 * DN_WIDTH:]
    z_ref[...] = _dot(h, w_ref[:, COL_Z:COL_Z + DN_WIDTH])
    gb_ref[...] = _gate_params(_dot(h, w_ref[:, COL_TAIL:COL_TAIL + LANES]), alog_ref[...], dtb_ref[...])


def _s_even_pre(x, g, w, dww, dwb, lng, lnb, scw, alog, dtb, cin, qin):
    n = x.shape[0]
    rbk = SAMPLE_ROW_BLOCK
    const = lambda shape: pl.BlockSpec(shape, lambda i: (0,) * len(shape))
    rows = lambda c: pl.BlockSpec((rbk, c), lambda i: (i, 0))
    cw = (CONV_K - 1) * CONV_CH
    qw = (SC_K - 1) * QKV_CH
    return pl.pallas_call(
        _s_even_pre_kernel,
        grid=(n // rbk,),
        in_specs=[rows(D_MODEL), const((1, D_MODEL)), const((D_MODEL, EVEN_IN_PAD)), const((CONV_K, CONV_CH)),
                  const((1, CONV_CH)), const((1, CONV_CH)), const((1, CONV_CH)), const((SC_K, QKV_CH)),
                  const((1, LANES)), const((1, LANES)), rows(cw), rows(qw)],
        out_specs=[rows(CONV_CH), rows(DN_WIDTH), rows(DN_WIDTH), rows(DN_WIDTH), rows(LANES), rows(DN_WIDTH),
                   rows(cw), rows(qw)],
        out_shape=[jax.ShapeDtypeStruct((n, CONV_CH), F32),
                   jax.ShapeDtypeStruct((n, DN_WIDTH), F32),
                   jax.ShapeDtypeStruct((n, DN_WIDTH), F32),
                   jax.ShapeDtypeStruct((n, DN_WIDTH), F32),
                   jax.ShapeDtypeStruct((n, LANES), F32),
                   jax.ShapeDtypeStruct((n, DN_WIDTH), F32),
                   jax.ShapeDtypeStruct((n, cw), F32),
                   jax.ShapeDtypeStruct((n, qw), F32)],
        compiler_params=_params(1),
        name="s_even_pre",
    )(x, g, w, dww, dwb, lng, lnb, scw, alog, dtb, cin, qin)


def _s_delta_kernel(q_ref, k_ref, v_ref, gb_ref, z_ref, dng_ref, sin_ref, bout_ref, sout_ref):
    dh = DN_HEAD_DIM
    chains = [(i, hd) for i in range(DELTA_ROW_BLOCK) for hd in range(DN_HEADS)]
    vec = lambda ref, i, hd: ref[i:i + 1, hd * dh:(hd + 1) * dh]
    col = lambda ref, i, hd: jnp.broadcast_to(vec(ref, i, hd), (dh, dh)).T
    kcol = [col(k_ref, i, hd) for i, hd in chains]
    qcol = [col(q_ref, i, hd) for i, hd in chains]
    s = [sin_ref[0, i, hd] * jnp.exp(gb_ref[i:i + 1, DN_HEADS + hd:DN_HEADS + hd + 1]) for i, hd in chains]
    v_new = [(vec(v_ref, i, hd) - jnp.sum(kc * sc, axis=0, keepdims=True)) * gb_ref[i:i + 1, hd:hd + 1]
             for (i, hd), kc, sc in zip(chains, kcol, s)]
    s = [sc + kc * vn for sc, kc, vn in zip(s, kcol, v_new)]
    for (i, hd), sc, qc in zip(chains, s, qcol):
        sout_ref[0, i, hd] = sc
        o = jnp.sum(qc * sc, axis=0, keepdims=True)
        bout_ref[i:i + 1, hd * dh:(hd + 1) * dh] = _rms(o, dng_ref[...]) * _silu(vec(z_ref, i, hd))


def _s_delta(q, k, v, gb, z, dng, s_in):
    n = q.shape[0]
    rbk = DELTA_ROW_BLOCK
    rows = lambda c: pl.BlockSpec((rbk, c), lambda i: (i, 0))
    st = pl.BlockSpec((1, rbk, DN_HEADS, DN_HEAD_DIM, DN_HEAD_DIM), lambda i: (0, i, 0, 0, 0))
    return pl.pallas_call(
        _s_delta_kernel,
        grid=(n // rbk,),
        in_specs=[rows(DN_WIDTH), rows(DN_WIDTH), rows(DN_WIDTH), rows(LANES), rows(DN_WIDTH),
                  pl.BlockSpec((1, DN_HEAD_DIM), lambda i: (0, 0)), st],
        out_specs=[rows(DN_WIDTH), st],
        out_shape=[jax.ShapeDtypeStruct((n, DN_WIDTH), F32),
                   jax.ShapeDtypeStruct(s_in.shape, F32)],
        compiler_params=_params(1),
        name="s_delta",
    )(q, k, v, gb, z, dng, s_in)


def _s_mix_out_kernel(a_ref, b_ref, x_ref, wout_ref, gx_ref, wq_ref, x1_ref, q_ref):
    mix = jnp.concatenate([a_ref[...], b_ref[...]], axis=1).astype(BF16)
    x1 = x_ref[...] + jnp.dot(mix, wout_ref[...], preferred_element_type=F32)
    x1_ref[...] = x1
    q_ref[...] = jnp.dot(_rms(x1, gx_ref[...]).astype(BF16), wq_ref[...], preferred_element_type=F32)


def _s_mix_out(a, bo, x, wout, gx, wq):
    n = x.shape[0]
    full = lambda arr: pl.BlockSpec(arr.shape, lambda: (0,) * arr.ndim)
    args = (a, bo, x, wout, gx, wq)
    out = jax.ShapeDtypeStruct((n, D_MODEL), F32)
    return pl.pallas_call(
        _s_mix_out_kernel,
        in_specs=[full(v) for v in args],
        out_specs=[pl.BlockSpec((n, D_MODEL), lambda: (0, 0))] * 2,
        out_shape=[out, out],
        compiler_params=pltpu.CompilerParams(vmem_limit_bytes=VMEM_LIMIT_BYTES),
        name="s_mix_out",
    )(*args)


def _split_head_dim(x):
    lead = x.shape[:-2]
    halves = XA_HEAD_DIM // LANES
    x = x.reshape(lead + (XA_HEADS, halves, LANES))
    x = jnp.swapaxes(x, -3, -2)
    return x.reshape(lead + (halves * XA_HEADS, LANES))


def _merge_head_dim(x):
    lead = x.shape[:-2]
    halves = XA_HEAD_DIM // LANES
    x = jnp.swapaxes(x.reshape(lead + (halves, XA_HEADS, LANES)), -3, -2)
    return x.reshape(lead + (XA_HEADS, XA_HEAD_DIM))


def _s_xattn_kernel(q_ref, mk_ref, mv_ref, o_ref):
    for i in range(XATTN_ROW_BLOCK):
        prod = mk_ref[0, i] * q_ref[i]
        prod = prod + pltpu.roll(prod, XA_HEADS, 1)
        s = jnp.sum(prod, axis=-1, keepdims=True) * (XA_HEAD_DIM ** -0.5)
        e = jnp.exp(s - jnp.max(s, axis=0, keepdims=True))
        pr = e / jnp.sum(e, axis=0, keepdims=True)
        o_ref[i] = jnp.sum(pr * mv_ref[0, i], axis=0)


def _s_xattn(q, mk, mv, layer):
    n = q.shape[0]
    rbk = XATTN_ROW_BLOCK
    sub = XA_HEADS * XA_HEAD_DIM // LANES
    mem = pl.BlockSpec((1, rbk, N_MEM, sub, LANES), lambda i: (layer, i, 0, 0, 0))
    rows = pl.BlockSpec((rbk, sub, LANES), lambda i: (i, 0, 0))
    out = pl.pallas_call(
        _s_xattn_kernel,
        grid=(n // rbk,),
        in_specs=[rows, mem, mem],
        out_specs=rows,
        out_shape=jax.ShapeDtypeStruct((n, sub, LANES), F32),
        compiler_params=_params(1),
        name="s_xattn",
    )(_split_head_dim(q.reshape(n, XA_HEADS, XA_HEAD_DIM)), _split_head_dim(mk), _split_head_dim(mv))
    return _merge_head_dim(out).reshape(n, D_MODEL)


def _s_odd_kernel(x1_ref, o_ref, wo_ref, g_ref, win_ref, wp_ref, bp_ref, sc_ref, wout_ref, gx_ref, wq_ref, pin_ref,
                  x1o_ref, q_ref, pout_ref):
    x = x1_ref[...] + _dot(o_ref[...], wo_ref[...])
    h = _rms(x, g_ref[...]).astype(BF16)
    u = jnp.dot(h, win_ref[:, 0:D_MODEL], preferred_element_type=F32)
    gate = _silu(jnp.dot(h, win_ref[:, D_MODEL:], preferred_element_type=F32))
    means = []
    for gi, win in enumerate(POOL_WINDOWS):
        lo = gi * POOL_GROUP
        acc = u[:, lo:lo + POOL_GROUP]
        for d in range(1, win):
            src = (POOL_BUF - d) * D_MODEL + lo
            acc = acc + pin_ref[:, src:src + POOL_GROUP]
        means.append(acc / float(min(PAST_LEN + 1, win)))
    pooled = jnp.concatenate(means, axis=1) - u
    pout_ref[:, 0:(POOL_BUF - 1) * D_MODEL] = pin_ref[:, D_MODEL:POOL_BUF * D_MODEL]
    pout_ref[:, (POOL_BUF - 1) * D_MODEL:POOL_BUF * D_MODEL] = u
    z = _pool_group_linear(pooled, wp_ref, bp_ref) * sc_ref[...] * gate
    x1 = x + jnp.dot(z.astype(BF16), wout_ref[...], preferred_element_type=F32)
    x1o_ref[...] = x1
    q_ref[...] = jnp.dot(_rms(x1, gx_ref[...]).astype(BF16), wq_ref[...], preferred_element_type=F32)


def _s_odd(x1, o, wo, g, win, wp, bp, sc, wout, gx, wq, pin):
    n = x1.shape[0]
    rbk = SAMPLE_ROW_BLOCK
    ngrp = len(POOL_WINDOWS)
    const = lambda shape: pl.BlockSpec(shape, lambda i: (0,) * len(shape))
    rows = lambda c: pl.BlockSpec((rbk, c), lambda i: (i, 0))
    pw = POOL_BUF * D_MODEL
    out = jax.ShapeDtypeStruct((n, D_MODEL), F32)
    return pl.pallas_call(
        _s_odd_kernel,
        grid=(n // rbk,),
        in_specs=[rows(D_MODEL), rows(D_MODEL), const((D_MODEL, D_MODEL)), const((1, D_MODEL)),
                  const((D_MODEL, 2 * D_MODEL)), const((ngrp, POOL_GROUP, POOL_GROUP)), const((ngrp, POOL_GROUP)),
                  const((1, D_MODEL)), const((D_MODEL, D_MODEL)), const((1, D_MODEL)), const((D_MODEL, D_MODEL)),
                  rows(pw)],
        out_specs=[rows(D_MODEL), rows(D_MODEL), rows(pw)],
        out_shape=[out, out, jax.ShapeDtypeStruct((n, pw), F32)],
        compiler_params=_params(1),
        name="s_odd",
    )(x1, o, wo, g, win, wp, bp, sc, wout, gx, wq, pin)


def _s_final_kernel(x1_ref, o_ref, wo_ref, gf_ref, y_ref):
    y_ref[...] = _rms(x1_ref[...] + _dot(o_ref[...], wo_ref[...]), gf_ref[...])


def _s_final(x1, o, wo, gf):
    n = x1.shape[0]
    full = lambda arr: pl.BlockSpec(arr.shape, lambda: (0,) * arr.ndim)
    args = (x1, o, wo, gf)
    return pl.pallas_call(
        _s_final_kernel,
        in_specs=[full(v) for v in args],
        out_specs=pl.BlockSpec((n, D_MODEL), lambda: (0, 0)),
        out_shape=jax.ShapeDtypeStruct((n, D_MODEL), F32),
        compiler_params=pltpu.CompilerParams(vmem_limit_bytes=VMEM_LIMIT_BYTES),
        name="s_final",
    )(*args)


def _lane_pad(vec, offset):
    return jnp.pad(vec.astype(F32), (offset, LANES - offset - vec.shape[0])).reshape(1, LANES)


def kernel(x_prompt, x_sample, state_conv_a, state_qkv_conv, state_delta, state_pool, cache_mem_k, cache_mem_v, mem_prompt, norm_mix, norm_xattn, norm_final, w_in_even, w_out_even, dw_w, dw_b, ln_a_g, ln_a_b, sc_w, a_log, dt_bias, dn_norm_g, w_in_odd, w_pool, b_pool, pool_scale, w_out_odd, w_xq, w_xk, w_xv, w_xo):
    bp, t, _ = x_prompt.shape
    ns = x_sample.shape[0]
    row = lambda v: v.reshape(1, -1)

    w_in0 = jnp.pad(w_in_even[0], ((0, 0), (0, EVEN_IN_PAD - EVEN_IN))).astype(BF16)
    w_out0 = w_out_even[0].astype(BF16)
    w_in1 = w_in_odd[0].astype(BF16)
    w_pool1 = w_pool[0].astype(BF16)
    w_out1 = w_out_odd[0].astype(BF16)
    wq = w_xq.astype(BF16)
    wk = w_xk.astype(BF16)
    wv = w_xv.astype(BF16)
    wo = w_xo.astype(BF16)
    alog = _lane_pad(a_log[0], DN_HEADS)
    dtb = _lane_pad(dt_bias[0], DN_HEADS)
    even_small = (dw_w[0], row(dw_b[0]), row(ln_a_g[0]), row(ln_a_b[0]), sc_w[0], alog, dtb)
    dng = row(dn_norm_g[0])

    mk_f, mv_f, mk_b, mv_b = _mem_kv(mem_prompt.reshape(bp * N_MEM, D_MODEL), wk, wv)
    new_mem_k_p = mk_f.reshape(DEPTH, bp, N_MEM, XA_HEADS, XA_HEAD_DIM)
    new_mem_v_p = mv_f.reshape(DEPTH, bp, N_MEM, XA_HEADS, XA_HEAD_DIM)

    a_out, q, k, v, gb, z, new_conv_a_p, new_qkv_conv_p = _even_pre(x_prompt, row(norm_mix[0]), w_in0, *even_small)
    u, wqd, kd, qk = _delta_prep(q, k, v, gb)
    b_out, new_delta_p = _delta_scan(u, wqd, kd, qk, gb, z, dng)
    x2 = _even_post(a_out, b_out, x_prompt, w_out0, row(norm_xattn[0]), wq[0], wo[0], mk_b[0], mv_b[0])
    y_prompt, new_pool_p = _odd(x2, row(norm_mix[1]), w_in1, w_pool1, b_pool[0], row(pool_scale[0]), w_out1,
                                row(norm_xattn[1]), wq[1], wo[1], mk_b[1], mv_b[1], row(norm_final))

    xs = x_sample.reshape(ns, D_MODEL)
    cin = state_conv_a.reshape(ns, (CONV_K - 1) * CONV_CH)
    qin = state_qkv_conv.reshape(ns, (SC_K - 1) * QKV_CH)
    pin = state_pool.reshape(ns, POOL_BUF * D_MODEL)

    sa, sq, sk, sv, sgb, sz, cout, qout = _s_even_pre(xs, row(norm_mix[0]), w_in0, *even_small, cin, qin)
    sb, new_delta_s = _s_delta(sq, sk, sv, sgb, sz, dng, state_delta)
    x1, xq = _s_mix_out(sa, sb, xs, w_out0, row(norm_xattn[0]), wq[0])
    o0 = _s_xattn(xq, cache_mem_k, cache_mem_v, 0)
    x1, xq, pout = _s_odd(x1, o0, wo[0], row(norm_mix[1]), w_in1, w_pool1, b_pool[0], row(pool_scale[0]), w_out1,
                          row(norm_xattn[1]), wq[1], pin)
    o1 = _s_xattn(xq, cache_mem_k, cache_mem_v, 1)
    y_sample = _s_final(x1, o1, wo[1], row(norm_final)).reshape(ns, 1, D_MODEL)

    new_conv_a_s = cout.reshape(1, ns, CONV_K - 1, CONV_CH)
    new_qkv_conv_s = qout.reshape(1, ns, SC_K - 1, QKV_CH)
    new_pool_s = pout.reshape(1, ns, POOL_BUF, D_MODEL)
    return (y_prompt, y_sample, new_conv_a_p, new_qkv_conv_p, new_delta_p, new_pool_p, new_mem_k_p,
            new_mem_v_p, new_conv_a_s, new_qkv_conv_s, new_delta_s, new_pool_s)
```

```python
import functools

import jax
import jax.numpy as jnp
from jax import lax
from jax.experimental import pallas as pl
from jax.experimental.pallas import tpu as pltpu

F32 = jnp.float32
BF16 = jnp.bfloat16

D_MODEL = 1024
DEPTH = 2
PAST_LEN = 16384
CONV_CH = 512
CONV_K = 31
DN_HEAD_DIM = 128
DN_HEADS = 4
DN_WIDTH = 512
QKV_CH = 1536
SC_K = 4
DN_CHUNK = 64
EVEN_IN = 3592
POOL_WINDOWS = (2, 4, 8, 16)
POOL_GROUP = 256
POOL_BUF = 15
N_MEM = 256
XA_HEADS = 4
XA_HEAD_DIM = 256
EPS = 1e-6

LANES = 128
SUBLANES = 8
VMEM_LIMIT_BYTES = 56 * 1024 * 1024

COL_GLU_VAL = 0
COL_GLU_GATE = 512
COL_GATE_A = 1024
COL_QKV = 1536
COL_Z = 3072
COL_TAIL = 3584
EVEN_IN_PAD = COL_TAIL + LANES

PROMPT_TILE = 256
POST_TILE = 512
CONV_HIST = 32
SC_HIST = 8
POOL_HIST = 16
CONV_ROW_BLOCK = 64
CONV_COL_BLOCK = 256
SAMPLE_ROW_BLOCK = 32
DELTA_ROW_BLOCK = 8
XATTN_ROW_BLOCK = 4


def _params(n_axes):
    return pltpu.CompilerParams(dimension_semantics=("arbitrary",) * n_axes,
                                vmem_limit_bytes=VMEM_LIMIT_BYTES)


def _dot(a, b):
    return jnp.dot(a.astype(BF16), b.astype(BF16), preferred_element_type=F32)


def _dot_nt(a, b):
    return lax.dot_general(a.astype(BF16), b.astype(BF16), (((1,), (1,)), ((), ())),
                           preferred_element_type=F32)


def _split3(x):
    x1 = x.astype(BF16)
    r1 = x - x1.astype(F32)
    x2 = r1.astype(BF16)
    x3 = (r1 - x2.astype(F32)).astype(BF16)
    return x1, x2, x3


def _split2(x):
    x1 = x.astype(BF16)
    return x1, (x - x1.astype(F32)).astype(BF16)


def _dot_hi(a, b):
    a1, a2 = _split2(a)
    b1, b2 = _split2(b)
    d = functools.partial(jnp.dot, preferred_element_type=F32)
    return d(a1, b1) + (d(a1, b2) + d(a2, b1))


def _bmm(a, b):
    return lax.dot_general(a.astype(BF16), b.astype(BF16), (((2,), (1,)), ((0,), (0,))),
                           preferred_element_type=F32)


def _bmm_nt(a, b):
    return lax.dot_general(a.astype(BF16), b.astype(BF16), (((2,), (2,)), ((0,), (0,))),
                           preferred_element_type=F32)


def _bmm_tn(a, b):
    return lax.dot_general(a.astype(BF16), b.astype(BF16), (((1,), (1,)), ((0,), (0,))),
                           preferred_element_type=F32)


def _bmm_hi(a, b):
    a1, a2 = _split2(a)
    b1, b2 = _split2(b)
    return _bmm(a1, b1) + (_bmm(a1, b2) + _bmm(a2, b1))


def _dot_exact_lhs(a_bf, b):
    b1, b2, b3 = _split3(b)
    d = functools.partial(jnp.dot, preferred_element_type=F32)
    return d(a_bf, b1) + d(a_bf, b2) + d(a_bf, b3)


def _sigmoid(x):
    return 1.0 / (1.0 + jnp.exp(-x))


def _silu(x):
    return x * _sigmoid(x)


def _softplus(x):
    return jnp.maximum(x, 0.0) + jnp.log(1.0 + jnp.exp(-jnp.abs(x)))


def _rms(x, g):
    return x * lax.rsqrt(jnp.mean(x * x, axis=-1, keepdims=True) + EPS) * g


def _layer_norm(x, g, b):
    xc = x - jnp.mean(x, axis=-1, keepdims=True)
    return xc * lax.rsqrt(jnp.mean(xc * xc, axis=-1, keepdims=True) + EPS) * g + b


def _l2n(x):
    return x * lax.rsqrt(jnp.sum(x * x, axis=-1, keepdims=True) + EPS)


def _gate_params(tail, alog, dtb):
    lane = lax.broadcasted_iota(jnp.int32, tail.shape, 1)
    beta = _sigmoid(tail)
    g = -jnp.exp(alog) * _softplus(tail + dtb)
    return jnp.where(lane < DN_HEADS, beta, g)


def _softmax_lanes(s):
    m = jnp.max(s, axis=-1, keepdims=True)
    e = jnp.exp(s - m)
    return e / jnp.sum(e, axis=-1, keepdims=True)


def _mem_kv_kernel(x_ref, wk_ref, wv_ref, k_ref, v_ref, kb_ref, vb_ref):
    x = x_ref[...]
    k = _dot(x, wk_ref[0])
    v = _dot(x, wv_ref[0])
    k_ref[0] = k
    v_ref[0] = v
    kb_ref[0] = k.astype(BF16)
    vb_ref[0] = v.astype(BF16)


def _mem_kv(mem2d, wk, wv):
    rows = mem2d.shape[0]
    tile = 512
    f32_out = jax.ShapeDtypeStruct((DEPTH, rows, D_MODEL), F32)
    bf_out = jax.ShapeDtypeStruct((DEPTH, rows, D_MODEL), BF16)
    w_spec = pl.BlockSpec((1, D_MODEL, D_MODEL), lambda l, i: (l, 0, 0))
    o_spec = pl.BlockSpec((1, tile, D_MODEL), lambda l, i: (l, i, 0))
    return pl.pallas_call(
        _mem_kv_kernel,
        grid=(DEPTH, rows // tile),
        in_specs=[pl.BlockSpec((tile, D_MODEL), lambda l, i: (i, 0)), w_spec, w_spec],
        out_specs=[o_spec, o_spec, o_spec, o_spec],
        out_shape=[f32_out, f32_out, bf_out, bf_out],
        compiler_params=_params(2),
        name="p_mem_kv",
    )(mem2d, wk, wv)


def _causal_conv(buf_ref, w_ref, n_taps, hist, r0, width):
    rb = CONV_ROW_BLOCK
    off = hist - (n_taps - 1)
    cols = []
    for c0 in range(0, width, CONV_COL_BLOCK):
        cs = slice(c0, c0 + CONV_COL_BLOCK)
        total = None
        for res in range(min(SUBLANES, n_taps)):
            base = (off + res) // SUBLANES * SUBLANES
            shift = off + res - base
            span = rb + (SUBLANES if shift else 0)
            part = None
            for kk in range(res, n_taps, SUBLANES):
                lo = r0 + base + kk - res
                rows = buf_ref[lo:lo + span, cs].reshape(span // SUBLANES, SUBLANES, CONV_COL_BLOCK)
                term = rows * w_ref[kk * SUBLANES:(kk + 1) * SUBLANES, cs][None]
                part = term if part is None else part + term
            part = part.reshape(span, CONV_COL_BLOCK)[shift:shift + rb, :]
            total = part if total is None else total + part
        cols.append(total)
    return jnp.concatenate(cols, axis=1)


def _even_pre_kernel(x_ref, g_ref, w_ref, dww_ref, dwb_ref, lng_ref, lnb_ref, scw_ref, alog_ref, dtb_ref,
                     aout_ref, q_ref, k_ref, v_ref, gb_ref, z_ref, cst_ref, qst_ref,
                     cbuf, qbuf):
    tt = PROMPT_TILE
    j = pl.program_id(1)
    last = pl.num_programs(1) - 1
    h = _rms(x_ref[0], g_ref[...]).astype(BF16)

    glu = (_dot(h, w_ref[:, COL_GLU_VAL:COL_GLU_VAL + CONV_CH])
           * _sigmoid(_dot(h, w_ref[:, COL_GLU_GATE:COL_GLU_GATE + CONV_CH])))

    @pl.when(j == 0)
    def _():
        cbuf[0:CONV_HIST, :] = jnp.zeros((CONV_HIST, CONV_CH), F32)
        qbuf[0:SC_HIST, :] = jnp.zeros((SC_HIST, QKV_CH), F32)

    @pl.when(j > 0)
    def _():
        cbuf[0:CONV_HIST, :] = cbuf[tt:tt + CONV_HIST, :]
        qbuf[0:SC_HIST, :] = qbuf[tt:tt + SC_HIST, :]

    cbuf[CONV_HIST:CONV_HIST + tt, :] = glu
    gate_a = _silu(_dot(h, w_ref[:, COL_GATE_A:COL_GATE_A + CONV_CH]))
    for r0 in range(0, tt, CONV_ROW_BLOCK):
        acc = _causal_conv(cbuf, dww_ref, CONV_K, CONV_HIST, r0, CONV_CH)
        c = _silu(_layer_norm(acc + dwb_ref[...], lng_ref[...], lnb_ref[...]))
        aout_ref[0, r0:r0 + CONV_ROW_BLOCK, :] = (c * gate_a[r0:r0 + CONV_ROW_BLOCK, :]).astype(BF16)

    qbuf[SC_HIST:SC_HIST + tt, :] = _dot(h, w_ref[:, COL_QKV:COL_QKV + QKV_CH])
    for r0 in range(0, tt, CONV_ROW_BLOCK):
        rows = slice(r0, r0 + CONV_ROW_BLOCK)
        acc = _silu(_causal_conv(qbuf, scw_ref, SC_K, SC_HIST, r0, QKV_CH))
        for hd in range(DN_HEADS):
            lo = hd * DN_HEAD_DIM
            q_ref[0, rows, lo:lo + DN_HEAD_DIM] = _l2n(acc[:, lo:lo + DN_HEAD_DIM]) * (DN_HEAD_DIM ** -0.5)
            k_ref[0, rows, lo:lo + DN_HEAD_DIM] = _l2n(acc[:, DN_WIDTH + lo:DN_WIDTH + lo + DN_HEAD_DIM])
        v_ref[0, rows, :] = acc[:, 2 * DN_WIDTH:]

    z_ref[0] = _dot(h, w_ref[:, COL_Z:COL_Z + DN_WIDTH])
    gb_ref[0] = _gate_params(_dot(h, w_ref[:, COL_TAIL:COL_TAIL + LANES]), alog_ref[...], dtb_ref[...])

    @pl.when(j == last)
    def _():
        cst_ref[0, 0] = cbuf[CONV_HIST + tt - (CONV_K - 1):CONV_HIST + tt, :]
        qst_ref[0, 0] = qbuf[SC_HIST + tt - (SC_K - 1):SC_HIST + tt, :]


def _even_pre(x, g, w, dww, dwb, lng, lnb, scw, alog, dtb):
    b, t, _ = x.shape
    tt = PROMPT_TILE
    const = lambda shape: pl.BlockSpec(shape, lambda i, j: (0,) * len(shape))
    tile = lambda c: pl.BlockSpec((1, tt, c), lambda i, j: (i, j, 0))
    return pl.pallas_call(
        _even_pre_kernel,
        grid=(b, t // tt),
        in_specs=[tile(D_MODEL), const((1, D_MODEL)), const((D_MODEL, EVEN_IN_PAD)),
                  const((CONV_K * SUBLANES, CONV_CH)), const((1, CONV_CH)), const((1, CONV_CH)), const((1, CONV_CH)),
                  const((SC_K * SUBLANES, QKV_CH)), const((1, LANES)), const((1, LANES))],
        out_specs=[tile(CONV_CH), tile(DN_WIDTH), tile(DN_WIDTH), tile(DN_WIDTH), tile(LANES), tile(DN_WIDTH),
                   pl.BlockSpec((1, 1, CONV_K - 1, CONV_CH), lambda i, j: (0, i, 0, 0)),
                   pl.BlockSpec((1, 1, SC_K - 1, QKV_CH), lambda i, j: (0, i, 0, 0))],
        out_shape=[jax.ShapeDtypeStruct((b, t, CONV_CH), BF16),
                   jax.ShapeDtypeStruct((b, t, DN_WIDTH), F32),
                   jax.ShapeDtypeStruct((b, t, DN_WIDTH), F32),
                   jax.ShapeDtypeStruct((b, t, DN_WIDTH), F32),
                   jax.ShapeDtypeStruct((b, t, LANES), F32),
                   jax.ShapeDtypeStruct((b, t, DN_WIDTH), F32),
                   jax.ShapeDtypeStruct((1, b, CONV_K - 1, CONV_CH), F32),
                   jax.ShapeDtypeStruct((1, b, SC_K - 1, QKV_CH), F32)],
        scratch_shapes=[pltpu.VMEM((CONV_HIST + tt, CONV_CH), F32),
                        pltpu.VMEM((SC_HIST + tt, QKV_CH), F32)],
        compiler_params=_params(2),
        name="p_even_pre",
    )(x, g, w, dww, dwb, lng, lnb, scw, alog, dtb)


def _delta_prep_kernel(q_ref, k_ref, v_ref, gb_ref, u_ref, wq_ref, kd_ref, qk_ref):
    tt = PROMPT_TILE
    c = DN_CHUNK
    dh = DN_HEAD_DIM
    nch = tt // c
    row = lax.broadcasted_iota(jnp.int32, (c, c), 0)
    col = lax.broadcasted_iota(jnp.int32, (c, c), 1)
    incl = (row >= col)[None]
    strict = (row > col)[None]
    eye = jnp.where(row == col, 1.0, 0.0)[None]
    trow = lax.broadcasted_iota(jnp.int32, (tt, tt), 0)
    tcol = lax.broadcasted_iota(jnp.int32, (tt, tt), 1)
    same_chunk = jnp.right_shift(trow, 6) == jnp.right_shift(tcol, 6)
    tri = jnp.where((trow >= tcol) & same_chunk, 1.0, 0.0).astype(BF16)

    def lane_rep(col0):
        return jnp.concatenate([jnp.broadcast_to(gb[:, col0 + hd:col0 + hd + 1], (tt, dh))
                                for hd in range(DN_HEADS)], axis=1)

    def chunks(x):
        return jnp.concatenate([x[:, hd * dh:(hd + 1) * dh].reshape(nch, c, dh) for hd in range(DN_HEADS)], axis=0)

    gb = gb_ref[0]
    gc4 = _dot_exact_lhs(tri, lane_rep(DN_HEADS))
    gc = chunks(gc4)
    gc_rows = []
    for hd in range(DN_HEADS):
        gc_t = gc4[:, hd * dh:(hd + 1) * dh].T
        gc_rows += [gc_t[0:c, n * c:(n + 1) * c][None] for n in range(nch)]
    gc_row = jnp.concatenate(gc_rows, axis=0)
    beta = chunks(lane_rep(0))
    q = chunks(q_ref[0])
    k = chunks(k_ref[0])
    v = chunks(v_ref[0])
    egc = jnp.exp(gc)
    kb = k * beta
    decay = jnp.where(incl, jnp.exp(jnp.where(incl, gc[:, :, 0:c] - gc_row, 0.0)), 0.0)
    a = jnp.where(strict, _bmm_nt(kb, k) * decay, 0.0)
    rhs = jnp.concatenate([v * beta, kb * egc], axis=2)
    p = -a
    t_inv = eye + p
    for _ in range(5):
        p = _bmm(p, p)
        t_inv = t_inv + _bmm(p, t_inv)
    x0 = _bmm(t_inv, rhs)
    rho = rhs - x0 - _bmm_hi(a, x0)
    x = x0 + _bmm(t_inv, rho)
    k_dec = k * jnp.exp(gc[:, c - 1:c, :] - gc)
    q_dec = q * egc
    qk = jnp.where(incl, _bmm_nt(q, k) * decay, 0.0)
    for hd in range(DN_HEADS):
        lo = hd * dh
        hs = slice(hd * nch, (hd + 1) * nch)
        u_ref[0, :, lo:lo + dh] = x[hs, :, 0:dh].reshape(tt, dh)
        wq_ref[0, :, 0:c, lo:lo + dh] = x[hs, :, dh:].astype(BF16)
        wq_ref[0, :, c:2 * c, lo:lo + dh] = q_dec[hs].astype(BF16)
        kd_ref[0, :, lo:lo + dh] = k_dec[hs].reshape(tt, dh).astype(BF16)
        qk_ref[0, hd] = qk[hs].reshape(tt, c).astype(BF16)


def _delta_prep(q, k, v, gb):
    b, t, _ = q.shape
    tt = PROMPT_TILE
    c = DN_CHUNK
    tile = lambda ch: pl.BlockSpec((1, tt, ch), lambda i, j: (i, j, 0))
    return pl.pallas_call(
        _delta_prep_kernel,
        grid=(b, t // tt),
        in_specs=[tile(DN_WIDTH), tile(DN_WIDTH), tile(DN_WIDTH), tile(LANES)],
        out_specs=[tile(DN_WIDTH),
                   pl.BlockSpec((1, tt // c, 2 * c, DN_WIDTH), lambda i, j: (i, j, 0, 0)),
                   tile(DN_WIDTH),
                   pl.BlockSpec((1, DN_HEADS, tt, c), lambda i, j: (i, 0, j, 0))],
        out_shape=[jax.ShapeDtypeStruct((b, t, DN_WIDTH), F32),
                   jax.ShapeDtypeStruct((b, t // c, 2 * c, DN_WIDTH), BF16),
                   jax.ShapeDtypeStruct((b, t, DN_WIDTH), BF16),
                   jax.ShapeDtypeStruct((b, DN_HEADS, t, c), BF16)],
        compiler_params=_params(2),
        name="p_delta_prep",
    )(q, k, v, gb)


def _delta_scan_kernel(u_ref, wq_ref, kd_ref, qk_ref, gb_ref, z_ref, dng_ref, bout_ref, sout_ref, s_ref):
    c = DN_CHUNK
    dh = DN_HEAD_DIM
    nb = u_ref.shape[0]
    n = pl.program_id(0)
    last = pl.num_programs(0) - 1

    @pl.when(n == 0)
    def _():
        s_ref[...] = jnp.zeros(s_ref.shape, F32)

    def heads(ref):
        return jnp.concatenate([ref[:, :, hd * dh:(hd + 1) * dh] for hd in range(DN_HEADS)], axis=0)

    g_tot = jnp.sum(gb_ref[...], axis=1, keepdims=True)
    g_last = jnp.exp(jnp.concatenate(
        [jnp.broadcast_to(g_tot[:, :, DN_HEADS + hd:DN_HEADS + hd + 1], (nb, 1, dh)) for hd in range(DN_HEADS)],
        axis=0))
    wq = jnp.concatenate([wq_ref[:, 0, :, hd * dh:(hd + 1) * dh] for hd in range(DN_HEADS)], axis=0)
    qk = jnp.concatenate([qk_ref[:, hd] for hd in range(DN_HEADS)], axis=0)
    s = s_ref[...]
    ws = _bmm(wq, s)
    v_new = heads(u_ref) - ws[:, 0:c]
    o = ws[:, c:] + _bmm(qk, v_new)
    s_ref[...] = s * g_last + _bmm_tn(heads(kd_ref), v_new)
    out = (_rms(o, dng_ref[...]) * _silu(heads(z_ref))).astype(BF16)
    for hd in range(DN_HEADS):
        bout_ref[:, :, hd * dh:(hd + 1) * dh] = out[hd * nb:(hd + 1) * nb]

    @pl.when(n == last)
    def _():
        for hd in range(DN_HEADS):
            sout_ref[0, :, hd] = s_ref[hd * nb:(hd + 1) * nb]


def _delta_scan(u, wq, kd, qk, gb, z, dng):
    b, t, _ = u.shape
    c = DN_CHUNK
    rows = lambda ch: pl.BlockSpec((b, c, ch), lambda n: (0, n, 0))
    return pl.pallas_call(
        _delta_scan_kernel,
        grid=(t // c,),
        in_specs=[rows(DN_WIDTH),
                  pl.BlockSpec((b, 1, 2 * c, DN_WIDTH), lambda n: (0, n, 0, 0)),
                  rows(DN_WIDTH),
                  pl.BlockSpec((b, DN_HEADS, c, c), lambda n: (0, 0, n, 0)),
                  rows(LANES), rows(DN_WIDTH),
                  pl.BlockSpec((1, DN_HEAD_DIM), lambda n: (0, 0))],
        out_specs=[rows(DN_WIDTH),
                   pl.BlockSpec((1, b, DN_HEADS, DN_HEAD_DIM, DN_HEAD_DIM), lambda n: (0, 0, 0, 0, 0))],
        out_shape=[jax.ShapeDtypeStruct((b, t, DN_WIDTH), BF16),
                   jax.ShapeDtypeStruct((1, b, DN_HEADS, DN_HEAD_DIM, DN_HEAD_DIM), F32)],
        scratch_shapes=[pltpu.VMEM((DN_HEADS * b, DN_HEAD_DIM, DN_HEAD_DIM), F32)],
        compiler_params=_params(1),
        name="p_delta_scan",
    )(u, wq, kd, qk, gb, z, dng)


def _xattn_tile(x1, g, wq_ref, wo_ref, mk_ref, mv_ref):
    h2 = _rms(x1, g).astype(BF16)
    q = jnp.dot(h2, wq_ref[...], preferred_element_type=F32)
    outs = []
    for hd in range(XA_HEADS):
        lo = hd * XA_HEAD_DIM
        s = _dot_nt(q[:, lo:lo + XA_HEAD_DIM], mk_ref[:, lo:lo + XA_HEAD_DIM]) * (XA_HEAD_DIM ** -0.5)
        outs.append(_dot(_softmax_lanes(s), mv_ref[:, lo:lo + XA_HEAD_DIM]).astype(BF16))
    o = jnp.concatenate(outs, axis=1)
    return x1 + jnp.dot(o, wo_ref[...], preferred_element_type=F32)


def _even_post_kernel(a_ref, b_ref, x_ref, wout_ref, gx_ref, wq_ref, wo_ref, mk_ref, mv_ref, o_ref):
    mix = jnp.concatenate([a_ref[0], b_ref[0]], axis=1)
    x1 = x_ref[0] + jnp.dot(mix, wout_ref[...], preferred_element_type=F32)
    o_ref[0] = _xattn_tile(x1, gx_ref[...], wq_ref, wo_ref, mk_ref, mv_ref)


def _even_post(a, bo, x, wout, gx, wq, wo, mk, mv):
    b, t, _ = x.shape
    tt = POST_TILE
    const = lambda shape: pl.BlockSpec(shape, lambda i, j: (0,) * len(shape))
    tile = lambda c: pl.BlockSpec((1, tt, c), lambda i, j: (i, j, 0))
    mem = pl.BlockSpec((N_MEM, D_MODEL), lambda i, j: (i, 0))
    return pl.pallas_call(
        _even_post_kernel,
        grid=(b, t // tt),
        in_specs=[tile(CONV_CH), tile(DN_WIDTH), tile(D_MODEL), const((D_MODEL, D_MODEL)), const((1, D_MODEL)),
                  const((D_MODEL, D_MODEL)), const((D_MODEL, D_MODEL)), mem, mem],
        out_specs=tile(D_MODEL),
        out_shape=jax.ShapeDtypeStruct((b, t, D_MODEL), F32),
        compiler_params=_params(2),
        name="p_even_post",
    )(a, bo, x, wout, gx, wq, wo, mk, mv)


def _pool_group_linear(pooled, wp_ref, bp_ref):
    outs = []
    for gi in range(len(POOL_WINDOWS)):
        lo = gi * POOL_GROUP
        outs.append(_dot(pooled[:, lo:lo + POOL_GROUP], wp_ref[gi]) + bp_ref[gi:gi + 1, :])
    return jnp.concatenate(outs, axis=1)


def _odd_kernel(x_ref, g_ref, win_ref, wp_ref, bp_ref, sc_ref, wout_ref, gx_ref, wq_ref, wo_ref, mk_ref, mv_ref,
                gf_ref, y_ref, pst_ref, pbuf):
    tt = POST_TILE
    j = pl.program_id(1)
    last = pl.num_programs(1) - 1
    x = x_ref[0]
    h = _rms(x, g_ref[...]).astype(BF16)
    u = jnp.dot(h, win_ref[:, 0:D_MODEL], preferred_element_type=F32)
    gate = _silu(jnp.dot(h, win_ref[:, D_MODEL:], preferred_element_type=F32))

    @pl.when(j == 0)
    def _():
        pbuf[0:POOL_HIST, :] = jnp.zeros((POOL_HIST, D_MODEL), F32)

    @pl.when(j > 0)
    def _():
        pbuf[0:POOL_HIST, :] = pbuf[tt:tt + POOL_HIST, :]

    pbuf[POOL_HIST:POOL_HIST + tt, :] = u

    pos = j * tt + lax.broadcasted_iota(jnp.int32, (tt, 1), 0)
    acc = pbuf[...]
    means = []
    for gi, win in enumerate(POOL_WINDOWS):
        acc = acc + pltpu.roll(acc, win // 2, 0)
        cnt = jnp.minimum(pos + 1, win).astype(F32)
        means.append(acc[POOL_HIST:, 0:POOL_GROUP] / cnt)
        acc = acc[:, POOL_GROUP:]
    pooled = jnp.concatenate(means, axis=1) - u
    z = _pool_group_linear(pooled, wp_ref, bp_ref) * sc_ref[...] * gate
    x1 = x + jnp.dot(z.astype(BF16), wout_ref[...], preferred_element_type=F32)
    x2 = _xattn_tile(x1, gx_ref[...], wq_ref, wo_ref, mk_ref, mv_ref)
    y_ref[0] = _rms(x2, gf_ref[...])

    @pl.when(j == last)
    def _():
        pst_ref[0, 0] = pbuf[POOL_HIST + tt - POOL_BUF:POOL_HIST + tt, :]


def _odd(x, g, win, wp, bp, sc, wout, gx, wq, wo, mk, mv, gf):
    b, t, _ = x.shape
    tt = POST_TILE
    const = lambda shape: pl.BlockSpec(shape, lambda i, j: (0,) * len(shape))
    tile = lambda c: pl.BlockSpec((1, tt, c), lambda i, j: (i, j, 0))
    mem = pl.BlockSpec((N_MEM, D_MODEL), lambda i, j: (i, 0))
    ngrp = len(POOL_WINDOWS)
    return pl.pallas_call(
        _odd_kernel,
        grid=(b, t // tt),
        in_specs=[tile(D_MODEL), const((1, D_MODEL)), const((D_MODEL, 2 * D_MODEL)),
                  const((ngrp, POOL_GROUP, POOL_GROUP)), const((ngrp, POOL_GROUP)), const((1, D_MODEL)),
                  const((D_MODEL, D_MODEL)), const((1, D_MODEL)), const((D_MODEL, D_MODEL)),
                  const((D_MODEL, D_MODEL)), mem, mem, const((1, D_MODEL))],
        out_specs=[tile(D_MODEL), pl.BlockSpec((1, 1, POOL_BUF, D_MODEL), lambda i, j: (0, i, 0, 0))],
        out_shape=[jax.ShapeDtypeStruct((b, t, D_MODEL), F32),
                   jax.ShapeDtypeStruct((1, b, POOL_BUF, D_MODEL), F32)],
        scratch_shapes=[pltpu.VMEM((POOL_HIST + tt, D_MODEL), F32)],
        compiler_params=_params(2),
        name="p_odd_layer",
    )(x, g, win, wp, bp, sc, wout, gx, wq, wo, mk, mv, gf)


def _s_even_pre_kernel(x_ref, g_ref, w_ref, dww_ref, dwb_ref, lng_ref, lnb_ref, scw_ref, alog_ref, dtb_ref,
                       cin_ref, qin_ref,
                       aout_ref, q_ref, k_ref, v_ref, gb_ref, z_ref, cout_ref, qout_ref):
    h = _rms(x_ref[...], g_ref[...]).astype(BF16)
    glu = (_dot(h, w_ref[:, COL_GLU_VAL:COL_GLU_VAL + CONV_CH])
           * _sigmoid(_dot(h, w_ref[:, COL_GLU_GATE:COL_GLU_GATE + CONV_CH])))
    nb = CONV_K - 1
    acc = dww_ref[nb:nb + 1, :] * glu
    for kk in range(nb):
        acc = acc + dww_ref[kk:kk + 1, :] * cin_ref[:, kk * CONV_CH:(kk + 1) * CONV_CH]
    cout_ref[:, 0:(nb - 1) * CONV_CH] = cin_ref[:, CONV_CH:nb * CONV_CH]
    cout_ref[:, (nb - 1) * CONV_CH:nb * CONV_CH] = glu
    c = _silu(_layer_norm(acc + dwb_ref[...], lng_ref[...], lnb_ref[...]))
    aout_ref[...] = c * _silu(_dot(h, w_ref[:, COL_GATE_A:COL_GATE_A + CONV_CH]))

    qkv = _dot(h, w_ref[:, COL_QKV:COL_QKV + QKV_CH])
    ns = SC_K - 1
    acc = scw_ref[ns:ns + 1, :] * qkv
    for kk in range(ns):
        acc = acc + scw_ref[kk:kk + 1, :] * qin_ref[:, kk * QKV_CH:(kk + 1) * QKV_CH]
    qout_ref[:, 0:(ns - 1) * QKV_CH] = qin_ref[:, QKV_CH:ns * QKV_CH]
    qout_ref[:, (ns - 1) * QKV_CH:ns * QKV_CH] = qkv
    acc = _silu(acc)
    for hd in range(DN_HEADS):
        lo = hd * DN_HEAD_DIM
        q_ref[:, lo:lo + DN_HEAD_DIM] = _l2n(acc[:, lo:lo + DN_HEAD_DIM]) * (DN_HEAD_DIM ** -0.5)
        k_ref[:, lo:lo + DN_HEAD_DIM] = _l2n(acc[:, DN_WIDTH + lo:DN_WIDTH + lo + DN_HEAD_DIM])
    v_ref[...] = acc[:, 2 * DN_WIDTH:]
    z_ref[...] = _dot(h, w_ref[:, COL_Z:COL_Z + DN_WIDTH])
    gb_ref[...] = _gate_params(_dot(h, w_ref[:, COL_TAIL:COL_TAIL + LANES]), alog_ref[...], dtb_ref[...])


def _s_even_pre(x, g, w, dww, dwb, lng, lnb, scw, alog, dtb, cin, qin):
    n = x.shape[0]
    rbk = SAMPLE_ROW_BLOCK
    const = lambda shape: pl.BlockSpec(shape, lambda i: (0,) * len(shape))
    rows = lambda c: pl.BlockSpec((rbk, c), lambda i: (i, 0))
    cw = (CONV_K - 1) * CONV_CH
    qw = (SC_K - 1) * QKV_CH
    return pl.pallas_call(
        _s_even_pre_kernel,
        grid=(n // rbk,),
        in_specs=[rows(D_MODEL), const((1, D_MODEL)), const((D_MODEL, EVEN_IN_PAD)), const((CONV_K, CONV_CH)),
                  const((1, CONV_CH)), const((1, CONV_CH)), const((1, CONV_CH)), const((SC_K, QKV_CH)),
                  const((1, LANES)), const((1, LANES)), rows(cw), rows(qw)],
        out_specs=[rows(CONV_CH), rows(DN_WIDTH), rows(DN_WIDTH), rows(DN_WIDTH), rows(LANES), rows(DN_WIDTH),
                   rows(cw), rows(qw)],
        out_shape=[jax.ShapeDtypeStruct((n, CONV_CH), F32),
                   jax.ShapeDtypeStruct((n, DN_WIDTH), F32),
                   jax.ShapeDtypeStruct((n, DN_WIDTH), F32),
                   jax.ShapeDtypeStruct((n, DN_WIDTH), F32),
                   jax.ShapeDtypeStruct((n, LANES), F32),
                   jax.ShapeDtypeStruct((n, DN_WIDTH), F32),
                   jax.ShapeDtypeStruct((n, cw), F32),
                   jax.ShapeDtypeStruct((n, qw), F32)],
        compiler_params=_params(1),
        name="s_even_pre",
    )(x, g, w, dww, dwb, lng, lnb, scw, alog, dtb, cin, qin)


def _s_delta_kernel(q_ref, k_ref, v_ref, gb_ref, z_ref, dng_ref, sin_ref, bout_ref, sout_ref):
    dh = DN_HEAD_DIM
    chains = [(i, hd) for i in range(DELTA_ROW_BLOCK) for hd in range(DN_HEADS)]
    vec = lambda ref, i, hd: ref[i:i + 1, hd * dh:(hd + 1) * dh]
    col = lambda ref, i, hd: jnp.broadcast_to(vec(ref, i, hd), (dh, dh)).T
    kcol = [col(k_ref, i, hd) for i, hd in chains]
    qcol = [col(q_ref, i, hd) for i, hd in chains]
    s = [sin_ref[0, i, hd] * jnp.exp(gb_ref[i:i + 1, DN_HEADS + hd:DN_HEADS + hd + 1]) for i, hd in chains]
    v_new = [(vec(v_ref, i, hd) - jnp.sum(kc * sc, axis=0, keepdims=True)) * gb_ref[i:i + 1, hd:hd + 1]
             for (i, hd), kc, sc in zip(chains, kcol, s)]
    s = [sc + kc * vn for sc, kc, vn in zip(s, kcol, v_new)]
    for (i, hd), sc, qc in zip(chains, s, qcol):
        sout_ref[0, i, hd] = sc
        o = jnp.sum(qc * sc, axis=0, keepdims=True)
        bout_ref[i:i + 1, hd * dh:(hd + 1) * dh] = _rms(o, dng_ref[...]) * _silu(vec(z_ref, i, hd))


def _s_delta(q, k, v, gb, z, dng, s_in):
    n = q.shape[0]
    rbk = DELTA_ROW_BLOCK
    rows = lambda c: pl.BlockSpec((rbk, c), lambda i: (i, 0))
    st = pl.BlockSpec((1, rbk, DN_HEADS, DN_HEAD_DIM, DN_HEAD_DIM), lambda i: (0, i, 0, 0, 0))
    return pl.pallas_call(
        _s_delta_kernel,
        grid=(n // rbk,),
        in_specs=[rows(DN_WIDTH), rows(DN_WIDTH), rows(DN_WIDTH), rows(LANES), rows(DN_WIDTH),
                  pl.BlockSpec((1, DN_HEAD_DIM), lambda i: (0, 0)), st],
        out_specs=[rows(DN_WIDTH), st],
        out_shape=[jax.ShapeDtypeStruct((n, DN_WIDTH), F32),
                   jax.ShapeDtypeStruct(s_in.shape, F32)],
        compiler_params=_params(1),
        name="s_delta",
    )(q, k, v, gb, z, dng, s_in)


def _s_mix_out_kernel(a_ref, b_ref, x_ref, wout_ref, gx_ref, wq_ref, x1_ref, q_ref):
    mix = jnp.concatenate([a_ref[...], b_ref[...]], axis=1).astype(BF16)
    x1 = x_ref[...] + jnp.dot(mix, wout_ref[...], preferred_element_type=F32)
    x1_ref[...] = x1
    q_ref[...] = jnp.dot(_rms(x1, gx_ref[...]).astype(BF16), wq_ref[...], preferred_element_type=F32)


def _s_mix_out(a, bo, x, wout, gx, wq):
    n = x.shape[0]
    full = lambda arr: pl.BlockSpec(arr.shape, lambda: (0,) * arr.ndim)
    args = (a, bo, x, wout, gx, wq)
    out = jax.ShapeDtypeStruct((n, D_MODEL), F32)
    return pl.pallas_call(
        _s_mix_out_kernel,
        in_specs=[full(v) for v in args],
        out_specs=[pl.BlockSpec((n, D_MODEL), lambda: (0, 0))] * 2,
        out_shape=[out, out],
        compiler_params=pltpu.CompilerParams(vmem_limit_bytes=VMEM_LIMIT_BYTES),
        name="s_mix_out",
    )(*args)


def _split_head_dim(x):
    lead = x.shape[:-2]
    halves = XA_HEAD_DIM // LANES
    x = x.reshape(lead + (XA_HEADS, halves, LANES))
    x = jnp.swapaxes(x, -3, -2)
    return x.reshape(lead + (halves * XA_HEADS, LANES))


def _merge_head_dim(x):
    lead = x.shape[:-2]
    halves = XA_HEAD_DIM // LANES
    x = jnp.swapaxes(x.reshape(lead + (halves, XA_HEADS, LANES)), -3, -2)
    return x.reshape(lead + (XA_HEADS, XA_HEAD_DIM))


def _s_xattn_kernel(q_ref, mk_ref, mv_ref, o_ref):
    for i in range(XATTN_ROW_BLOCK):
        prod = mk_ref[0, i] * q_ref[i]
        prod = prod + pltpu.roll(prod, XA_HEADS, 1)
        s = jnp.sum(prod, axis=-1, keepdims=True) * (XA_HEAD_DIM ** -0.5)
        e = jnp.exp(s - jnp.max(s, axis=0, keepdims=True))
        o_ref[i] = jnp.sum(e * mv_ref[0, i], axis=0) / jnp.sum(e, axis=0)


def _s_xattn(q, mk, mv, layer):
    n = q.shape[0]
    rbk = XATTN_ROW_BLOCK
    sub = XA_HEADS * XA_HEAD_DIM // LANES
    mem = pl.BlockSpec((1, rbk, N_MEM, sub, LANES), lambda i: (layer, i, 0, 0, 0))
    rows = pl.BlockSpec((rbk, sub, LANES), lambda i: (i, 0, 0))
    out = pl.pallas_call(
        _s_xattn_kernel,
        grid=(n // rbk,),
        in_specs=[rows, mem, mem],
        out_specs=rows,
        out_shape=jax.ShapeDtypeStruct((n, sub, LANES), F32),
        compiler_params=_params(1),
        name="s_xattn",
    )(_split_head_dim(q.reshape(n, XA_HEADS, XA_HEAD_DIM)), _split_head_dim(mk), _split_head_dim(mv))
    return _merge_head_dim(out).reshape(n, D_MODEL)


def _s_odd_kernel(x1_ref, o_ref, wo_ref, g_ref, win_ref, wp_ref, bp_ref, sc_ref, wout_ref, gx_ref, wq_ref, pin_ref,
                  x1o_ref, q_ref, pout_ref):
    x = x1_ref[...] + _dot(o_ref[...], wo_ref[...])
    h = _rms(x, g_ref[...]).astype(BF16)
    u = jnp.dot(h, win_ref[:, 0:D_MODEL], preferred_element_type=F32)
    gate = _silu(jnp.dot(h, win_ref[:, D_MODEL:], preferred_element_type=F32))
    means = []
    for gi, win in enumerate(POOL_WINDOWS):
        lo = gi * POOL_GROUP
        acc = u[:, lo:lo + POOL_GROUP]
        for d in range(1, win):
            src = (POOL_BUF - d) * D_MODEL + lo
            acc = acc + pin_ref[:, src:src + POOL_GROUP]
        means.append(acc / float(min(PAST_LEN + 1, win)))
    pooled = jnp.concatenate(means, axis=1) - u
    pout_ref[:, 0:(POOL_BUF - 1) * D_MODEL] = pin_ref[:, D_MODEL:POOL_BUF * D_MODEL]
    pout_ref[:, (POOL_BUF - 1) * D_MODEL:POOL_BUF * D_MODEL] = u
    z = _pool_group_linear(pooled, wp_ref, bp_ref) * sc_ref[...] * gate
    x1 = x + jnp.dot(z.astype(BF16), wout_ref[...], preferred_element_type=F32)
    x1o_ref[...] = x1
    q_ref[...] = jnp.dot(_rms(x1, gx_ref[...]).astype(BF16), wq_ref[...], preferred_element_type=F32)


def _s_odd(x1, o, wo, g, win, wp, bp, sc, wout, gx, wq, pin):
    n = x1.shape[0]
    rbk = SAMPLE_ROW_BLOCK
    ngrp = len(POOL_WINDOWS)
    const = lambda shape: pl.BlockSpec(shape, lambda i: (0,) * len(shape))
    rows = lambda c: pl.BlockSpec((rbk, c), lambda i: (i, 0))
    pw = POOL_BUF * D_MODEL
    out = jax.ShapeDtypeStruct((n, D_MODEL), F32)
    return pl.pallas_call(
        _s_odd_kernel,
        grid=(n // rbk,),
        in_specs=[rows(D_MODEL), rows(D_MODEL), const((D_MODEL, D_MODEL)), const((1, D_MODEL)),
                  const((D_MODEL, 2 * D_MODEL)), const((ngrp, POOL_GROUP, POOL_GROUP)), const((ngrp, POOL_GROUP)),
                  const((1, D_MODEL)), const((D_MODEL, D_MODEL)), const((1, D_MODEL)), const((D_MODEL, D_MODEL)),
                  rows(pw)],
        out_specs=[rows(D_MODEL), rows(D_MODEL), rows(pw)],
        out_shape=[out, out, jax.ShapeDtypeStruct((n, pw), F32)],
        compiler_params=_params(1),
        name="s_odd",
    )(x1, o, wo, g, win, wp, bp, sc, wout, gx, wq, pin)


def _s_final_kernel(x1_ref, o_ref, wo_ref, gf_ref, y_ref):
    y_ref[...] = _rms(x1_ref[...] + _dot(o_ref[...], wo_ref[...]), gf_ref[...])


def _s_final(x1, o, wo, gf):
    n = x1.shape[0]
    full = lambda arr: pl.BlockSpec(arr.shape, lambda: (0,) * arr.ndim)
    args = (x1, o, wo, gf)
    return pl.pallas_call(
        _s_final_kernel,
        in_specs=[full(v) for v in args],
        out_specs=pl.BlockSpec((n, D_MODEL), lambda: (0, 0)),
        out_shape=jax.ShapeDtypeStruct((n, D_MODEL), F32),
        compiler_params=pltpu.CompilerParams(vmem_limit_bytes=VMEM_LIMIT_BYTES),
        name="s_final",
    )(*args)


def _lane_pad(vec, offset):
    return jnp.pad(vec.astype(F32), (offset, LANES - offset - vec.shape[0])).reshape(1, LANES)


def kernel(x_prompt, x_sample, state_conv_a, state_qkv_conv, state_delta, state_pool, cache_mem_k, cache_mem_v, mem_prompt, norm_mix, norm_xattn, norm_final, w_in_even, w_out_even, dw_w, dw_b, ln_a_g, ln_a_b, sc_w, a_log, dt_bias, dn_norm_g, w_in_odd, w_pool, b_pool, pool_scale, w_out_odd, w_xq, w_xk, w_xv, w_xo):
    bp, t, _ = x_prompt.shape
    ns = x_sample.shape[0]
    row = lambda v: v.reshape(1, -1)

    w_in0 = jnp.pad(w_in_even[0], ((0, 0), (0, EVEN_IN_PAD - EVEN_IN))).astype(BF16)
    w_out0 = w_out_even[0].astype(BF16)
    w_in1 = w_in_odd[0].astype(BF16)
    w_pool1 = w_pool[0].astype(BF16)
    w_out1 = w_out_odd[0].astype(BF16)
    wq = w_xq.astype(BF16)
    wk = w_xk.astype(BF16)
    wv = w_xv.astype(BF16)
    wo = w_xo.astype(BF16)
    alog = _lane_pad(a_log[0], DN_HEADS)
    dtb = _lane_pad(dt_bias[0], DN_HEADS)
    even_small = (dw_w[0], row(dw_b[0]), row(ln_a_g[0]), row(ln_a_b[0]), sc_w[0], alog, dtb)
    dng = row(dn_norm_g[0])

    mk_f, mv_f, mk_b, mv_b = _mem_kv(mem_prompt.reshape(bp * N_MEM, D_MODEL), wk, wv)
    new_mem_k_p = mk_f.reshape(DEPTH, bp, N_MEM, XA_HEADS, XA_HEAD_DIM)
    new_mem_v_p = mv_f.reshape(DEPTH, bp, N_MEM, XA_HEADS, XA_HEAD_DIM)

    even_small_p = (jnp.repeat(dw_w[0], SUBLANES, axis=0),) + even_small[1:4] + (
        jnp.repeat(sc_w[0], SUBLANES, axis=0),) + even_small[5:]
    a_out, q, k, v, gb, z, new_conv_a_p, new_qkv_conv_p = _even_pre(x_prompt, row(norm_mix[0]), w_in0,
                                                                    *even_small_p)
    u, wqd, kd, qk = _delta_prep(q, k, v, gb)
    b_out, new_delta_p = _delta_scan(u, wqd, kd, qk, gb, z, dng)
    x2 = _even_post(a_out, b_out, x_prompt, w_out0, row(norm_xattn[0]), wq[0], wo[0], mk_b[0], mv_b[0])
    y_prompt, new_pool_p = _odd(x2, row(norm_mix[1]), w_in1, w_pool1, b_pool[0], row(pool_scale[0]), w_out1,
                                row(norm_xattn[1]), wq[1], wo[1], mk_b[1], mv_b[1], row(norm_final))

    xs = x_sample.reshape(ns, D_MODEL)
    cin = state_conv_a.reshape(ns, (CONV_K - 1) * CONV_CH)
    qin = state_qkv_conv.reshape(ns, (SC_K - 1) * QKV_CH)
    pin = state_pool.reshape(ns, POOL_BUF * D_MODEL)

    sa, sq, sk, sv, sgb, sz, cout, qout = _s_even_pre(xs, row(norm_mix[0]), w_in0, *even_small, cin, qin)
    sb, new_delta_s = _s_delta(sq, sk, sv, sgb, sz, dng, state_delta)
    x1, xq = _s_mix_out(sa, sb, xs, w_out0, row(norm_xattn[0]), wq[0])
    o0 = _s_xattn(xq, cache_mem_k, cache_mem_v, 0)
    x1, xq, pout = _s_odd(x1, o0, wo[0], row(norm_mix[1]), w_in1, w_pool1, b_pool[0], row(pool_scale[0]), w_out1,
                          row(norm_xattn[1]), wq[1], pin)
    o1 = _s_xattn(xq, cache_mem_k, cache_mem_v, 1)
    y_sample = _s_final(x1, o1, wo[1], row(norm_final)).reshape(ns, 1, D_MODEL)

    new_conv_a_s = cout.reshape(1, ns, CONV_K - 1, CONV_CH)
    new_qkv_conv_s = qout.reshape(1, ns, SC_K - 1, QKV_CH)
    new_pool_s = pout.reshape(1, ns, POOL_BUF, D_MODEL)
    return (y_prompt, y_sample, new_conv_a_p, new_qkv_conv_p, new_delta_p, new_pool_p, new_mem_k_p,
            new_mem_v_p, new_conv_a_s, new_qkv_conv_s, new_delta_s, new_pool_s)
```

```python
import functools

import jax
import jax.numpy as jnp
from jax import lax
from jax.experimental import pallas as pl
from jax.experimental.pallas import tpu as pltpu

F32 = jnp.float32
BF16 = jnp.bfloat16

D_MODEL = 1024
DEPTH = 2
PAST_LEN = 16384
CONV_CH = 512
CONV_K = 31
DN_HEAD_DIM = 128
DN_HEADS = 4
DN_WIDTH = 512
QKV_CH = 1536
SC_K = 4
DN_CHUNK = 64
EVEN_IN = 3592
POOL_WINDOWS = (2, 4, 8, 16)
POOL_GROUP = 256
POOL_BUF = 15
N_MEM = 256
XA_HEADS = 4
XA_HEAD_DIM = 256
EPS = 1e-6

LANES = 128
SUBLANES = 8
VMEM_LIMIT_BYTES = 56 * 1024 * 1024

COL_GLU_VAL = 0
COL_GLU_GATE = 512
COL_GATE_A = 1024
COL_QKV = 1536
COL_Z = 3072
COL_TAIL = 3584
EVEN_IN_PAD = COL_TAIL + LANES

PROMPT_TILE = 256
POST_TILE = 512
POST_STREAMS = 2
CONV_HIST = 32
SC_HIST = 8
POOL_HIST = 16
CONV_ROW_BLOCK = 64
CONV_COL_BLOCK = 256
SAMPLE_ROW_BLOCK = 32
DELTA_ROW_BLOCK = 8
XATTN_ROW_BLOCK = 4


def _params(n_axes):
    return pltpu.CompilerParams(dimension_semantics=("arbitrary",) * n_axes,
                                vmem_limit_bytes=VMEM_LIMIT_BYTES)


def _dot(a, b):
    return jnp.dot(a.astype(BF16), b.astype(BF16), preferred_element_type=F32)


def _dot_nt(a, b):
    return lax.dot_general(a.astype(BF16), b.astype(BF16), (((1,), (1,)), ((), ())),
                           preferred_element_type=F32)


def _split3(x):
    x1 = x.astype(BF16)
    r1 = x - x1.astype(F32)
    x2 = r1.astype(BF16)
    x3 = (r1 - x2.astype(F32)).astype(BF16)
    return x1, x2, x3


def _split2(x):
    x1 = x.astype(BF16)
    return x1, (x - x1.astype(F32)).astype(BF16)


def _dot_hi(a, b):
    a1, a2 = _split2(a)
    b1, b2 = _split2(b)
    d = functools.partial(jnp.dot, preferred_element_type=F32)
    return d(a1, b1) + (d(a1, b2) + d(a2, b1))


def _bmm(a, b):
    return lax.dot_general(a.astype(BF16), b.astype(BF16), (((2,), (1,)), ((0,), (0,))),
                           preferred_element_type=F32)


def _bmm_nt(a, b):
    return lax.dot_general(a.astype(BF16), b.astype(BF16), (((2,), (2,)), ((0,), (0,))),
                           preferred_element_type=F32)


def _bmm_tn(a, b):
    return lax.dot_general(a.astype(BF16), b.astype(BF16), (((1,), (1,)), ((0,), (0,))),
                           preferred_element_type=F32)


def _bmm_hi(a, b):
    a1, a2 = _split2(a)
    b1, b2 = _split2(b)
    return _bmm(a1, b1) + (_bmm(a1, b2) + _bmm(a2, b1))


def _dot_exact_lhs(a_bf, b):
    b1, b2, b3 = _split3(b)
    d = functools.partial(jnp.dot, preferred_element_type=F32)
    return d(a_bf, b1) + d(a_bf, b2) + d(a_bf, b3)


def _sigmoid(x):
    return 1.0 / (1.0 + jnp.exp(-x))


def _silu(x):
    return x * _sigmoid(x)


def _softplus(x):
    return jnp.maximum(x, 0.0) + jnp.log(1.0 + jnp.exp(-jnp.abs(x)))


def _rms(x, g):
    return x * lax.rsqrt(jnp.mean(x * x, axis=-1, keepdims=True) + EPS) * g


def _layer_norm(x, g, b):
    xc = x - jnp.mean(x, axis=-1, keepdims=True)
    return xc * lax.rsqrt(jnp.mean(xc * xc, axis=-1, keepdims=True) + EPS) * g + b


def _l2n(x):
    return x * lax.rsqrt(jnp.sum(x * x, axis=-1, keepdims=True) + EPS)


def _gate_params(tail, alog, dtb):
    lane = lax.broadcasted_iota(jnp.int32, tail.shape, 1)
    beta = _sigmoid(tail)
    g = -jnp.exp(alog) * _softplus(tail + dtb)
    return jnp.where(lane < DN_HEADS, beta, g)


def _softmax_lanes(s):
    m = jnp.max(s, axis=-1, keepdims=True)
    e = jnp.exp(s - m)
    return e / jnp.sum(e, axis=-1, keepdims=True)


def _mem_kv_kernel(x_ref, wk_ref, wv_ref, k_ref, v_ref, kb_ref, vb_ref):
    x = x_ref[...]
    k = _dot(x, wk_ref[0])
    v = _dot(x, wv_ref[0])
    k_ref[0] = k
    v_ref[0] = v
    kb_ref[0] = k.astype(BF16)
    vb_ref[0] = v.astype(BF16)


def _mem_kv(mem2d, wk, wv):
    rows = mem2d.shape[0]
    tile = 512
    f32_out = jax.ShapeDtypeStruct((DEPTH, rows, D_MODEL), F32)
    bf_out = jax.ShapeDtypeStruct((DEPTH, rows, D_MODEL), BF16)
    w_spec = pl.BlockSpec((1, D_MODEL, D_MODEL), lambda l, i: (l, 0, 0))
    o_spec = pl.BlockSpec((1, tile, D_MODEL), lambda l, i: (l, i, 0))
    return pl.pallas_call(
        _mem_kv_kernel,
        grid=(DEPTH, rows // tile),
        in_specs=[pl.BlockSpec((tile, D_MODEL), lambda l, i: (i, 0)), w_spec, w_spec],
        out_specs=[o_spec, o_spec, o_spec, o_spec],
        out_shape=[f32_out, f32_out, bf_out, bf_out],
        compiler_params=_params(2),
        name="p_mem_kv",
    )(mem2d, wk, wv)


def _causal_conv(buf_ref, w_ref, n_taps, hist, r0, width):
    rb = CONV_ROW_BLOCK
    off = hist - (n_taps - 1)
    cols = []
    for c0 in range(0, width, CONV_COL_BLOCK):
        cs = slice(c0, c0 + CONV_COL_BLOCK)
        total = None
        for res in range(min(SUBLANES, n_taps)):
            base = (off + res) // SUBLANES * SUBLANES
            shift = off + res - base
            span = rb + (SUBLANES if shift else 0)
            part = None
            for kk in range(res, n_taps, SUBLANES):
                lo = r0 + base + kk - res
                rows = buf_ref[lo:lo + span, cs].reshape(span // SUBLANES, SUBLANES, CONV_COL_BLOCK)
                term = rows * w_ref[kk * SUBLANES:(kk + 1) * SUBLANES, cs][None]
                part = term if part is None else part + term
            part = part.reshape(span, CONV_COL_BLOCK)[shift:shift + rb, :]
            total = part if total is None else total + part
        cols.append(total)
    return jnp.concatenate(cols, axis=1)


def _even_front_kernel(x_ref, g_ref, w_ref, dww_ref, dwb_ref, lng_ref, lnb_ref, scw_ref, alog_ref, dtb_ref,
                       aout_ref, z_ref, gb_ref, u_ref, wq_ref, kd_ref, qk_ref, cst_ref, qst_ref,
                       cbuf, qbuf, qkv):
    tt = PROMPT_TILE
    j = pl.program_id(1)
    last = pl.num_programs(1) - 1

    @pl.when(j == 0)
    def _():
        cbuf[0:CONV_HIST, :] = jnp.zeros((CONV_HIST, CONV_CH), F32)
        qbuf[0:SC_HIST, :] = jnp.zeros((SC_HIST, QKV_CH), F32)

    @pl.when(j > 0)
    def _():
        cbuf[0:CONV_HIST, :] = cbuf[tt:tt + CONV_HIST, :]
        qbuf[0:SC_HIST, :] = qbuf[tt:tt + SC_HIST, :]

    h = _rms(x_ref[0], g_ref[...]).astype(BF16)

    qbuf[SC_HIST:SC_HIST + tt, :] = _dot(h, w_ref[:, COL_QKV:COL_QKV + QKV_CH])
    gb = _gate_params(_dot(h, w_ref[:, COL_TAIL:COL_TAIL + LANES]), alog_ref[...], dtb_ref[...])
    gb_ref[0] = gb
    for r0 in range(0, tt, CONV_ROW_BLOCK):
        rows = slice(r0, r0 + CONV_ROW_BLOCK)
        acc = _silu(_causal_conv(qbuf, scw_ref, SC_K, SC_HIST, r0, QKV_CH))
        for hd in range(DN_HEADS):
            lo = hd * DN_HEAD_DIM
            qkv[rows, lo:lo + DN_HEAD_DIM] = _l2n(acc[:, lo:lo + DN_HEAD_DIM]) * (DN_HEAD_DIM ** -0.5)
            qkv[rows, DN_WIDTH + lo:DN_WIDTH + lo + DN_HEAD_DIM] = _l2n(
                acc[:, DN_WIDTH + lo:DN_WIDTH + lo + DN_HEAD_DIM])
        qkv[rows, 2 * DN_WIDTH:] = acc[:, 2 * DN_WIDTH:]
    cbuf[CONV_HIST:CONV_HIST + tt, :] = (_dot(h, w_ref[:, COL_GLU_VAL:COL_GLU_VAL + CONV_CH])
                                         * _sigmoid(_dot(h, w_ref[:, COL_GLU_GATE:COL_GLU_GATE + CONV_CH])))
    gate_a = _silu(_dot(h, w_ref[:, COL_GATE_A:COL_GATE_A + CONV_CH]))

    def conv_rows(r0):
        acc = _causal_conv(cbuf, dww_ref, CONV_K, CONV_HIST, r0, CONV_CH)
        c = _silu(_layer_norm(acc + dwb_ref[...], lng_ref[...], lnb_ref[...]))
        aout_ref[0, r0:r0 + CONV_ROW_BLOCK, :] = (c * gate_a[r0:r0 + CONV_ROW_BLOCK, :]).astype(BF16)

    _delta_prep_tile(qkv[:, 0:DN_WIDTH], qkv[:, DN_WIDTH:2 * DN_WIDTH], qkv[:, 2 * DN_WIDTH:], gb,
                     u_ref, wq_ref, kd_ref, qk_ref,
                     fillers=[functools.partial(conv_rows, r0) for r0 in range(0, tt, CONV_ROW_BLOCK)])
    z_ref[0] = _dot(h, w_ref[:, COL_Z:COL_Z + DN_WIDTH])

    @pl.when(j == last)
    def _():
        cst_ref[0, 0] = cbuf[CONV_HIST + tt - (CONV_K - 1):CONV_HIST + tt, :]
        qst_ref[0, 0] = qbuf[SC_HIST + tt - (SC_K - 1):SC_HIST + tt, :]


def _even_front(x, g, w, dww, dwb, lng, lnb, scw, alog, dtb):
    b, t, _ = x.shape
    tt = PROMPT_TILE
    c = DN_CHUNK
    const = lambda shape: pl.BlockSpec(shape, lambda i, j: (0,) * len(shape))
    tile = lambda ch: pl.BlockSpec((1, tt, ch), lambda i, j: (i, j, 0))
    return pl.pallas_call(
        _even_front_kernel,
        grid=(b, t // tt),
        in_specs=[tile(D_MODEL), const((1, D_MODEL)), const((D_MODEL, EVEN_IN_PAD)),
                  const((CONV_K * SUBLANES, CONV_CH)), const((1, CONV_CH)), const((1, CONV_CH)), const((1, CONV_CH)),
                  const((SC_K * SUBLANES, QKV_CH)), const((1, LANES)), const((1, LANES))],
        out_specs=[tile(CONV_CH), tile(DN_WIDTH), tile(LANES), tile(DN_WIDTH),
                   pl.BlockSpec((1, tt // c, 2 * c, DN_WIDTH), lambda i, j: (i, j, 0, 0)),
                   tile(DN_WIDTH),
                   pl.BlockSpec((1, DN_HEADS, tt, c), lambda i, j: (i, 0, j, 0)),
                   pl.BlockSpec((1, 1, CONV_K - 1, CONV_CH), lambda i, j: (0, i, 0, 0)),
                   pl.BlockSpec((1, 1, SC_K - 1, QKV_CH), lambda i, j: (0, i, 0, 0))],
        out_shape=[jax.ShapeDtypeStruct((b, t, CONV_CH), BF16),
                   jax.ShapeDtypeStruct((b, t, DN_WIDTH), F32),
                   jax.ShapeDtypeStruct((b, t, LANES), F32),
                   jax.ShapeDtypeStruct((b, t, DN_WIDTH), F32),
                   jax.ShapeDtypeStruct((b, t // c, 2 * c, DN_WIDTH), BF16),
                   jax.ShapeDtypeStruct((b, t, DN_WIDTH), BF16),
                   jax.ShapeDtypeStruct((b, DN_HEADS, t, c), BF16),
                   jax.ShapeDtypeStruct((1, b, CONV_K - 1, CONV_CH), F32),
                   jax.ShapeDtypeStruct((1, b, SC_K - 1, QKV_CH), F32)],
        scratch_shapes=[pltpu.VMEM((CONV_HIST + tt, CONV_CH), F32),
                        pltpu.VMEM((SC_HIST + tt, QKV_CH), F32),
                        pltpu.VMEM((tt, QKV_CH), F32)],
        compiler_params=_params(2),
        name="p_even_front",
    )(x, g, w, dww, dwb, lng, lnb, scw, alog, dtb)


def _delta_prep_tile(q2, k2, v2, gb, u_ref, wq_ref, kd_ref, qk_ref, fillers=()):
    fillers = iter(fillers)
    tt = PROMPT_TILE
    c = DN_CHUNK
    dh = DN_HEAD_DIM
    nch = tt // c
    row = lax.broadcasted_iota(jnp.int32, (c, c), 0)
    col = lax.broadcasted_iota(jnp.int32, (c, c), 1)
    incl = (row >= col)[None]
    strict = (row > col)[None]
    eye = jnp.where(row == col, 1.0, 0.0)[None]
    trow = lax.broadcasted_iota(jnp.int32, (tt, tt), 0)
    tcol = lax.broadcasted_iota(jnp.int32, (tt, tt), 1)
    same_chunk = jnp.right_shift(trow, 6) == jnp.right_shift(tcol, 6)
    tri = jnp.where((trow >= tcol) & same_chunk, 1.0, 0.0).astype(BF16)

    def lane_rep(col0):
        return jnp.concatenate([jnp.broadcast_to(gb[:, col0 + hd:col0 + hd + 1], (tt, dh))
                                for hd in range(DN_HEADS)], axis=1)

    def chunks(x):
        return jnp.concatenate([x[:, hd * dh:(hd + 1) * dh].reshape(nch, c, dh) for hd in range(DN_HEADS)], axis=0)

    gc4 = _dot_exact_lhs(tri, lane_rep(DN_HEADS))
    gc = chunks(gc4)
    gc_rows = []
    for hd in range(DN_HEADS):
        gc_t = gc4[:, hd * dh:(hd + 1) * dh].T
        gc_rows += [gc_t[0:c, n * c:(n + 1) * c][None] for n in range(nch)]
    gc_row = jnp.concatenate(gc_rows, axis=0)
    beta = chunks(lane_rep(0))
    q = chunks(q2)
    k = chunks(k2)
    v = chunks(v2)
    egc = jnp.exp(gc)
    kb = k * beta
    decay = jnp.where(incl, jnp.exp(jnp.where(incl, gc[:, :, 0:c] - gc_row, 0.0)), 0.0)
    a = jnp.where(strict, _bmm_nt(kb, k) * decay, 0.0)
    rhs = jnp.concatenate([v * beta, kb * egc], axis=2)
    p = -a
    t_inv = eye + p
    for _ in range(5):
        p = _bmm(p, p)
        t_inv = t_inv + _bmm(p, t_inv)
        next(fillers, lambda: None)()
    x0 = _bmm(t_inv, rhs)
    rho = rhs - x0 - _bmm_hi(a, x0)
    x = x0 + _bmm(t_inv, rho)
    k_dec = k * jnp.exp(gc[:, c - 1:c, :] - gc)
    q_dec = q * egc
    qk = jnp.where(incl, _bmm_nt(q, k) * decay, 0.0)
    for hd in range(DN_HEADS):
        lo = hd * dh
        hs = slice(hd * nch, (hd + 1) * nch)
        u_ref[0, :, lo:lo + dh] = x[hs, :, 0:dh].reshape(tt, dh)
        wq_ref[0, :, 0:c, lo:lo + dh] = x[hs, :, dh:].astype(BF16)
        wq_ref[0, :, c:2 * c, lo:lo + dh] = q_dec[hs].astype(BF16)
        kd_ref[0, :, lo:lo + dh] = k_dec[hs].reshape(tt, dh).astype(BF16)
        qk_ref[0, hd] = qk[hs].reshape(tt, c).astype(BF16)
    for rest in fillers:
        rest()


def _delta_scan_kernel(u_ref, wq_ref, kd_ref, qk_ref, gb_ref, z_ref, dng_ref, bout_ref, sout_ref, s_ref):
    c = DN_CHUNK
    dh = DN_HEAD_DIM
    nb = u_ref.shape[0]
    n = pl.program_id(0)
    last = pl.num_programs(0) - 1

    @pl.when(n == 0)
    def _():
        s_ref[...] = jnp.zeros(s_ref.shape, F32)

    def heads(ref):
        return jnp.concatenate([ref[:, :, hd * dh:(hd + 1) * dh] for hd in range(DN_HEADS)], axis=0)

    g_tot = jnp.sum(gb_ref[...], axis=1, keepdims=True)
    g_last = jnp.exp(jnp.concatenate(
        [jnp.broadcast_to(g_tot[:, :, DN_HEADS + hd:DN_HEADS + hd + 1], (nb, 1, dh)) for hd in range(DN_HEADS)],
        axis=0))
    wq = jnp.concatenate([wq_ref[:, 0, :, hd * dh:(hd + 1) * dh] for hd in range(DN_HEADS)], axis=0)
    qk = jnp.concatenate([qk_ref[:, hd] for hd in range(DN_HEADS)], axis=0)
    s = s_ref[...]
    ws = _bmm(wq, s)
    v_new = heads(u_ref) - ws[:, 0:c]
    o = ws[:, c:] + _bmm(qk, v_new)
    s_ref[...] = s * g_last + _bmm_tn(heads(kd_ref), v_new)
    out = (_rms(o, dng_ref[...]) * _silu(heads(z_ref))).astype(BF16)
    for hd in range(DN_HEADS):
        bout_ref[:, :, hd * dh:(hd + 1) * dh] = out[hd * nb:(hd + 1) * nb]

    @pl.when(n == last)
    def _():
        for hd in range(DN_HEADS):
            sout_ref[0, :, hd] = s_ref[hd * nb:(hd + 1) * nb]


def _delta_scan(u, wq, kd, qk, gb, z, dng):
    b, t, _ = u.shape
    c = DN_CHUNK
    rows = lambda ch: pl.BlockSpec((b, c, ch), lambda n: (0, n, 0))
    return pl.pallas_call(
        _delta_scan_kernel,
        grid=(t // c,),
        in_specs=[rows(DN_WIDTH),
                  pl.BlockSpec((b, 1, 2 * c, DN_WIDTH), lambda n: (0, n, 0, 0)),
                  rows(DN_WIDTH),
                  pl.BlockSpec((b, DN_HEADS, c, c), lambda n: (0, 0, n, 0)),
                  rows(LANES), rows(DN_WIDTH),
                  pl.BlockSpec((1, DN_HEAD_DIM), lambda n: (0, 0))],
        out_specs=[rows(DN_WIDTH),
                   pl.BlockSpec((1, b, DN_HEADS, DN_HEAD_DIM, DN_HEAD_DIM), lambda n: (0, 0, 0, 0, 0))],
        out_shape=[jax.ShapeDtypeStruct((b, t, DN_WIDTH), BF16),
                   jax.ShapeDtypeStruct((1, b, DN_HEADS, DN_HEAD_DIM, DN_HEAD_DIM), F32)],
        scratch_shapes=[pltpu.VMEM((DN_HEADS * b, DN_HEAD_DIM, DN_HEAD_DIM), F32)],
        compiler_params=_params(1),
        name="p_delta_scan",
    )(u, wq, kd, qk, gb, z, dng)


def _row_streams(tt):
    rows = tt // POST_STREAMS
    return [slice(i * rows, (i + 1) * rows) for i in range(POST_STREAMS)]


def _xattn_streams(x1s, g, wq_ref, wo_ref, mk_ref, mv_ref):
    qs = [jnp.dot(_rms(x1, g).astype(BF16), wq_ref[...], preferred_element_type=F32) for x1 in x1s]
    outs = [[] for _ in x1s]
    for hd in range(XA_HEADS):
        lo = hd * XA_HEAD_DIM
        ss = [_dot_nt(q[:, lo:lo + XA_HEAD_DIM], mk_ref[:, lo:lo + XA_HEAD_DIM]) * (XA_HEAD_DIM ** -0.5) for q in qs]
        for out, sc in zip(outs, ss):
            out.append(_dot(_softmax_lanes(sc), mv_ref[:, lo:lo + XA_HEAD_DIM]).astype(BF16))
    return [x1 + jnp.dot(jnp.concatenate(out, axis=1), wo_ref[...], preferred_element_type=F32)
            for x1, out in zip(x1s, outs)]


def _even_post_kernel(a_ref, b_ref, x_ref, wout_ref, gx_ref, wq_ref, wo_ref, mk_ref, mv_ref, o_ref):
    streams = _row_streams(POST_TILE)
    x1s = [x_ref[0, rs, :] + jnp.dot(jnp.concatenate([a_ref[0, rs, :], b_ref[0, rs, :]], axis=1), wout_ref[...],
                                     preferred_element_type=F32) for rs in streams]
    for rs, x2 in zip(streams, _xattn_streams(x1s, gx_ref[...], wq_ref, wo_ref, mk_ref, mv_ref)):
        o_ref[0, rs, :] = x2


def _even_post(a, bo, x, wout, gx, wq, wo, mk, mv):
    b, t, _ = x.shape
    tt = POST_TILE
    const = lambda shape: pl.BlockSpec(shape, lambda i, j: (0,) * len(shape))
    tile = lambda c: pl.BlockSpec((1, tt, c), lambda i, j: (i, j, 0))
    mem = pl.BlockSpec((N_MEM, D_MODEL), lambda i, j: (i, 0))
    return pl.pallas_call(
        _even_post_kernel,
        grid=(b, t // tt),
        in_specs=[tile(CONV_CH), tile(DN_WIDTH), tile(D_MODEL), const((D_MODEL, D_MODEL)), const((1, D_MODEL)),
                  const((D_MODEL, D_MODEL)), const((D_MODEL, D_MODEL)), mem, mem],
        out_specs=tile(D_MODEL),
        out_shape=jax.ShapeDtypeStruct((b, t, D_MODEL), F32),
        compiler_params=_params(2),
        name="p_even_post",
    )(a, bo, x, wout, gx, wq, wo, mk, mv)


def _pool_group_linear(pooled, wp_ref, bp_ref):
    outs = []
    for gi in range(len(POOL_WINDOWS)):
        lo = gi * POOL_GROUP
        outs.append(_dot(pooled[:, lo:lo + POOL_GROUP], wp_ref[gi]) + bp_ref[gi:gi + 1, :])
    return jnp.concatenate(outs, axis=1)


def _odd_kernel(x_ref, g_ref, win_ref, wp_ref, bp_ref, sc_ref, wout_ref, gx_ref, wq_ref, wo_ref, mk_ref, mv_ref,
                gf_ref, y_ref, pst_ref, pbuf):
    tt = POST_TILE
    j = pl.program_id(1)
    last = pl.num_programs(1) - 1
    streams = _row_streams(tt)

    @pl.when(j == 0)
    def _():
        pbuf[0:POOL_HIST, :] = jnp.zeros((POOL_HIST, D_MODEL), F32)

    @pl.when(j > 0)
    def _():
        pbuf[0:POOL_HIST, :] = pbuf[tt:tt + POOL_HIST, :]

    xs = [x_ref[0, rs, :] for rs in streams]
    us, gates = [], []
    for x, rs in zip(xs, streams):
        h = _rms(x, g_ref[...]).astype(BF16)
        u = jnp.dot(h, win_ref[:, 0:D_MODEL], preferred_element_type=F32)
        pbuf[POOL_HIST + rs.start:POOL_HIST + rs.stop, :] = u
        us.append(u)
        gates.append(_silu(jnp.dot(h, win_ref[:, D_MODEL:], preferred_element_type=F32)))

    x1s = []
    for x, u, gate, rs in zip(xs, us, gates, streams):
        pos = j * tt + rs.start + lax.broadcasted_iota(jnp.int32, (rs.stop - rs.start, 1), 0)
        acc = pbuf[rs.start:POOL_HIST + rs.stop, :]
        means = []
        for gi, win in enumerate(POOL_WINDOWS):
            acc = acc + pltpu.roll(acc, win // 2, 0)
            cnt = jnp.minimum(pos + 1, win).astype(F32)
            means.append(acc[POOL_HIST:, 0:POOL_GROUP] / cnt)
            acc = acc[:, POOL_GROUP:]
        pooled = jnp.concatenate(means, axis=1) - u
        z = _pool_group_linear(pooled, wp_ref, bp_ref) * sc_ref[...] * gate
        x1s.append(x + jnp.dot(z.astype(BF16), wout_ref[...], preferred_element_type=F32))
    for rs, x2 in zip(streams, _xattn_streams(x1s, gx_ref[...], wq_ref, wo_ref, mk_ref, mv_ref)):
        y_ref[0, rs, :] = _rms(x2, gf_ref[...])

    @pl.when(j == last)
    def _():
        pst_ref[0, 0] = pbuf[POOL_HIST + tt - POOL_BUF:POOL_HIST + tt, :]


def _odd(x, g, win, wp, bp, sc, wout, gx, wq, wo, mk, mv, gf):
    b, t, _ = x.shape
    tt = POST_TILE
    const = lambda shape: pl.BlockSpec(shape, lambda i, j: (0,) * len(shape))
    tile = lambda c: pl.BlockSpec((1, tt, c), lambda i, j: (i, j, 0))
    mem = pl.BlockSpec((N_MEM, D_MODEL), lambda i, j: (i, 0))
    ngrp = len(POOL_WINDOWS)
    return pl.pallas_call(
        _odd_kernel,
        grid=(b, t // tt),
        in_specs=[tile(D_MODEL), const((1, D_MODEL)), const((D_MODEL, 2 * D_MODEL)),
                  const((ngrp, POOL_GROUP, POOL_GROUP)), const((ngrp, POOL_GROUP)), const((1, D_MODEL)),
                  const((D_MODEL, D_MODEL)), const((1, D_MODEL)), const((D_MODEL, D_MODEL)),
                  const((D_MODEL, D_MODEL)), mem, mem, const((1, D_MODEL))],
        out_specs=[tile(D_MODEL), pl.BlockSpec((1, 1, POOL_BUF, D_MODEL), lambda i, j: (0, i, 0, 0))],
        out_shape=[jax.ShapeDtypeStruct((b, t, D_MODEL), F32),
                   jax.ShapeDtypeStruct((1, b, POOL_BUF, D_MODEL), F32)],
        scratch_shapes=[pltpu.VMEM((POOL_HIST + tt, D_MODEL), F32)],
        compiler_params=_params(2),
        name="p_odd_layer",
    )(x, g, win, wp, bp, sc, wout, gx, wq, wo, mk, mv, gf)


def _push_row(old_ref, new_ref, row):
    depth = old_ref.shape[1]
    new_ref[:, 0:depth - 1, :] = old_ref[:, 1:depth, :]
    for i in range(old_ref.shape[0]):
        new_ref[i, depth - 1:depth, :] = row[i:i + 1, :]


def _s_even_pre_kernel(x_ref, g_ref, w_ref, dww_ref, dwb_ref, lng_ref, lnb_ref, scw_ref, alog_ref, dtb_ref,
                       cin_ref, qin_ref,
                       aout_ref, q_ref, k_ref, v_ref, gb_ref, z_ref, cout_ref, qout_ref):
    h = _rms(x_ref[...], g_ref[...]).astype(BF16)
    glu = (_dot(h, w_ref[:, COL_GLU_VAL:COL_GLU_VAL + CONV_CH])
           * _sigmoid(_dot(h, w_ref[:, COL_GLU_GATE:COL_GLU_GATE + CONV_CH])))
    nb = CONV_K - 1
    acc = dww_ref[nb:nb + 1, :] * glu + jnp.sum(cin_ref[...] * dww_ref[0:nb, :][None], axis=1)
    _push_row(cin_ref, cout_ref, glu)
    c = _silu(_layer_norm(acc + dwb_ref[...], lng_ref[...], lnb_ref[...]))
    aout_ref[...] = c * _silu(_dot(h, w_ref[:, COL_GATE_A:COL_GATE_A + CONV_CH]))

    qkv = _dot(h, w_ref[:, COL_QKV:COL_QKV + QKV_CH])
    ns = SC_K - 1
    acc = scw_ref[ns:ns + 1, :] * qkv + jnp.sum(qin_ref[...] * scw_ref[0:ns, :][None], axis=1)
    _push_row(qin_ref, qout_ref, qkv)
    acc = _silu(acc)
    for hd in range(DN_HEADS):
        lo = hd * DN_HEAD_DIM
        q_ref[:, lo:lo + DN_HEAD_DIM] = _l2n(acc[:, lo:lo + DN_HEAD_DIM]) * (DN_HEAD_DIM ** -0.5)
        k_ref[:, lo:lo + DN_HEAD_DIM] = _l2n(acc[:, DN_WIDTH + lo:DN_WIDTH + lo + DN_HEAD_DIM])
    v_ref[...] = acc[:, 2 * DN_WIDTH:]
    z_ref[...] = _dot(h, w_ref[:, COL_Z:COL_Z + DN_WIDTH])
    gb_ref[...] = _gate_params(_dot(h, w_ref[:, COL_TAIL:COL_TAIL + LANES]), alog_ref[...], dtb_ref[...])


def _s_even_pre(x, g, w, dww, dwb, lng, lnb, scw, alog, dtb, cin, qin):
    n = x.shape[0]
    rbk = SAMPLE_ROW_BLOCK
    const = lambda shape: pl.BlockSpec(shape, lambda i: (0,) * len(shape))
    rows = lambda c: pl.BlockSpec((rbk, c), lambda i: (i, 0))
    hist = lambda arr: pl.BlockSpec((rbk,) + arr.shape[1:], lambda i: (i, 0, 0))
    return pl.pallas_call(
        _s_even_pre_kernel,
        grid=(n // rbk,),
        in_specs=[rows(D_MODEL), const((1, D_MODEL)), const((D_MODEL, EVEN_IN_PAD)), const((CONV_K, CONV_CH)),
                  const((1, CONV_CH)), const((1, CONV_CH)), const((1, CONV_CH)), const((SC_K, QKV_CH)),
                  const((1, LANES)), const((1, LANES)), hist(cin), hist(qin)],
        out_specs=[rows(CONV_CH), rows(DN_WIDTH), rows(DN_WIDTH), rows(DN_WIDTH), rows(LANES), rows(DN_WIDTH),
                   hist(cin), hist(qin)],
        out_shape=[jax.ShapeDtypeStruct((n, CONV_CH), F32),
                   jax.ShapeDtypeStruct((n, DN_WIDTH), F32),
                   jax.ShapeDtypeStruct((n, DN_WIDTH), F32),
                   jax.ShapeDtypeStruct((n, DN_WIDTH), F32),
                   jax.ShapeDtypeStruct((n, LANES), F32),
                   jax.ShapeDtypeStruct((n, DN_WIDTH), F32),
                   jax.ShapeDtypeStruct(cin.shape, F32),
                   jax.ShapeDtypeStruct(qin.shape, F32)],
        compiler_params=_params(1),
        name="s_even_pre",
    )(x, g, w, dww, dwb, lng, lnb, scw, alog, dtb, cin, qin)


def _s_delta_kernel(q_ref, k_ref, v_ref, gb_ref, z_ref, dng_ref, sin_ref, bout_ref, sout_ref):
    dh = DN_HEAD_DIM
    chains = [(i, hd) for i in range(DELTA_ROW_BLOCK) for hd in range(DN_HEADS)]
    vec = lambda ref, i, hd: ref[i:i + 1, hd * dh:(hd + 1) * dh]
    col = lambda ref, i, hd: jnp.broadcast_to(vec(ref, i, hd), (dh, dh)).T
    kcol = [col(k_ref, i, hd) for i, hd in chains]
    qcol = [col(q_ref, i, hd) for i, hd in chains]
    s = [sin_ref[0, i, hd] * jnp.exp(gb_ref[i:i + 1, DN_HEADS + hd:DN_HEADS + hd + 1]) for i, hd in chains]
    v_new = [(vec(v_ref, i, hd) - jnp.sum(kc * sc, axis=0, keepdims=True)) * gb_ref[i:i + 1, hd:hd + 1]
             for (i, hd), kc, sc in zip(chains, kcol, s)]
    s = [sc + kc * vn for sc, kc, vn in zip(s, kcol, v_new)]
    for (i, hd), sc, qc in zip(chains, s, qcol):
        sout_ref[0, i, hd] = sc
        o = jnp.sum(qc * sc, axis=0, keepdims=True)
        bout_ref[i:i + 1, hd * dh:(hd + 1) * dh] = _rms(o, dng_ref[...]) * _silu(vec(z_ref, i, hd))


def _s_delta(q, k, v, gb, z, dng, s_in):
    n = q.shape[0]
    rbk = DELTA_ROW_BLOCK
    rows = lambda c: pl.BlockSpec((rbk, c), lambda i: (i, 0))
    st = pl.BlockSpec((1, rbk, DN_HEADS, DN_HEAD_DIM, DN_HEAD_DIM), lambda i: (0, i, 0, 0, 0))
    return pl.pallas_call(
        _s_delta_kernel,
        grid=(n // rbk,),
        in_specs=[rows(DN_WIDTH), rows(DN_WIDTH), rows(DN_WIDTH), rows(LANES), rows(DN_WIDTH),
                  pl.BlockSpec((1, DN_HEAD_DIM), lambda i: (0, 0)), st],
        out_specs=[rows(DN_WIDTH), st],
        out_shape=[jax.ShapeDtypeStruct((n, DN_WIDTH), F32),
                   jax.ShapeDtypeStruct(s_in.shape, F32)],
        compiler_params=_params(1),
        name="s_delta",
    )(q, k, v, gb, z, dng, s_in)


def _s_mix_out_kernel(a_ref, b_ref, x_ref, wout_ref, gx_ref, wq_ref, x1_ref, q_ref):
    mix = jnp.concatenate([a_ref[...], b_ref[...]], axis=1).astype(BF16)
    x1 = x_ref[...] + jnp.dot(mix, wout_ref[...], preferred_element_type=F32)
    x1_ref[...] = x1
    q_ref[...] = jnp.dot(_rms(x1, gx_ref[...]).astype(BF16), wq_ref[...], preferred_element_type=F32)


def _s_mix_out(a, bo, x, wout, gx, wq):
    n = x.shape[0]
    full = lambda arr: pl.BlockSpec(arr.shape, lambda: (0,) * arr.ndim)
    args = (a, bo, x, wout, gx, wq)
    out = jax.ShapeDtypeStruct((n, D_MODEL), F32)
    return pl.pallas_call(
        _s_mix_out_kernel,
        in_specs=[full(v) for v in args],
        out_specs=[pl.BlockSpec((n, D_MODEL), lambda: (0, 0))] * 2,
        out_shape=[out, out],
        compiler_params=pltpu.CompilerParams(vmem_limit_bytes=VMEM_LIMIT_BYTES),
        name="s_mix_out",
    )(*args)


def _split_head_dim(x):
    lead = x.shape[:-2]
    halves = XA_HEAD_DIM // LANES
    x = x.reshape(lead + (XA_HEADS, halves, LANES))
    x = jnp.swapaxes(x, -3, -2)
    return x.reshape(lead + (halves * XA_HEADS, LANES))


def _merge_head_dim(x):
    lead = x.shape[:-2]
    halves = XA_HEAD_DIM // LANES
    x = jnp.swapaxes(x.reshape(lead + (halves, XA_HEADS, LANES)), -3, -2)
    return x.reshape(lead + (XA_HEADS, XA_HEAD_DIM))


def _s_xattn_kernel(q_ref, mk_ref, mv_ref, o_ref):
    for i in range(XATTN_ROW_BLOCK):
        prod = mk_ref[0, i] * q_ref[i]
        prod = prod + pltpu.roll(prod, XA_HEADS, 1)
        s = jnp.sum(prod, axis=-1, keepdims=True) * (XA_HEAD_DIM ** -0.5)
        e = jnp.exp(s - jnp.max(s, axis=0, keepdims=True))
        o_ref[i] = jnp.sum(e * mv_ref[0, i], axis=0) / jnp.sum(e, axis=0)


def _s_xattn(q, mk, mv, layer):
    n = q.shape[0]
    rbk = XATTN_ROW_BLOCK
    sub = XA_HEADS * XA_HEAD_DIM // LANES
    mem = pl.BlockSpec((1, rbk, N_MEM, sub, LANES), lambda i: (layer, i, 0, 0, 0))
    rows = pl.BlockSpec((rbk, sub, LANES), lambda i: (i, 0, 0))
    out = pl.pallas_call(
        _s_xattn_kernel,
        grid=(n // rbk,),
        in_specs=[rows, mem, mem],
        out_specs=rows,
        out_shape=jax.ShapeDtypeStruct((n, sub, LANES), F32),
        compiler_params=_params(1),
        name="s_xattn",
    )(_split_head_dim(q.reshape(n, XA_HEADS, XA_HEAD_DIM)), _split_head_dim(mk), _split_head_dim(mv))
    return _merge_head_dim(out).reshape(n, D_MODEL)


def _s_odd_kernel(x1_ref, o_ref, wo_ref, g_ref, win_ref, wp_ref, bp_ref, sc_ref, wout_ref, gx_ref, wq_ref, pin_ref,
                  x1o_ref, q_ref, pout_ref):
    x = x1_ref[...] + _dot(o_ref[...], wo_ref[...])
    h = _rms(x, g_ref[...]).astype(BF16)
    u = jnp.dot(h, win_ref[:, 0:D_MODEL], preferred_element_type=F32)
    gate = _silu(jnp.dot(h, win_ref[:, D_MODEL:], preferred_element_type=F32))
    means = []
    for gi, win in enumerate(POOL_WINDOWS):
        lo = gi * POOL_GROUP
        acc = u[:, lo:lo + POOL_GROUP] + jnp.sum(pin_ref[:, POOL_BUF - (win - 1):POOL_BUF, lo:lo + POOL_GROUP], axis=1)
        means.append(acc / float(min(PAST_LEN + 1, win)))
    pooled = jnp.concatenate(means, axis=1) - u
    _push_row(pin_ref, pout_ref, u)
    z = _pool_group_linear(pooled, wp_ref, bp_ref) * sc_ref[...] * gate
    x1 = x + jnp.dot(z.astype(BF16), wout_ref[...], preferred_element_type=F32)
    x1o_ref[...] = x1
    q_ref[...] = jnp.dot(_rms(x1, gx_ref[...]).astype(BF16), wq_ref[...], preferred_element_type=F32)


def _s_odd(x1, o, wo, g, win, wp, bp, sc, wout, gx, wq, pin):
    n = x1.shape[0]
    rbk = SAMPLE_ROW_BLOCK
    ngrp = len(POOL_WINDOWS)
    const = lambda shape: pl.BlockSpec(shape, lambda i: (0,) * len(shape))
    rows = lambda c: pl.BlockSpec((rbk, c), lambda i: (i, 0))
    hist = pl.BlockSpec((rbk,) + pin.shape[1:], lambda i: (i, 0, 0))
    out = jax.ShapeDtypeStruct((n, D_MODEL), F32)
    return pl.pallas_call(
        _s_odd_kernel,
        grid=(n // rbk,),
        in_specs=[rows(D_MODEL), rows(D_MODEL), const((D_MODEL, D_MODEL)), const((1, D_MODEL)),
                  const((D_MODEL, 2 * D_MODEL)), const((ngrp, POOL_GROUP, POOL_GROUP)), const((ngrp, POOL_GROUP)),
                  const((1, D_MODEL)), const((D_MODEL, D_MODEL)), const((1, D_MODEL)), const((D_MODEL, D_MODEL)),
                  hist],
        out_specs=[rows(D_MODEL), rows(D_MODEL), hist],
        out_shape=[out, out, jax.ShapeDtypeStruct(pin.shape, F32)],
        compiler_params=_params(1),
        name="s_odd",
    )(x1, o, wo, g, win, wp, bp, sc, wout, gx, wq, pin)


def _s_final_kernel(x1_ref, o_ref, wo_ref, gf_ref, y_ref):
    y_ref[...] = _rms(x1_ref[...] + _dot(o_ref[...], wo_ref[...]), gf_ref[...])


def _s_final(x1, o, wo, gf):
    n = x1.shape[0]
    full = lambda arr: pl.BlockSpec(arr.shape, lambda: (0,) * arr.ndim)
    args = (x1, o, wo, gf)
    return pl.pallas_call(
        _s_final_kernel,
        in_specs=[full(v) for v in args],
        out_specs=pl.BlockSpec((n, D_MODEL), lambda: (0, 0)),
        out_shape=jax.ShapeDtypeStruct((n, D_MODEL), F32),
        compiler_params=pltpu.CompilerParams(vmem_limit_bytes=VMEM_LIMIT_BYTES),
        name="s_final",
    )(*args)


def _lane_pad(vec, offset):
    return jnp.pad(vec.astype(F32), (offset, LANES - offset - vec.shape[0])).reshape(1, LANES)


def kernel(x_prompt, x_sample, state_conv_a, state_qkv_conv, state_delta, state_pool, cache_mem_k, cache_mem_v, mem_prompt, norm_mix, norm_xattn, norm_final, w_in_even, w_out_even, dw_w, dw_b, ln_a_g, ln_a_b, sc_w, a_log, dt_bias, dn_norm_g, w_in_odd, w_pool, b_pool, pool_scale, w_out_odd, w_xq, w_xk, w_xv, w_xo):
    bp, t, _ = x_prompt.shape
    ns = x_sample.shape[0]
    row = lambda v: v.reshape(1, -1)

    w_in0 = jnp.pad(w_in_even[0], ((0, 0), (0, EVEN_IN_PAD - EVEN_IN))).astype(BF16)
    w_out0 = w_out_even[0].astype(BF16)
    w_in1 = w_in_odd[0].astype(BF16)
    w_pool1 = w_pool[0].astype(BF16)
    w_out1 = w_out_odd[0].astype(BF16)
    wq = w_xq.astype(BF16)
    wk = w_xk.astype(BF16)
    wv = w_xv.astype(BF16)
    wo = w_xo.astype(BF16)
    alog = _lane_pad(a_log[0], DN_HEADS)
    dtb = _lane_pad(dt_bias[0], DN_HEADS)
    even_small = (dw_w[0], row(dw_b[0]), row(ln_a_g[0]), row(ln_a_b[0]), sc_w[0], alog, dtb)
    dng = row(dn_norm_g[0])

    mk_f, mv_f, mk_b, mv_b = _mem_kv(mem_prompt.reshape(bp * N_MEM, D_MODEL), wk, wv)
    new_mem_k_p = mk_f.reshape(DEPTH, bp, N_MEM, XA_HEADS, XA_HEAD_DIM)
    new_mem_v_p = mv_f.reshape(DEPTH, bp, N_MEM, XA_HEADS, XA_HEAD_DIM)

    even_small_p = (jnp.repeat(dw_w[0], SUBLANES, axis=0),) + even_small[1:4] + (
        jnp.repeat(sc_w[0], SUBLANES, axis=0),) + even_small[5:]
    a_out, z, gb, u, wqd, kd, qk, new_conv_a_p, new_qkv_conv_p = _even_front(x_prompt, row(norm_mix[0]), w_in0,
                                                                             *even_small_p)
    b_out, new_delta_p = _delta_scan(u, wqd, kd, qk, gb, z, dng)
    x2 = _even_post(a_out, b_out, x_prompt, w_out0, row(norm_xattn[0]), wq[0], wo[0], mk_b[0], mv_b[0])
    y_prompt, new_pool_p = _odd(x2, row(norm_mix[1]), w_in1, w_pool1, b_pool[0], row(pool_scale[0]), w_out1,
                                row(norm_xattn[1]), wq[1], wo[1], mk_b[1], mv_b[1], row(norm_final))

    xs = x_sample.reshape(ns, D_MODEL)

    sa, sq, sk, sv, sgb, sz, cout, qout = _s_even_pre(xs, row(norm_mix[0]), w_in0, *even_small, state_conv_a[0], state_qkv_conv[0])
    sb, new_delta_s = _s_delta(sq, sk, sv, sgb, sz, dng, state_delta)
    x1, xq = _s_mix_out(sa, sb, xs, w_out0, row(norm_xattn[0]), wq[0])
    o0 = _s_xattn(xq, cache_mem_k, cache_mem_v, 0)
    x1, xq, pout = _s_odd(x1, o0, wo[0], row(norm_mix[1]), w_in1, w_pool1, b_pool[0], row(pool_scale[0]), w_out1,
                          row(norm_xattn[1]), wq[1], state_pool[0])
    o1 = _s_xattn(xq, cache_mem_k, cache_mem_v, 1)
    y_sample = _s_final(x1, o1, wo[1], row(norm_final)).reshape(ns, 1, D_MODEL)

    new_conv_a_s, new_qkv_conv_s, new_pool_s = cout[None], qout[None], pout[None]
    return (y_prompt, y_sample, new_conv_a_p, new_qkv_conv_p, new_delta_p, new_pool_p, new_mem_k_p,
            new_mem_v_p, new_conv_a_s, new_qkv_conv_s, new_delta_s, new_pool_s)
```

```python
import functools

import jax
import jax.numpy as jnp
from jax import lax
from jax.experimental import pallas as pl
from jax.experimental.pallas import tpu as pltpu

F32 = jnp.float32
BF16 = jnp.bfloat16

D_MODEL = 1024
DEPTH = 2
PAST_LEN = 16384
CONV_CH = 512
CONV_K = 31
DN_HEAD_DIM = 128
DN_HEADS = 4
DN_WIDTH = 512
QKV_CH = 1536
SC_K = 4
DN_CHUNK = 64
EVEN_IN = 3592
POOL_WINDOWS = (2, 4, 8, 16)
POOL_GROUP = 256
POOL_BUF = 15
N_MEM = 256
XA_HEADS = 4
XA_HEAD_DIM = 256
EPS = 1e-6

LANES = 128
SUBLANES = 8
VMEM_LIMIT_BYTES = 56 * 1024 * 1024

COL_GLU_VAL = 0
COL_GLU_GATE = 512
COL_GATE_A = 1024
COL_QKV = 1536
COL_Z = 3072
COL_TAIL = 3584
EVEN_IN_PAD = COL_TAIL + LANES

PROMPT_TILE = 256
POST_TILE = 512
POST_STREAMS = 1
CONV_HIST = 32
SC_HIST = 8
POOL_HIST = 16
CONV_ROW_BLOCK = 64
CONV_COL_BLOCK = 256
SAMPLE_ROW_BLOCK = 32
DELTA_ROW_BLOCK = 8
XATTN_ROW_BLOCK = 4


def _params(n_axes):
    return pltpu.CompilerParams(dimension_semantics=("arbitrary",) * n_axes,
                                vmem_limit_bytes=VMEM_LIMIT_BYTES)


def _dot(a, b):
    return jnp.dot(a.astype(BF16), b.astype(BF16), preferred_element_type=F32)


def _dot_nt(a, b):
    return lax.dot_general(a.astype(BF16), b.astype(BF16), (((1,), (1,)), ((), ())),
                           preferred_element_type=F32)


def _split3(x):
    x1 = x.astype(BF16)
    r1 = x - x1.astype(F32)
    x2 = r1.astype(BF16)
    x3 = (r1 - x2.astype(F32)).astype(BF16)
    return x1, x2, x3


def _split2(x):
    x1 = x.astype(BF16)
    return x1, (x - x1.astype(F32)).astype(BF16)


def _dot_hi(a, b):
    a1, a2 = _split2(a)
    b1, b2 = _split2(b)
    d = functools.partial(jnp.dot, preferred_element_type=F32)
    return d(a1, b1) + (d(a1, b2) + d(a2, b1))


def _bmm(a, b):
    return lax.dot_general(a.astype(BF16), b.astype(BF16), (((2,), (1,)), ((0,), (0,))),
                           preferred_element_type=F32)


def _bmm_nt(a, b):
    return lax.dot_general(a.astype(BF16), b.astype(BF16), (((2,), (2,)), ((0,), (0,))),
                           preferred_element_type=F32)


def _bmm_tn(a, b):
    return lax.dot_general(a.astype(BF16), b.astype(BF16), (((1,), (1,)), ((0,), (0,))),
                           preferred_element_type=F32)


def _bmm_hi(a, b):
    a1, a2 = _split2(a)
    b1, b2 = _split2(b)
    return _bmm(a1, b1) + (_bmm(a1, b2) + _bmm(a2, b1))


def _dot_exact_lhs(a_bf, b):
    b1, b2, b3 = _split3(b)
    d = functools.partial(jnp.dot, preferred_element_type=F32)
    return d(a_bf, b1) + d(a_bf, b2) + d(a_bf, b3)


def _sigmoid(x):
    return 1.0 / (1.0 + jnp.exp(-x))


def _silu(x):
    return x * _sigmoid(x)


def _softplus(x):
    return jnp.maximum(x, 0.0) + jnp.log(1.0 + jnp.exp(-jnp.abs(x)))


def _rms(x, g):
    return x * lax.rsqrt(jnp.mean(x * x, axis=-1, keepdims=True) + EPS) * g


def _layer_norm(x, g, b):
    xc = x - jnp.mean(x, axis=-1, keepdims=True)
    return xc * lax.rsqrt(jnp.mean(xc * xc, axis=-1, keepdims=True) + EPS) * g + b


def _l2n(x):
    return x * lax.rsqrt(jnp.sum(x * x, axis=-1, keepdims=True) + EPS)


def _gate_params(tail, alog, dtb):
    lane = lax.broadcasted_iota(jnp.int32, tail.shape, 1)
    beta = _sigmoid(tail)
    g = -jnp.exp(alog) * _softplus(tail + dtb)
    return jnp.where(lane < DN_HEADS, beta, g)


def _softmax_lanes(s):
    m = jnp.max(s, axis=-1, keepdims=True)
    e = jnp.exp(s - m)
    return e / jnp.sum(e, axis=-1, keepdims=True)


def _mem_kv_kernel(x_ref, wk_ref, wv_ref, k_ref, v_ref, kb_ref, vb_ref):
    x = x_ref[...]
    k = _dot(x, wk_ref[0])
    v = _dot(x, wv_ref[0])
    k_ref[0] = k
    v_ref[0] = v
    kb_ref[0] = k.astype(BF16)
    vb_ref[0] = v.astype(BF16)


def _mem_kv(mem2d, wk, wv):
    rows = mem2d.shape[0]
    tile = 512
    f32_out = jax.ShapeDtypeStruct((DEPTH, rows, D_MODEL), F32)
    bf_out = jax.ShapeDtypeStruct((DEPTH, rows, D_MODEL), BF16)
    w_spec = pl.BlockSpec((1, D_MODEL, D_MODEL), lambda l, i: (l, 0, 0))
    o_spec = pl.BlockSpec((1, tile, D_MODEL), lambda l, i: (l, i, 0))
    return pl.pallas_call(
        _mem_kv_kernel,
        grid=(DEPTH, rows // tile),
        in_specs=[pl.BlockSpec((tile, D_MODEL), lambda l, i: (i, 0)), w_spec, w_spec],
        out_specs=[o_spec, o_spec, o_spec, o_spec],
        out_shape=[f32_out, f32_out, bf_out, bf_out],
        compiler_params=_params(2),
        name="p_mem_kv",
    )(mem2d, wk, wv)


def _causal_conv(buf_ref, w_ref, n_taps, hist, r0, width):
    rb = CONV_ROW_BLOCK
    off = hist - (n_taps - 1)
    cols = []
    for c0 in range(0, width, CONV_COL_BLOCK):
        cs = slice(c0, c0 + CONV_COL_BLOCK)
        total = None
        for res in range(min(SUBLANES, n_taps)):
            base = (off + res) // SUBLANES * SUBLANES
            shift = off + res - base
            span = rb + (SUBLANES if shift else 0)
            part = None
            for kk in range(res, n_taps, SUBLANES):
                lo = r0 + base + kk - res
                rows = buf_ref[lo:lo + span, cs].reshape(span // SUBLANES, SUBLANES, CONV_COL_BLOCK)
                term = rows * w_ref[kk * SUBLANES:(kk + 1) * SUBLANES, cs][None]
                part = term if part is None else part + term
            part = part.reshape(span, CONV_COL_BLOCK)[shift:shift + rb, :]
            total = part if total is None else total + part
        cols.append(total)
    return jnp.concatenate(cols, axis=1)


def _even_front_kernel(x_ref, g_ref, w_ref, dww_ref, dwb_ref, lng_ref, lnb_ref, scw_ref, alog_ref, dtb_ref,
                       aout_ref, z_ref, gb_ref, u_ref, wq_ref, kd_ref, qk_ref, cst_ref, qst_ref,
                       cbuf, qbuf, qkv):
    tt = PROMPT_TILE
    j = pl.program_id(1)
    last = pl.num_programs(1) - 1

    @pl.when(j == 0)
    def _():
        cbuf[0:CONV_HIST, :] = jnp.zeros((CONV_HIST, CONV_CH), F32)
        qbuf[0:SC_HIST, :] = jnp.zeros((SC_HIST, QKV_CH), F32)

    @pl.when(j > 0)
    def _():
        cbuf[0:CONV_HIST, :] = cbuf[tt:tt + CONV_HIST, :]
        qbuf[0:SC_HIST, :] = qbuf[tt:tt + SC_HIST, :]

    h = _rms(x_ref[0], g_ref[...]).astype(BF16)

    qbuf[SC_HIST:SC_HIST + tt, :] = _dot(h, w_ref[:, COL_QKV:COL_QKV + QKV_CH])
    gb = _gate_params(_dot(h, w_ref[:, COL_TAIL:COL_TAIL + LANES]), alog_ref[...], dtb_ref[...])
    gb_ref[0] = gb
    for r0 in range(0, tt, CONV_ROW_BLOCK):
        rows = slice(r0, r0 + CONV_ROW_BLOCK)
        acc = _silu(_causal_conv(qbuf, scw_ref, SC_K, SC_HIST, r0, QKV_CH))
        for hd in range(DN_HEADS):
            lo = hd * DN_HEAD_DIM
            qkv[rows, lo:lo + DN_HEAD_DIM] = _l2n(acc[:, lo:lo + DN_HEAD_DIM]) * (DN_HEAD_DIM ** -0.5)
            qkv[rows, DN_WIDTH + lo:DN_WIDTH + lo + DN_HEAD_DIM] = _l2n(
                acc[:, DN_WIDTH + lo:DN_WIDTH + lo + DN_HEAD_DIM])
        qkv[rows, 2 * DN_WIDTH:] = acc[:, 2 * DN_WIDTH:]
    cbuf[CONV_HIST:CONV_HIST + tt, :] = (_dot(h, w_ref[:, COL_GLU_VAL:COL_GLU_VAL + CONV_CH])
                                         * _sigmoid(_dot(h, w_ref[:, COL_GLU_GATE:COL_GLU_GATE + CONV_CH])))
    gate_a = _silu(_dot(h, w_ref[:, COL_GATE_A:COL_GATE_A + CONV_CH]))

    def conv_rows(r0):
        acc = _causal_conv(cbuf, dww_ref, CONV_K, CONV_HIST, r0, CONV_CH)
        c = _silu(_layer_norm(acc + dwb_ref[...], lng_ref[...], lnb_ref[...]))
        aout_ref[0, r0:r0 + CONV_ROW_BLOCK, :] = (c * gate_a[r0:r0 + CONV_ROW_BLOCK, :]).astype(BF16)

    _delta_prep_tile(qkv[:, 0:DN_WIDTH], qkv[:, DN_WIDTH:2 * DN_WIDTH], qkv[:, 2 * DN_WIDTH:], gb,
                     u_ref, wq_ref, kd_ref, qk_ref,
                     fillers=[functools.partial(conv_rows, r0) for r0 in range(0, tt, CONV_ROW_BLOCK)])
    z_ref[0] = _dot(h, w_ref[:, COL_Z:COL_Z + DN_WIDTH])

    @pl.when(j == last)
    def _():
        cst_ref[0, 0] = cbuf[CONV_HIST + tt - (CONV_K - 1):CONV_HIST + tt, :]
        qst_ref[0, 0] = qbuf[SC_HIST + tt - (SC_K - 1):SC_HIST + tt, :]


def _even_front(x, g, w, dww, dwb, lng, lnb, scw, alog, dtb):
    b, t, _ = x.shape
    tt = PROMPT_TILE
    c = DN_CHUNK
    const = lambda shape: pl.BlockSpec(shape, lambda i, j: (0,) * len(shape))
    tile = lambda ch: pl.BlockSpec((1, tt, ch), lambda i, j: (i, j, 0))
    return pl.pallas_call(
        _even_front_kernel,
        grid=(b, t // tt),
        in_specs=[tile(D_MODEL), const((1, D_MODEL)), const((D_MODEL, EVEN_IN_PAD)),
                  const((CONV_K * SUBLANES, CONV_CH)), const((1, CONV_CH)), const((1, CONV_CH)), const((1, CONV_CH)),
                  const((SC_K * SUBLANES, QKV_CH)), const((1, LANES)), const((1, LANES))],
        out_specs=[tile(CONV_CH), tile(DN_WIDTH), tile(LANES), tile(DN_WIDTH),
                   pl.BlockSpec((1, tt // c, 2 * c, DN_WIDTH), lambda i, j: (i, j, 0, 0)),
                   tile(DN_WIDTH),
                   pl.BlockSpec((1, DN_HEADS, tt, c), lambda i, j: (i, 0, j, 0)),
                   pl.BlockSpec((1, 1, CONV_K - 1, CONV_CH), lambda i, j: (0, i, 0, 0)),
                   pl.BlockSpec((1, 1, SC_K - 1, QKV_CH), lambda i, j: (0, i, 0, 0))],
        out_shape=[jax.ShapeDtypeStruct((b, t, CONV_CH), BF16),
                   jax.ShapeDtypeStruct((b, t, DN_WIDTH), F32),
                   jax.ShapeDtypeStruct((b, t, LANES), F32),
                   jax.ShapeDtypeStruct((b, t, DN_WIDTH), F32),
                   jax.ShapeDtypeStruct((b, t // c, 2 * c, DN_WIDTH), BF16),
                   jax.ShapeDtypeStruct((b, t, DN_WIDTH), BF16),
                   jax.ShapeDtypeStruct((b, DN_HEADS, t, c), BF16),
                   jax.ShapeDtypeStruct((1, b, CONV_K - 1, CONV_CH), F32),
                   jax.ShapeDtypeStruct((1, b, SC_K - 1, QKV_CH), F32)],
        scratch_shapes=[pltpu.VMEM((CONV_HIST + tt, CONV_CH), F32),
                        pltpu.VMEM((SC_HIST + tt, QKV_CH), F32),
                        pltpu.VMEM((tt, QKV_CH), F32)],
        compiler_params=_params(2),
        name="p_even_front",
    )(x, g, w, dww, dwb, lng, lnb, scw, alog, dtb)


def _delta_prep_tile(q2, k2, v2, gb, u_ref, wq_ref, kd_ref, qk_ref, fillers=()):
    fillers = iter(fillers)
    tt = PROMPT_TILE
    c = DN_CHUNK
    dh = DN_HEAD_DIM
    nch = tt // c
    row = lax.broadcasted_iota(jnp.int32, (c, c), 0)
    col = lax.broadcasted_iota(jnp.int32, (c, c), 1)
    incl = (row >= col)[None]
    strict = (row > col)[None]
    eye = jnp.where(row == col, 1.0, 0.0)[None]
    trow = lax.broadcasted_iota(jnp.int32, (tt, tt), 0)
    tcol = lax.broadcasted_iota(jnp.int32, (tt, tt), 1)
    same_chunk = jnp.right_shift(trow, 6) == jnp.right_shift(tcol, 6)
    tri = jnp.where((trow >= tcol) & same_chunk, 1.0, 0.0).astype(BF16)

    def lane_rep(col0):
        return jnp.concatenate([jnp.broadcast_to(gb[:, col0 + hd:col0 + hd + 1], (tt, dh))
                                for hd in range(DN_HEADS)], axis=1)

    def chunks(x):
        return jnp.concatenate([x[:, hd * dh:(hd + 1) * dh].reshape(nch, c, dh) for hd in range(DN_HEADS)], axis=0)

    gc4 = _dot_exact_lhs(tri, lane_rep(DN_HEADS))
    gc = chunks(gc4)
    gc_rows = []
    for hd in range(DN_HEADS):
        gc_t = gc4[:, hd * dh:(hd + 1) * dh].T
        gc_rows += [gc_t[0:c, n * c:(n + 1) * c][None] for n in range(nch)]
    gc_row = jnp.concatenate(gc_rows, axis=0)
    beta = chunks(lane_rep(0))
    q = chunks(q2)
    k = chunks(k2)
    v = chunks(v2)
    egc = jnp.exp(gc)
    kb = k * beta
    decay = jnp.where(incl, jnp.exp(jnp.where(incl, gc[:, :, 0:c] - gc_row, 0.0)), 0.0)
    a = jnp.where(strict, _bmm_nt(kb, k) * decay, 0.0)
    rhs = jnp.concatenate([v * beta, kb * egc], axis=2)
    p = -a
    t_inv = eye + p
    for _ in range(5):
        p = _bmm(p, p)
        t_inv = t_inv + _bmm(p, t_inv)
        next(fillers, lambda: None)()
    x0 = _bmm(t_inv, rhs)
    rho = rhs - x0 - _bmm_hi(a, x0)
    x = x0 + _bmm(t_inv, rho)
    k_dec = k * jnp.exp(gc[:, c - 1:c, :] - gc)
    q_dec = q * egc
    qk = jnp.where(incl, _bmm_nt(q, k) * decay, 0.0)
    for hd in range(DN_HEADS):
        lo = hd * dh
        hs = slice(hd * nch, (hd + 1) * nch)
        u_ref[0, :, lo:lo + dh] = x[hs, :, 0:dh].reshape(tt, dh)
        wq_ref[0, :, 0:c, lo:lo + dh] = x[hs, :, dh:].astype(BF16)
        wq_ref[0, :, c:2 * c, lo:lo + dh] = q_dec[hs].astype(BF16)
        kd_ref[0, :, lo:lo + dh] = k_dec[hs].reshape(tt, dh).astype(BF16)
        qk_ref[0, hd] = qk[hs].reshape(tt, c).astype(BF16)
    for rest in fillers:
        rest()


def _delta_scan_kernel(u_ref, wq_ref, kd_ref, qk_ref, gb_ref, z_ref, dng_ref, bout_ref, sout_ref, s_ref):
    c = DN_CHUNK
    dh = DN_HEAD_DIM
    nb = u_ref.shape[0]
    n = pl.program_id(0)
    last = pl.num_programs(0) - 1

    @pl.when(n == 0)
    def _():
        s_ref[...] = jnp.zeros(s_ref.shape, F32)

    def heads(ref):
        return jnp.concatenate([ref[:, :, hd * dh:(hd + 1) * dh] for hd in range(DN_HEADS)], axis=0)

    g_tot = jnp.sum(gb_ref[...], axis=1, keepdims=True)
    g_last = jnp.exp(jnp.concatenate(
        [jnp.broadcast_to(g_tot[:, :, DN_HEADS + hd:DN_HEADS + hd + 1], (nb, 1, dh)) for hd in range(DN_HEADS)],
        axis=0))
    wq = jnp.concatenate([wq_ref[:, 0, :, hd * dh:(hd + 1) * dh] for hd in range(DN_HEADS)], axis=0)
    qk = jnp.concatenate([qk_ref[:, hd] for hd in range(DN_HEADS)], axis=0)
    s = s_ref[...]
    ws = _bmm(wq, s)
    v_new = heads(u_ref) - ws[:, 0:c]
    o = ws[:, c:] + _bmm(qk, v_new)
    s_ref[...] = s * g_last + _bmm_tn(heads(kd_ref), v_new)
    out = (_rms(o, dng_ref[...]) * _silu(heads(z_ref))).astype(BF16)
    for hd in range(DN_HEADS):
        bout_ref[:, :, hd * dh:(hd + 1) * dh] = out[hd * nb:(hd + 1) * nb]

    @pl.when(n == last)
    def _():
        for hd in range(DN_HEADS):
            sout_ref[0, :, hd] = s_ref[hd * nb:(hd + 1) * nb]


def _delta_scan(u, wq, kd, qk, gb, z, dng):
    b, t, _ = u.shape
    c = DN_CHUNK
    rows = lambda ch: pl.BlockSpec((b, c, ch), lambda n: (0, n, 0))
    return pl.pallas_call(
        _delta_scan_kernel,
        grid=(t // c,),
        in_specs=[rows(DN_WIDTH),
                  pl.BlockSpec((b, 1, 2 * c, DN_WIDTH), lambda n: (0, n, 0, 0)),
                  rows(DN_WIDTH),
                  pl.BlockSpec((b, DN_HEADS, c, c), lambda n: (0, 0, n, 0)),
                  rows(LANES), rows(DN_WIDTH),
                  pl.BlockSpec((1, DN_HEAD_DIM), lambda n: (0, 0))],
        out_specs=[rows(DN_WIDTH),
                   pl.BlockSpec((1, b, DN_HEADS, DN_HEAD_DIM, DN_HEAD_DIM), lambda n: (0, 0, 0, 0, 0))],
        out_shape=[jax.ShapeDtypeStruct((b, t, DN_WIDTH), BF16),
                   jax.ShapeDtypeStruct((1, b, DN_HEADS, DN_HEAD_DIM, DN_HEAD_DIM), F32)],
        scratch_shapes=[pltpu.VMEM((DN_HEADS * b, DN_HEAD_DIM, DN_HEAD_DIM), F32)],
        compiler_params=_params(1),
        name="p_delta_scan",
    )(u, wq, kd, qk, gb, z, dng)


def _row_streams(tt):
    rows = tt // POST_STREAMS
    return [slice(i * rows, (i + 1) * rows) for i in range(POST_STREAMS)]


def _xattn_streams(x1s, g, wq_ref, wo_ref, mk_ref, mv_ref):
    qs = [jnp.dot(_rms(x1, g).astype(BF16), wq_ref[...], preferred_element_type=F32) for x1 in x1s]
    outs = [[] for _ in x1s]
    for hd in range(XA_HEADS):
        lo = hd * XA_HEAD_DIM
        ss = [_dot_nt(q[:, lo:lo + XA_HEAD_DIM], mk_ref[:, lo:lo + XA_HEAD_DIM]) * (XA_HEAD_DIM ** -0.5) for q in qs]
        for out, sc in zip(outs, ss):
            out.append(_dot(_softmax_lanes(sc), mv_ref[:, lo:lo + XA_HEAD_DIM]).astype(BF16))
    return [x1 + jnp.dot(jnp.concatenate(out, axis=1), wo_ref[...], preferred_element_type=F32)
            for x1, out in zip(x1s, outs)]


def _even_post_kernel(a_ref, b_ref, x_ref, wout_ref, gx_ref, wq_ref, wo_ref, mk_ref, mv_ref, o_ref):
    streams = _row_streams(POST_TILE)
    x1s = [x_ref[0, rs, :] + jnp.dot(jnp.concatenate([a_ref[0, rs, :], b_ref[0, rs, :]], axis=1), wout_ref[...],
                                     preferred_element_type=F32) for rs in streams]
    for rs, x2 in zip(streams, _xattn_streams(x1s, gx_ref[...], wq_ref, wo_ref, mk_ref, mv_ref)):
        o_ref[0, rs, :] = x2


def _even_post(a, bo, x, wout, gx, wq, wo, mk, mv):
    b, t, _ = x.shape
    tt = POST_TILE
    const = lambda shape: pl.BlockSpec(shape, lambda i, j: (0,) * len(shape))
    tile = lambda c: pl.BlockSpec((1, tt, c), lambda i, j: (i, j, 0))
    mem = pl.BlockSpec((N_MEM, D_MODEL), lambda i, j: (i, 0))
    return pl.pallas_call(
        _even_post_kernel,
        grid=(b, t // tt),
        in_specs=[tile(CONV_CH), tile(DN_WIDTH), tile(D_MODEL), const((D_MODEL, D_MODEL)), const((1, D_MODEL)),
                  const((D_MODEL, D_MODEL)), const((D_MODEL, D_MODEL)), mem, mem],
        out_specs=tile(D_MODEL),
        out_shape=jax.ShapeDtypeStruct((b, t, D_MODEL), F32),
        compiler_params=_params(2),
        name="p_even_post",
    )(a, bo, x, wout, gx, wq, wo, mk, mv)


def _pool_group_linear(pooled, wp_ref, bp_ref):
    outs = []
    for gi in range(len(POOL_WINDOWS)):
        lo = gi * POOL_GROUP
        outs.append(_dot(pooled[:, lo:lo + POOL_GROUP], wp_ref[gi]) + bp_ref[gi:gi + 1, :])
    return jnp.concatenate(outs, axis=1)


def _odd_kernel(x_ref, g_ref, win_ref, wp_ref, bp_ref, sc_ref, wout_ref, gx_ref, wq_ref, wo_ref, mk_ref, mv_ref,
                gf_ref, y_ref, pst_ref, pbuf):
    tt = POST_TILE
    j = pl.program_id(1)
    last = pl.num_programs(1) - 1
    streams = _row_streams(tt)

    @pl.when(j == 0)
    def _():
        pbuf[0:POOL_HIST, :] = jnp.zeros((POOL_HIST, D_MODEL), F32)

    @pl.when(j > 0)
    def _():
        pbuf[0:POOL_HIST, :] = pbuf[tt:tt + POOL_HIST, :]

    xs = [x_ref[0, rs, :] for rs in streams]
    us, gates = [], []
    for x, rs in zip(xs, streams):
        h = _rms(x, g_ref[...]).astype(BF16)
        u = jnp.dot(h, win_ref[:, 0:D_MODEL], preferred_element_type=F32)
        pbuf[POOL_HIST + rs.start:POOL_HIST + rs.stop, :] = u
        us.append(u)
        gates.append(_silu(jnp.dot(h, win_ref[:, D_MODEL:], preferred_element_type=F32)))

    x1s = []
    for x, u, gate, rs in zip(xs, us, gates, streams):
        pos = j * tt + rs.start + lax.broadcasted_iota(jnp.int32, (rs.stop - rs.start, 1), 0)
        acc = pbuf[rs.start:POOL_HIST + rs.stop, :]
        means = []
        for gi, win in enumerate(POOL_WINDOWS):
            acc = acc + pltpu.roll(acc, win // 2, 0)
            cnt = jnp.minimum(pos + 1, win).astype(F32)
            means.append(acc[POOL_HIST:, 0:POOL_GROUP] / cnt)
            acc = acc[:, POOL_GROUP:]
        pooled = jnp.concatenate(means, axis=1) - u
        z = _pool_group_linear(pooled, wp_ref, bp_ref) * sc_ref[...] * gate
        x1s.append(x + jnp.dot(z.astype(BF16), wout_ref[...], preferred_element_type=F32))
    for rs, x2 in zip(streams, _xattn_streams(x1s, gx_ref[...], wq_ref, wo_ref, mk_ref, mv_ref)):
        y_ref[0, rs, :] = _rms(x2, gf_ref[...])

    @pl.when(j == last)
    def _():
        pst_ref[0, 0] = pbuf[POOL_HIST + tt - POOL_BUF:POOL_HIST + tt, :]


def _odd(x, g, win, wp, bp, sc, wout, gx, wq, wo, mk, mv, gf):
    b, t, _ = x.shape
    tt = POST_TILE
    const = lambda shape: pl.BlockSpec(shape, lambda i, j: (0,) * len(shape))
    tile = lambda c: pl.BlockSpec((1, tt, c), lambda i, j: (i, j, 0))
    mem = pl.BlockSpec((N_MEM, D_MODEL), lambda i, j: (i, 0))
    ngrp = len(POOL_WINDOWS)
    return pl.pallas_call(
        _odd_kernel,
        grid=(b, t // tt),
        in_specs=[tile(D_MODEL), const((1, D_MODEL)), const((D_MODEL, 2 * D_MODEL)),
                  const((ngrp, POOL_GROUP, POOL_GROUP)), const((ngrp, POOL_GROUP)), const((1, D_MODEL)),
                  const((D_MODEL, D_MODEL)), const((1, D_MODEL)), const((D_MODEL, D_MODEL)),
                  const((D_MODEL, D_MODEL)), mem, mem, const((1, D_MODEL))],
        out_specs=[tile(D_MODEL), pl.BlockSpec((1, 1, POOL_BUF, D_MODEL), lambda i, j: (0, i, 0, 0))],
        out_shape=[jax.ShapeDtypeStruct((b, t, D_MODEL), F32),
                   jax.ShapeDtypeStruct((1, b, POOL_BUF, D_MODEL), F32)],
        scratch_shapes=[pltpu.VMEM((POOL_HIST + tt, D_MODEL), F32)],
        compiler_params=_params(2),
        name="p_odd_layer",
    )(x, g, win, wp, bp, sc, wout, gx, wq, wo, mk, mv, gf)


def _push_row(old_ref, new_ref, row):
    depth = old_ref.shape[0]
    new_ref[0:depth - 1] = old_ref[1:depth]
    new_ref[depth - 1] = row


def _history_major(state):
    return jnp.transpose(state[0], (1, 0, 2))


def _s_even_pre_kernel(x_ref, g_ref, w_ref, dww_ref, dwb_ref, lng_ref, lnb_ref, scw_ref, alog_ref, dtb_ref,
                       cin_ref, qin_ref,
                       aout_ref, q_ref, k_ref, v_ref, gb_ref, z_ref, cout_ref, qout_ref):
    h = _rms(x_ref[...], g_ref[...]).astype(BF16)
    glu = (_dot(h, w_ref[:, COL_GLU_VAL:COL_GLU_VAL + CONV_CH])
           * _sigmoid(_dot(h, w_ref[:, COL_GLU_GATE:COL_GLU_GATE + CONV_CH])))
    nb = CONV_K - 1
    acc = dww_ref[nb:nb + 1, :] * glu
    for kk in range(nb):
        acc = acc + dww_ref[kk:kk + 1, :] * cin_ref[kk]
    _push_row(cin_ref, cout_ref, glu)
    c = _silu(_layer_norm(acc + dwb_ref[...], lng_ref[...], lnb_ref[...]))
    aout_ref[...] = c * _silu(_dot(h, w_ref[:, COL_GATE_A:COL_GATE_A + CONV_CH]))

    qkv = _dot(h, w_ref[:, COL_QKV:COL_QKV + QKV_CH])
    ns = SC_K - 1
    acc = scw_ref[ns:ns + 1, :] * qkv
    for kk in range(ns):
        acc = acc + scw_ref[kk:kk + 1, :] * qin_ref[kk]
    _push_row(qin_ref, qout_ref, qkv)
    acc = _silu(acc)
    for hd in range(DN_HEADS):
        lo = hd * DN_HEAD_DIM
        q_ref[:, lo:lo + DN_HEAD_DIM] = _l2n(acc[:, lo:lo + DN_HEAD_DIM]) * (DN_HEAD_DIM ** -0.5)
        k_ref[:, lo:lo + DN_HEAD_DIM] = _l2n(acc[:, DN_WIDTH + lo:DN_WIDTH + lo + DN_HEAD_DIM])
    v_ref[...] = acc[:, 2 * DN_WIDTH:]
    z_ref[...] = _dot(h, w_ref[:, COL_Z:COL_Z + DN_WIDTH])
    gb_ref[...] = _gate_params(_dot(h, w_ref[:, COL_TAIL:COL_TAIL + LANES]), alog_ref[...], dtb_ref[...])


def _s_even_pre(x, g, w, dww, dwb, lng, lnb, scw, alog, dtb, cin, qin):
    n = x.shape[0]
    rbk = SAMPLE_ROW_BLOCK
    const = lambda shape: pl.BlockSpec(shape, lambda i: (0,) * len(shape))
    rows = lambda c: pl.BlockSpec((rbk, c), lambda i: (i, 0))
    hist = lambda arr: pl.BlockSpec((arr.shape[0], rbk, arr.shape[2]), lambda i: (0, i, 0))
    return pl.pallas_call(
        _s_even_pre_kernel,
        grid=(n // rbk,),
        in_specs=[rows(D_MODEL), const((1, D_MODEL)), const((D_MODEL, EVEN_IN_PAD)), const((CONV_K, CONV_CH)),
                  const((1, CONV_CH)), const((1, CONV_CH)), const((1, CONV_CH)), const((SC_K, QKV_CH)),
                  const((1, LANES)), const((1, LANES)), hist(cin), hist(qin)],
        out_specs=[rows(CONV_CH), rows(DN_WIDTH), rows(DN_WIDTH), rows(DN_WIDTH), rows(LANES), rows(DN_WIDTH),
                   hist(cin), hist(qin)],
        out_shape=[jax.ShapeDtypeStruct((n, CONV_CH), F32),
                   jax.ShapeDtypeStruct((n, DN_WIDTH), F32),
                   jax.ShapeDtypeStruct((n, DN_WIDTH), F32),
                   jax.ShapeDtypeStruct((n, DN_WIDTH), F32),
                   jax.ShapeDtypeStruct((n, LANES), F32),
                   jax.ShapeDtypeStruct((n, DN_WIDTH), F32),
                   jax.ShapeDtypeStruct(cin.shape, F32),
                   jax.ShapeDtypeStruct(qin.shape, F32)],
        compiler_params=_params(1),
        name="s_even_pre",
    )(x, g, w, dww, dwb, lng, lnb, scw, alog, dtb, cin, qin)


def _s_delta_kernel(q_ref, k_ref, v_ref, gb_ref, z_ref, dng_ref, sin_ref, bout_ref, sout_ref):
    dh = DN_HEAD_DIM
    chains = [(i, hd) for i in range(DELTA_ROW_BLOCK) for hd in range(DN_HEADS)]
    vec = lambda ref, i, hd: ref[i:i + 1, hd * dh:(hd + 1) * dh]
    col = lambda ref, i, hd: jnp.broadcast_to(vec(ref, i, hd), (dh, dh)).T
    kcol = [col(k_ref, i, hd) for i, hd in chains]
    qcol = [col(q_ref, i, hd) for i, hd in chains]
    s = [sin_ref[0, i, hd] * jnp.exp(gb_ref[i:i + 1, DN_HEADS + hd:DN_HEADS + hd + 1]) for i, hd in chains]
    v_new = [(vec(v_ref, i, hd) - jnp.sum(kc * sc, axis=0, keepdims=True)) * gb_ref[i:i + 1, hd:hd + 1]
             for (i, hd), kc, sc in zip(chains, kcol, s)]
    s = [sc + kc * vn for sc, kc, vn in zip(s, kcol, v_new)]
    for (i, hd), sc, qc in zip(chains, s, qcol):
        sout_ref[0, i, hd] = sc
        o = jnp.sum(qc * sc, axis=0, keepdims=True)
        bout_ref[i:i + 1, hd * dh:(hd + 1) * dh] = _rms(o, dng_ref[...]) * _silu(vec(z_ref, i, hd))


def _s_delta(q, k, v, gb, z, dng, s_in):
    n = q.shape[0]
    rbk = DELTA_ROW_BLOCK
    rows = lambda c: pl.BlockSpec((rbk, c), lambda i: (i, 0))
    st = pl.BlockSpec((1, rbk, DN_HEADS, DN_HEAD_DIM, DN_HEAD_DIM), lambda i: (0, i, 0, 0, 0))
    return pl.pallas_call(
        _s_delta_kernel,
        grid=(n // rbk,),
        in_specs=[rows(DN_WIDTH), rows(DN_WIDTH), rows(DN_WIDTH), rows(LANES), rows(DN_WIDTH),
                  pl.BlockSpec((1, DN_HEAD_DIM), lambda i: (0, 0)), st],
        out_specs=[rows(DN_WIDTH), st],
        out_shape=[jax.ShapeDtypeStruct((n, DN_WIDTH), F32),
                   jax.ShapeDtypeStruct(s_in.shape, F32)],
        compiler_params=_params(1),
        name="s_delta",
    )(q, k, v, gb, z, dng, s_in)


def _s_mix_out_kernel(a_ref, b_ref, x_ref, wout_ref, gx_ref, wq_ref, x1_ref, q_ref):
    mix = jnp.concatenate([a_ref[...], b_ref[...]], axis=1).astype(BF16)
    x1 = x_ref[...] + jnp.dot(mix, wout_ref[...], preferred_element_type=F32)
    x1_ref[...] = x1
    q_ref[...] = jnp.dot(_rms(x1, gx_ref[...]).astype(BF16), wq_ref[...], preferred_element_type=F32)


def _s_mix_out(a, bo, x, wout, gx, wq):
    n = x.shape[0]
    full = lambda arr: pl.BlockSpec(arr.shape, lambda: (0,) * arr.ndim)
    args = (a, bo, x, wout, gx, wq)
    out = jax.ShapeDtypeStruct((n, D_MODEL), F32)
    return pl.pallas_call(
        _s_mix_out_kernel,
        in_specs=[full(v) for v in args],
        out_specs=[pl.BlockSpec((n, D_MODEL), lambda: (0, 0))] * 2,
        out_shape=[out, out],
        compiler_params=pltpu.CompilerParams(vmem_limit_bytes=VMEM_LIMIT_BYTES),
        name="s_mix_out",
    )(*args)


def _split_head_dim(x):
    lead = x.shape[:-2]
    halves = XA_HEAD_DIM // LANES
    x = x.reshape(lead + (XA_HEADS, halves, LANES))
    x = jnp.swapaxes(x, -3, -2)
    return x.reshape(lead + (halves * XA_HEADS, LANES))


def _merge_head_dim(x):
    lead = x.shape[:-2]
    halves = XA_HEAD_DIM // LANES
    x = jnp.swapaxes(x.reshape(lead + (halves, XA_HEADS, LANES)), -3, -2)
    return x.reshape(lead + (XA_HEADS, XA_HEAD_DIM))


def _s_xattn_kernel(q_ref, mk_ref, mv_ref, o_ref):
    for i in range(XATTN_ROW_BLOCK):
        prod = mk_ref[0, i] * q_ref[i]
        prod = prod + pltpu.roll(prod, XA_HEADS, 1)
        s = jnp.sum(prod, axis=-1, keepdims=True) * (XA_HEAD_DIM ** -0.5)
        e = jnp.exp(s - jnp.max(s, axis=0, keepdims=True))
        o_ref[i] = jnp.sum(e * mv_ref[0, i], axis=0) / jnp.sum(e, axis=0)


def _s_xattn(q, mk, mv, layer):
    n = q.shape[0]
    rbk = XATTN_ROW_BLOCK
    sub = XA_HEADS * XA_HEAD_DIM // LANES
    mem = pl.BlockSpec((1, rbk, N_MEM, sub, LANES), lambda i: (layer, i, 0, 0, 0))
    rows = pl.BlockSpec((rbk, sub, LANES), lambda i: (i, 0, 0))
    out = pl.pallas_call(
        _s_xattn_kernel,
        grid=(n // rbk,),
        in_specs=[rows, mem, mem],
        out_specs=rows,
        out_shape=jax.ShapeDtypeStruct((n, sub, LANES), F32),
        compiler_params=_params(1),
        name="s_xattn",
    )(_split_head_dim(q.reshape(n, XA_HEADS, XA_HEAD_DIM)), _split_head_dim(mk), _split_head_dim(mv))
    return _merge_head_dim(out).reshape(n, D_MODEL)


def _s_odd_kernel(x1_ref, o_ref, wo_ref, g_ref, win_ref, wp_ref, bp_ref, sc_ref, wout_ref, gx_ref, wq_ref, pin_ref,
                  x1o_ref, q_ref, pout_ref):
    x = x1_ref[...] + _dot(o_ref[...], wo_ref[...])
    h = _rms(x, g_ref[...]).astype(BF16)
    u = jnp.dot(h, win_ref[:, 0:D_MODEL], preferred_element_type=F32)
    gate = _silu(jnp.dot(h, win_ref[:, D_MODEL:], preferred_element_type=F32))
    means = []
    for gi, win in enumerate(POOL_WINDOWS):
        lo = gi * POOL_GROUP
        acc = u[:, lo:lo + POOL_GROUP]
        for d in range(1, win):
            acc = acc + pin_ref[POOL_BUF - d, :, lo:lo + POOL_GROUP]
        means.append(acc / float(min(PAST_LEN + 1, win)))
    pooled = jnp.concatenate(means, axis=1) - u
    _push_row(pin_ref, pout_ref, u)
    z = _pool_group_linear(pooled, wp_ref, bp_ref) * sc_ref[...] * gate
    x1 = x + jnp.dot(z.astype(BF16), wout_ref[...], preferred_element_type=F32)
    x1o_ref[...] = x1
    q_ref[...] = jnp.dot(_rms(x1, gx_ref[...]).astype(BF16), wq_ref[...], preferred_element_type=F32)


def _s_odd(x1, o, wo, g, win, wp, bp, sc, wout, gx, wq, pin):
    n = x1.shape[0]
    rbk = SAMPLE_ROW_BLOCK
    ngrp = len(POOL_WINDOWS)
    const = lambda shape: pl.BlockSpec(shape, lambda i: (0,) * len(shape))
    rows = lambda c: pl.BlockSpec((rbk, c), lambda i: (i, 0))
    hist = pl.BlockSpec((pin.shape[0], rbk, pin.shape[2]), lambda i: (0, i, 0))
    out = jax.ShapeDtypeStruct((n, D_MODEL), F32)
    return pl.pallas_call(
        _s_odd_kernel,
        grid=(n // rbk,),
        in_specs=[rows(D_MODEL), rows(D_MODEL), const((D_MODEL, D_MODEL)), const((1, D_MODEL)),
                  const((D_MODEL, 2 * D_MODEL)), const((ngrp, POOL_GROUP, POOL_GROUP)), const((ngrp, POOL_GROUP)),
                  const((1, D_MODEL)), const((D_MODEL, D_MODEL)), const((1, D_MODEL)), const((D_MODEL, D_MODEL)),
                  hist],
        out_specs=[rows(D_MODEL), rows(D_MODEL), hist],
        out_shape=[out, out, jax.ShapeDtypeStruct(pin.shape, F32)],
        compiler_params=_params(1),
        name="s_odd",
    )(x1, o, wo, g, win, wp, bp, sc, wout, gx, wq, pin)


def _s_final_kernel(x1_ref, o_ref, wo_ref, gf_ref, y_ref):
    y_ref[...] = _rms(x1_ref[...] + _dot(o_ref[...], wo_ref[...]), gf_ref[...])


def _s_final(x1, o, wo, gf):
    n = x1.shape[0]
    full = lambda arr: pl.BlockSpec(arr.shape, lambda: (0,) * arr.ndim)
    args = (x1, o, wo, gf)
    return pl.pallas_call(
        _s_final_kernel,
        in_specs=[full(v) for v in args],
        out_specs=pl.BlockSpec((n, D_MODEL), lambda: (0, 0)),
        out_shape=jax.ShapeDtypeStruct((n, D_MODEL), F32),
        compiler_params=pltpu.CompilerParams(vmem_limit_bytes=VMEM_LIMIT_BYTES),
        name="s_final",
    )(*args)


def _lane_pad(vec, offset):
    return jnp.pad(vec.astype(F32), (offset, LANES - offset - vec.shape[0])).reshape(1, LANES)


def kernel(x_prompt, x_sample, state_conv_a, state_qkv_conv, state_delta, state_pool, cache_mem_k, cache_mem_v, mem_prompt, norm_mix, norm_xattn, norm_final, w_in_even, w_out_even, dw_w, dw_b, ln_a_g, ln_a_b, sc_w, a_log, dt_bias, dn_norm_g, w_in_odd, w_pool, b_pool, pool_scale, w_out_odd, w_xq, w_xk, w_xv, w_xo):
    bp, t, _ = x_prompt.shape
    ns = x_sample.shape[0]
    row = lambda v: v.reshape(1, -1)

    w_in0 = jnp.pad(w_in_even[0], ((0, 0), (0, EVEN_IN_PAD - EVEN_IN))).astype(BF16)
    w_out0 = w_out_even[0].astype(BF16)
    w_in1 = w_in_odd[0].astype(BF16)
    w_pool1 = w_pool[0].astype(BF16)
    w_out1 = w_out_odd[0].astype(BF16)
    wq = w_xq.astype(BF16)
    wk = w_xk.astype(BF16)
    wv = w_xv.astype(BF16)
    wo = w_xo.astype(BF16)
    alog = _lane_pad(a_log[0], DN_HEADS)
    dtb = _lane_pad(dt_bias[0], DN_HEADS)
    even_small = (dw_w[0], row(dw_b[0]), row(ln_a_g[0]), row(ln_a_b[0]), sc_w[0], alog, dtb)
    dng = row(dn_norm_g[0])

    mk_f, mv_f, mk_b, mv_b = _mem_kv(mem_prompt.reshape(bp * N_MEM, D_MODEL), wk, wv)
    new_mem_k_p = mk_f.reshape(DEPTH, bp, N_MEM, XA_HEADS, XA_HEAD_DIM)
    new_mem_v_p = mv_f.reshape(DEPTH, bp, N_MEM, XA_HEADS, XA_HEAD_DIM)

    even_small_p = (jnp.repeat(dw_w[0], SUBLANES, axis=0),) + even_small[1:4] + (
        jnp.repeat(sc_w[0], SUBLANES, axis=0),) + even_small[5:]
    a_out, z, gb, u, wqd, kd, qk, new_conv_a_p, new_qkv_conv_p = _even_front(x_prompt, row(norm_mix[0]), w_in0,
                                                                             *even_small_p)
    b_out, new_delta_p = _delta_scan(u, wqd, kd, qk, gb, z, dng)
    x2 = _even_post(a_out, b_out, x_prompt, w_out0, row(norm_xattn[0]), wq[0], wo[0], mk_b[0], mv_b[0])
    y_prompt, new_pool_p = _odd(x2, row(norm_mix[1]), w_in1, w_pool1, b_pool[0], row(pool_scale[0]), w_out1,
                                row(norm_xattn[1]), wq[1], wo[1], mk_b[1], mv_b[1], row(norm_final))

    xs = x_sample.reshape(ns, D_MODEL)

    sa, sq, sk, sv, sgb, sz, cout, qout = _s_even_pre(xs, row(norm_mix[0]), w_in0, *even_small,
                                                      _history_major(state_conv_a), _history_major(state_qkv_conv))
    sb, new_delta_s = _s_delta(sq, sk, sv, sgb, sz, dng, state_delta)
    x1, xq = _s_mix_out(sa, sb, xs, w_out0, row(norm_xattn[0]), wq[0])
    o0 = _s_xattn(xq, cache_mem_k, cache_mem_v, 0)
    x1, xq, pout = _s_odd(x1, o0, wo[0], row(norm_mix[1]), w_in1, w_pool1, b_pool[0], row(pool_scale[0]), w_out1,
                          row(norm_xattn[1]), wq[1], _history_major(state_pool))
    o1 = _s_xattn(xq, cache_mem_k, cache_mem_v, 1)
    y_sample = _s_final(x1, o1, wo[1], row(norm_final)).reshape(ns, 1, D_MODEL)

    new_conv_a_s, new_qkv_conv_s, new_pool_s = (jnp.transpose(st, (1, 0, 2))[None] for st in (cout, qout, pout))
    return (y_prompt, y_sample, new_conv_a_p, new_qkv_conv_p, new_delta_p, new_pool_p, new_mem_k_p,
            new_mem_v_p, new_conv_a_s, new_qkv_conv_s, new_delta_s, new_pool_s)
```

```python
import functools

import jax
import jax.numpy as jnp
from jax import lax
from jax.experimental import pallas as pl
from jax.experimental.pallas import tpu as pltpu

F32 = jnp.float32
BF16 = jnp.bfloat16

D_MODEL = 1024
DEPTH = 2
PAST_LEN = 16384
CONV_CH = 512
CONV_K = 31
DN_HEAD_DIM = 128
DN_HEADS = 4
DN_WIDTH = 512
QKV_CH = 1536
SC_K = 4
DN_CHUNK = 64
EVEN_IN = 3592
POOL_WINDOWS = (2, 4, 8, 16)
POOL_GROUP = 256
POOL_BUF = 15
N_MEM = 256
XA_HEADS = 4
XA_HEAD_DIM = 256
EPS = 1e-6

LANES = 128
SUBLANES = 8
VMEM_LIMIT_BYTES = 56 * 1024 * 1024

COL_GLU_VAL = 0
COL_GLU_GATE = 512
COL_GATE_A = 1024
COL_QKV = 1536
COL_Z = 3072
COL_TAIL = 3584
EVEN_IN_PAD = COL_TAIL + LANES

PROMPT_TILE = 256
POST_TILE = 512
POST_STREAMS = 1
CONV_HIST = 32
SC_HIST = 8
POOL_HIST = 16
CONV_ROW_BLOCK = 64
CONV_COL_BLOCK = 256
SAMPLE_ROW_BLOCK = 32
DELTA_ROW_BLOCK = 8
XATTN_ROW_BLOCK = 4


def _params(n_axes):
    return pltpu.CompilerParams(dimension_semantics=("arbitrary",) * n_axes,
                                vmem_limit_bytes=VMEM_LIMIT_BYTES)


def _dot(a, b):
    return jnp.dot(a.astype(BF16), b.astype(BF16), preferred_element_type=F32)


def _dot_nt(a, b):
    return lax.dot_general(a.astype(BF16), b.astype(BF16), (((1,), (1,)), ((), ())),
                           preferred_element_type=F32)


def _split3(x):
    x1 = x.astype(BF16)
    r1 = x - x1.astype(F32)
    x2 = r1.astype(BF16)
    x3 = (r1 - x2.astype(F32)).astype(BF16)
    return x1, x2, x3


def _split2(x):
    x1 = x.astype(BF16)
    return x1, (x - x1.astype(F32)).astype(BF16)


def _dot_hi(a, b):
    a1, a2 = _split2(a)
    b1, b2 = _split2(b)
    d = functools.partial(jnp.dot, preferred_element_type=F32)
    return d(a1, b1) + (d(a1, b2) + d(a2, b1))


def _bmm(a, b):
    return lax.dot_general(a.astype(BF16), b.astype(BF16), (((2,), (1,)), ((0,), (0,))),
                           preferred_element_type=F32)


def _bmm_nt(a, b):
    return lax.dot_general(a.astype(BF16), b.astype(BF16), (((2,), (2,)), ((0,), (0,))),
                           preferred_element_type=F32)


def _bmm_tn(a, b):
    return lax.dot_general(a.astype(BF16), b.astype(BF16), (((1,), (1,)), ((0,), (0,))),
                           preferred_element_type=F32)


def _bmm_hi(a, b):
    a1, a2 = _split2(a)
    b1, b2 = _split2(b)
    return _bmm(a1, b1) + (_bmm(a1, b2) + _bmm(a2, b1))


def _dot_exact_lhs(a_bf, b):
    b1, b2, b3 = _split3(b)
    d = functools.partial(jnp.dot, preferred_element_type=F32)
    return d(a_bf, b1) + d(a_bf, b2) + d(a_bf, b3)


def _sigmoid(x):
    return 1.0 / (1.0 + jnp.exp(-x))


def _silu(x):
    return x * _sigmoid(x)


def _softplus(x):
    return jnp.maximum(x, 0.0) + jnp.log(1.0 + jnp.exp(-jnp.abs(x)))


def _rms(x, g):
    return x * lax.rsqrt(jnp.mean(x * x, axis=-1, keepdims=True) + EPS) * g


def _layer_norm(x, g, b):
    xc = x - jnp.mean(x, axis=-1, keepdims=True)
    return xc * lax.rsqrt(jnp.mean(xc * xc, axis=-1, keepdims=True) + EPS) * g + b


def _l2n(x):
    return x * lax.rsqrt(jnp.sum(x * x, axis=-1, keepdims=True) + EPS)


def _gate_params(tail, alog, dtb):
    lane = lax.broadcasted_iota(jnp.int32, tail.shape, 1)
    beta = _sigmoid(tail)
    g = -jnp.exp(alog) * _softplus(tail + dtb)
    return jnp.where(lane < DN_HEADS, beta, g)


def _softmax_lanes(s):
    m = jnp.max(s, axis=-1, keepdims=True)
    e = jnp.exp(s - m)
    return e / jnp.sum(e, axis=-1, keepdims=True)


def _mem_kv_kernel(x_ref, wk_ref, wv_ref, k_ref, v_ref, kb_ref, vb_ref):
    x = x_ref[...]
    k = _dot(x, wk_ref[0])
    v = _dot(x, wv_ref[0])
    k_ref[0] = k
    v_ref[0] = v
    kb_ref[0] = k.astype(BF16)
    vb_ref[0] = v.astype(BF16)


def _mem_kv(mem2d, wk, wv):
    rows = mem2d.shape[0]
    tile = 512
    f32_out = jax.ShapeDtypeStruct((DEPTH, rows, D_MODEL), F32)
    bf_out = jax.ShapeDtypeStruct((DEPTH, rows, D_MODEL), BF16)
    w_spec = pl.BlockSpec((1, D_MODEL, D_MODEL), lambda l, i: (l, 0, 0))
    o_spec = pl.BlockSpec((1, tile, D_MODEL), lambda l, i: (l, i, 0))
    return pl.pallas_call(
        _mem_kv_kernel,
        grid=(DEPTH, rows // tile),
        in_specs=[pl.BlockSpec((tile, D_MODEL), lambda l, i: (i, 0)), w_spec, w_spec],
        out_specs=[o_spec, o_spec, o_spec, o_spec],
        out_shape=[f32_out, f32_out, bf_out, bf_out],
        compiler_params=_params(2),
        name="p_mem_kv",
    )(mem2d, wk, wv)


def _causal_conv(buf_ref, w_ref, n_taps, hist, r0, width):
    rb = CONV_ROW_BLOCK
    off = hist - (n_taps - 1)
    cols = []
    for c0 in range(0, width, CONV_COL_BLOCK):
        cs = slice(c0, c0 + CONV_COL_BLOCK)
        total = None
        for res in range(min(SUBLANES, n_taps)):
            base = (off + res) // SUBLANES * SUBLANES
            shift = off + res - base
            span = rb + (SUBLANES if shift else 0)
            part = None
            for kk in range(res, n_taps, SUBLANES):
                lo = r0 + base + kk - res
                rows = buf_ref[lo:lo + span, cs].reshape(span // SUBLANES, SUBLANES, CONV_COL_BLOCK)
                term = rows * w_ref[kk * SUBLANES:(kk + 1) * SUBLANES, cs][None]
                part = term if part is None else part + term
            part = part.reshape(span, CONV_COL_BLOCK)[shift:shift + rb, :]
            total = part if total is None else total + part
        cols.append(total)
    return jnp.concatenate(cols, axis=1)


def _even_front_kernel(x_ref, g_ref, w_ref, dww_ref, dwb_ref, lng_ref, lnb_ref, scw_ref, alog_ref, dtb_ref,
                       aout_ref, z_ref, gb_ref, u_ref, wq_ref, kd_ref, qk_ref, cst_ref, qst_ref,
                       cbuf, qbuf, qkv):
    tt = PROMPT_TILE
    j = pl.program_id(1)
    last = pl.num_programs(1) - 1

    @pl.when(j == 0)
    def _():
        cbuf[0:CONV_HIST, :] = jnp.zeros((CONV_HIST, CONV_CH), F32)
        qbuf[0:SC_HIST, :] = jnp.zeros((SC_HIST, QKV_CH), F32)

    @pl.when(j > 0)
    def _():
        cbuf[0:CONV_HIST, :] = cbuf[tt:tt + CONV_HIST, :]
        qbuf[0:SC_HIST, :] = qbuf[tt:tt + SC_HIST, :]

    h = _rms(x_ref[0], g_ref[...]).astype(BF16)

    qbuf[SC_HIST:SC_HIST + tt, :] = _dot(h, w_ref[:, COL_QKV:COL_QKV + QKV_CH])
    gb = _gate_params(_dot(h, w_ref[:, COL_TAIL:COL_TAIL + LANES]), alog_ref[...], dtb_ref[...])
    gb_ref[0] = gb
    for r0 in range(0, tt, CONV_ROW_BLOCK):
        rows = slice(r0, r0 + CONV_ROW_BLOCK)
        acc = _silu(_causal_conv(qbuf, scw_ref, SC_K, SC_HIST, r0, QKV_CH))
        for hd in range(DN_HEADS):
            lo = hd * DN_HEAD_DIM
            qkv[rows, lo:lo + DN_HEAD_DIM] = _l2n(acc[:, lo:lo + DN_HEAD_DIM]) * (DN_HEAD_DIM ** -0.5)
            qkv[rows, DN_WIDTH + lo:DN_WIDTH + lo + DN_HEAD_DIM] = _l2n(
                acc[:, DN_WIDTH + lo:DN_WIDTH + lo + DN_HEAD_DIM])
        qkv[rows, 2 * DN_WIDTH:] = acc[:, 2 * DN_WIDTH:]
    cbuf[CONV_HIST:CONV_HIST + tt, :] = (_dot(h, w_ref[:, COL_GLU_VAL:COL_GLU_VAL + CONV_CH])
                                         * _sigmoid(_dot(h, w_ref[:, COL_GLU_GATE:COL_GLU_GATE + CONV_CH])))
    gate_a = _silu(_dot(h, w_ref[:, COL_GATE_A:COL_GATE_A + CONV_CH]))

    def conv_rows(r0):
        acc = _causal_conv(cbuf, dww_ref, CONV_K, CONV_HIST, r0, CONV_CH)
        c = _silu(_layer_norm(acc + dwb_ref[...], lng_ref[...], lnb_ref[...]))
        aout_ref[0, r0:r0 + CONV_ROW_BLOCK, :] = (c * gate_a[r0:r0 + CONV_ROW_BLOCK, :]).astype(BF16)

    _delta_prep_tile(qkv[:, 0:DN_WIDTH], qkv[:, DN_WIDTH:2 * DN_WIDTH], qkv[:, 2 * DN_WIDTH:], gb,
                     u_ref, wq_ref, kd_ref, qk_ref,
                     fillers=[functools.partial(conv_rows, r0) for r0 in range(0, tt, CONV_ROW_BLOCK)])
    z_ref[0] = _dot(h, w_ref[:, COL_Z:COL_Z + DN_WIDTH])

    @pl.when(j == last)
    def _():
        cst_ref[0, 0] = cbuf[CONV_HIST + tt - (CONV_K - 1):CONV_HIST + tt, :]
        qst_ref[0, 0] = qbuf[SC_HIST + tt - (SC_K - 1):SC_HIST + tt, :]


def _even_front(x, g, w, dww, dwb, lng, lnb, scw, alog, dtb):
    b, t, _ = x.shape
    tt = PROMPT_TILE
    c = DN_CHUNK
    const = lambda shape: pl.BlockSpec(shape, lambda i, j: (0,) * len(shape))
    tile = lambda ch: pl.BlockSpec((1, tt, ch), lambda i, j: (i, j, 0))
    return pl.pallas_call(
        _even_front_kernel,
        grid=(b, t // tt),
        in_specs=[tile(D_MODEL), const((1, D_MODEL)), const((D_MODEL, EVEN_IN_PAD)),
                  const((CONV_K * SUBLANES, CONV_CH)), const((1, CONV_CH)), const((1, CONV_CH)), const((1, CONV_CH)),
                  const((SC_K * SUBLANES, QKV_CH)), const((1, LANES)), const((1, LANES))],
        out_specs=[tile(CONV_CH), tile(DN_WIDTH), tile(LANES), tile(DN_WIDTH),
                   pl.BlockSpec((1, tt // c, 2 * c, DN_WIDTH), lambda i, j: (i, j, 0, 0)),
                   tile(DN_WIDTH),
                   pl.BlockSpec((1, DN_HEADS, tt, c), lambda i, j: (i, 0, j, 0)),
                   pl.BlockSpec((1, 1, CONV_K - 1, CONV_CH), lambda i, j: (0, i, 0, 0)),
                   pl.BlockSpec((1, 1, SC_K - 1, QKV_CH), lambda i, j: (0, i, 0, 0))],
        out_shape=[jax.ShapeDtypeStruct((b, t, CONV_CH), BF16),
                   jax.ShapeDtypeStruct((b, t, DN_WIDTH), F32),
                   jax.ShapeDtypeStruct((b, t, LANES), F32),
                   jax.ShapeDtypeStruct((b, t, DN_WIDTH), F32),
                   jax.ShapeDtypeStruct((b, t // c, 2 * c, DN_WIDTH), BF16),
                   jax.ShapeDtypeStruct((b, t, DN_WIDTH), BF16),
                   jax.ShapeDtypeStruct((b, DN_HEADS, t, c), BF16),
                   jax.ShapeDtypeStruct((1, b, CONV_K - 1, CONV_CH), F32),
                   jax.ShapeDtypeStruct((1, b, SC_K - 1, QKV_CH), F32)],
        scratch_shapes=[pltpu.VMEM((CONV_HIST + tt, CONV_CH), F32),
                        pltpu.VMEM((SC_HIST + tt, QKV_CH), F32),
                        pltpu.VMEM((tt, QKV_CH), F32)],
        compiler_params=_params(2),
        name="p_even_front",
    )(x, g, w, dww, dwb, lng, lnb, scw, alog, dtb)


def _delta_prep_tile(q2, k2, v2, gb, u_ref, wq_ref, kd_ref, qk_ref, fillers=()):
    fillers = iter(fillers)
    tt = PROMPT_TILE
    c = DN_CHUNK
    dh = DN_HEAD_DIM
    nch = tt // c
    row = lax.broadcasted_iota(jnp.int32, (c, c), 0)
    col = lax.broadcasted_iota(jnp.int32, (c, c), 1)
    incl = (row >= col)[None]
    strict = (row > col)[None]
    eye = jnp.where(row == col, 1.0, 0.0)[None]
    trow = lax.broadcasted_iota(jnp.int32, (tt, tt), 0)
    tcol = lax.broadcasted_iota(jnp.int32, (tt, tt), 1)
    same_chunk = jnp.right_shift(trow, 6) == jnp.right_shift(tcol, 6)
    tri = jnp.where((trow >= tcol) & same_chunk, 1.0, 0.0).astype(BF16)

    def lane_rep(col0):
        return jnp.concatenate([jnp.broadcast_to(gb[:, col0 + hd:col0 + hd + 1], (tt, dh))
                                for hd in range(DN_HEADS)], axis=1)

    def chunks(x):
        return jnp.concatenate([x[:, hd * dh:(hd + 1) * dh].reshape(nch, c, dh) for hd in range(DN_HEADS)], axis=0)

    gc4 = _dot_exact_lhs(tri, lane_rep(DN_HEADS))
    gc = chunks(gc4)
    gc_rows = []
    for hd in range(DN_HEADS):
        gc_t = gc4[:, hd * dh:(hd + 1) * dh].T
        gc_rows += [gc_t[0:c, n * c:(n + 1) * c][None] for n in range(nch)]
    gc_row = jnp.concatenate(gc_rows, axis=0)
    beta = chunks(lane_rep(0))
    q = chunks(q2)
    k = chunks(k2)
    v = chunks(v2)
    egc = jnp.exp(gc)
    kb = k * beta
    decay = jnp.where(incl, jnp.exp(jnp.where(incl, gc[:, :, 0:c] - gc_row, 0.0)), 0.0)
    a = jnp.where(strict, _bmm_nt(kb, k) * decay, 0.0)
    rhs = jnp.concatenate([v * beta, kb * egc], axis=2)
    p = -a
    t_inv = eye + p
    for _ in range(5):
        p = _bmm(p, p)
        t_inv = t_inv + _bmm(p, t_inv)
        next(fillers, lambda: None)()
    x0 = _bmm(t_inv, rhs)
    rho = rhs - x0 - _bmm_hi(a, x0)
    x = x0 + _bmm(t_inv, rho)
    k_dec = k * jnp.exp(gc[:, c - 1:c, :] - gc)
    q_dec = q * egc
    qk = jnp.where(incl, _bmm_nt(q, k) * decay, 0.0)
    for hd in range(DN_HEADS):
        lo = hd * dh
        hs = slice(hd * nch, (hd + 1) * nch)
        u_ref[0, :, lo:lo + dh] = x[hs, :, 0:dh].reshape(tt, dh)
        wq_ref[0, :, 0:c, lo:lo + dh] = x[hs, :, dh:].astype(BF16)
        wq_ref[0, :, c:2 * c, lo:lo + dh] = q_dec[hs].astype(BF16)
        kd_ref[0, :, lo:lo + dh] = k_dec[hs].reshape(tt, dh).astype(BF16)
        qk_ref[0, hd] = qk[hs].reshape(tt, c).astype(BF16)
    for rest in fillers:
        rest()


def _delta_scan_kernel(u_ref, wq_ref, kd_ref, qk_ref, gb_ref, z_ref, dng_ref, bout_ref, sout_ref, s_ref):
    c = DN_CHUNK
    dh = DN_HEAD_DIM
    nb = u_ref.shape[0]
    n = pl.program_id(0)
    last = pl.num_programs(0) - 1

    @pl.when(n == 0)
    def _():
        s_ref[...] = jnp.zeros(s_ref.shape, F32)

    def heads(ref):
        return jnp.concatenate([ref[:, :, hd * dh:(hd + 1) * dh] for hd in range(DN_HEADS)], axis=0)

    g_tot = jnp.sum(gb_ref[...], axis=1, keepdims=True)
    g_last = jnp.exp(jnp.concatenate(
        [jnp.broadcast_to(g_tot[:, :, DN_HEADS + hd:DN_HEADS + hd + 1], (nb, 1, dh)) for hd in range(DN_HEADS)],
        axis=0))
    wq = jnp.concatenate([wq_ref[:, 0, :, hd * dh:(hd + 1) * dh] for hd in range(DN_HEADS)], axis=0)
    qk = jnp.concatenate([qk_ref[:, hd] for hd in range(DN_HEADS)], axis=0)
    s = s_ref[...]
    ws = _bmm(wq, s)
    v_new = heads(u_ref) - ws[:, 0:c]
    o = ws[:, c:] + _bmm(qk, v_new)
    s_ref[...] = s * g_last + _bmm_tn(heads(kd_ref), v_new)
    out = (_rms(o, dng_ref[...]) * _silu(heads(z_ref))).astype(BF16)
    for hd in range(DN_HEADS):
        bout_ref[:, :, hd * dh:(hd + 1) * dh] = out[hd * nb:(hd + 1) * nb]

    @pl.when(n == last)
    def _():
        for hd in range(DN_HEADS):
            sout_ref[0, :, hd] = s_ref[hd * nb:(hd + 1) * nb]


def _delta_scan(u, wq, kd, qk, gb, z, dng):
    b, t, _ = u.shape
    c = DN_CHUNK
    rows = lambda ch: pl.BlockSpec((b, c, ch), lambda n: (0, n, 0))
    return pl.pallas_call(
        _delta_scan_kernel,
        grid=(t // c,),
        in_specs=[rows(DN_WIDTH),
                  pl.BlockSpec((b, 1, 2 * c, DN_WIDTH), lambda n: (0, n, 0, 0)),
                  rows(DN_WIDTH),
                  pl.BlockSpec((b, DN_HEADS, c, c), lambda n: (0, 0, n, 0)),
                  rows(LANES), rows(DN_WIDTH),
                  pl.BlockSpec((1, DN_HEAD_DIM), lambda n: (0, 0))],
        out_specs=[rows(DN_WIDTH),
                   pl.BlockSpec((1, b, DN_HEADS, DN_HEAD_DIM, DN_HEAD_DIM), lambda n: (0, 0, 0, 0, 0))],
        out_shape=[jax.ShapeDtypeStruct((b, t, DN_WIDTH), BF16),
                   jax.ShapeDtypeStruct((1, b, DN_HEADS, DN_HEAD_DIM, DN_HEAD_DIM), F32)],
        scratch_shapes=[pltpu.VMEM((DN_HEADS * b, DN_HEAD_DIM, DN_HEAD_DIM), F32)],
        compiler_params=_params(1),
        name="p_delta_scan",
    )(u, wq, kd, qk, gb, z, dng)


def _row_streams(tt):
    rows = tt // POST_STREAMS
    return [slice(i * rows, (i + 1) * rows) for i in range(POST_STREAMS)]


def _split_head_dim(x):
    lead = x.shape[:-2]
    halves = XA_HEAD_DIM // LANES
    x = x.reshape(lead + (XA_HEADS, halves, LANES))
    x = jnp.swapaxes(x, -3, -2)
    return x.reshape(lead + (halves * XA_HEADS, LANES))


def _merge_head_dim(x):
    lead = x.shape[:-2]
    halves = XA_HEAD_DIM // LANES
    x = jnp.swapaxes(x.reshape(lead + (halves, XA_HEADS, LANES)), -3, -2)
    return x.reshape(lead + (XA_HEADS, XA_HEAD_DIM))


def _sample_xattn_rows(q_ref, mk_ref, mv_ref, o_ref):
    def row(i):
        prod = mk_ref[0, i] * q_ref[i]
        prod = prod + pltpu.roll(prod, XA_HEADS, 1)
        s = jnp.sum(prod, axis=-1, keepdims=True) * (XA_HEAD_DIM ** -0.5)
        e = jnp.exp(s - jnp.max(s, axis=0, keepdims=True))
        o_ref[i] = jnp.sum(e * mv_ref[0, i], axis=0) / jnp.sum(e, axis=0)
    return [functools.partial(row, i) for i in range(XATTN_ROW_BLOCK)]


def _xattn_streams(x1s, g, wq_ref, wo_ref, mk_ref, mv_ref, fillers=()):
    fillers = iter(fillers)
    qs = [jnp.dot(_rms(x1, g).astype(BF16), wq_ref[0], preferred_element_type=F32) for x1 in x1s]
    next(fillers, lambda: None)()
    outs = [[] for _ in x1s]
    for hd in range(XA_HEADS):
        lo = hd * XA_HEAD_DIM
        ss = [_dot_nt(q[:, lo:lo + XA_HEAD_DIM], mk_ref[0, :, lo:lo + XA_HEAD_DIM]) * (XA_HEAD_DIM ** -0.5)
              for q in qs]
        for out, sc in zip(outs, ss):
            out.append(_dot(_softmax_lanes(sc), mv_ref[0, :, lo:lo + XA_HEAD_DIM]).astype(BF16))
    next(fillers, lambda: None)()
    x2s = [x1 + jnp.dot(jnp.concatenate(out, axis=1), wo_ref[0], preferred_element_type=F32)
           for x1, out in zip(x1s, outs)]
    for rest in fillers:
        rest()
    return x2s


def _even_post_kernel(a_ref, b_ref, x_ref, wout_ref, gx_ref, wq_ref, wo_ref, mk_ref, mv_ref, sq_ref, smk_ref, smv_ref,
                      o_ref, so_ref):
    streams = _row_streams(POST_TILE)
    fillers = _sample_xattn_rows(sq_ref, smk_ref, smv_ref, so_ref)
    x1s = [x_ref[0, rs, :] + jnp.dot(jnp.concatenate([a_ref[0, rs, :], b_ref[0, rs, :]], axis=1), wout_ref[...],
                                     preferred_element_type=F32) for rs in streams]
    fillers.pop(0)()
    for rs, x2 in zip(streams, _xattn_streams(x1s, gx_ref[...], wq_ref, wo_ref, mk_ref, mv_ref, fillers)):
        o_ref[0, rs, :] = x2


def _layer_specs(layer, n_tiles):
    sub = XA_HEADS * XA_HEAD_DIM // LANES
    step = lambda i, j: i * n_tiles + j
    weight = pl.BlockSpec((1, D_MODEL, D_MODEL), lambda i, j: (layer, 0, 0))
    mem = pl.BlockSpec((1, N_MEM, D_MODEL), lambda i, j: (layer, i, 0))
    srows = pl.BlockSpec((XATTN_ROW_BLOCK, sub, LANES), lambda i, j: (step(i, j), 0, 0))
    scache = pl.BlockSpec((1, XATTN_ROW_BLOCK, N_MEM, sub, LANES), lambda i, j: (layer, step(i, j), 0, 0, 0))
    return weight, mem, srows, scache


def _even_post(a, bo, x, wout, gx, wq, wo, mk, mv, layer, sq, smk, smv):
    b, t, _ = x.shape
    tt = POST_TILE
    ns = sq.shape[0]
    assert ns == b * (t // tt) * XATTN_ROW_BLOCK
    const = lambda shape: pl.BlockSpec(shape, lambda i, j: (0,) * len(shape))
    tile = lambda c: pl.BlockSpec((1, tt, c), lambda i, j: (i, j, 0))
    weight, mem, srows, scache = _layer_specs(layer, t // tt)
    return pl.pallas_call(
        _even_post_kernel,
        grid=(b, t // tt),
        in_specs=[tile(CONV_CH), tile(DN_WIDTH), tile(D_MODEL), const((D_MODEL, D_MODEL)), const((1, D_MODEL)),
                  weight, weight, mem, mem, srows, scache, scache],
        out_specs=[tile(D_MODEL), srows],
        out_shape=[jax.ShapeDtypeStruct((b, t, D_MODEL), F32), jax.ShapeDtypeStruct(sq.shape, F32)],
        compiler_params=_params(2),
        name="p_even_post",
    )(a, bo, x, wout, gx, wq, wo, mk, mv, sq, smk, smv)


def _pool_group_linear(pooled, wp_ref, bp_ref):
    outs = []
    for gi in range(len(POOL_WINDOWS)):
        lo = gi * POOL_GROUP
        outs.append(_dot(pooled[:, lo:lo + POOL_GROUP], wp_ref[gi]) + bp_ref[gi:gi + 1, :])
    return jnp.concatenate(outs, axis=1)


def _odd_kernel(x_ref, g_ref, win_ref, wp_ref, bp_ref, sc_ref, wout_ref, gx_ref, wq_ref, wo_ref, mk_ref, mv_ref,
                gf_ref, sq_ref, smk_ref, smv_ref, y_ref, pst_ref, so_ref, pbuf):
    tt = POST_TILE
    j = pl.program_id(1)
    last = pl.num_programs(1) - 1
    streams = _row_streams(tt)
    fillers = _sample_xattn_rows(sq_ref, smk_ref, smv_ref, so_ref)

    @pl.when(j == 0)
    def _():
        pbuf[0:POOL_HIST, :] = jnp.zeros((POOL_HIST, D_MODEL), F32)

    @pl.when(j > 0)
    def _():
        pbuf[0:POOL_HIST, :] = pbuf[tt:tt + POOL_HIST, :]

    xs = [x_ref[0, rs, :] for rs in streams]
    us, gates = [], []
    for x, rs in zip(xs, streams):
        h = _rms(x, g_ref[...]).astype(BF16)
        u = jnp.dot(h, win_ref[:, 0:D_MODEL], preferred_element_type=F32)
        pbuf[POOL_HIST + rs.start:POOL_HIST + rs.stop, :] = u
        us.append(u)
        gates.append(_silu(jnp.dot(h, win_ref[:, D_MODEL:], preferred_element_type=F32)))
    fillers.pop(0)()

    x1s = []
    for x, u, gate, rs in zip(xs, us, gates, streams):
        pos = j * tt + rs.start + lax.broadcasted_iota(jnp.int32, (rs.stop - rs.start, 1), 0)
        acc = pbuf[rs.start:POOL_HIST + rs.stop, :]
        means = []
        for gi, win in enumerate(POOL_WINDOWS):
            acc = acc + pltpu.roll(acc, win // 2, 0)
            cnt = jnp.minimum(pos + 1, win).astype(F32)
            means.append(acc[POOL_HIST:, 0:POOL_GROUP] / cnt)
            acc = acc[:, POOL_GROUP:]
        pooled = jnp.concatenate(means, axis=1) - u
        z = _pool_group_linear(pooled, wp_ref, bp_ref) * sc_ref[...] * gate
        x1s.append(x + jnp.dot(z.astype(BF16), wout_ref[...], preferred_element_type=F32))
    for rs, x2 in zip(streams, _xattn_streams(x1s, gx_ref[...], wq_ref, wo_ref, mk_ref, mv_ref, fillers)):
        y_ref[0, rs, :] = _rms(x2, gf_ref[...])

    @pl.when(j == last)
    def _():
        pst_ref[0, 0] = pbuf[POOL_HIST + tt - POOL_BUF:POOL_HIST + tt, :]


def _odd(x, g, win, wp, bp, sc, wout, gx, wq, wo, mk, mv, gf, layer, sq, smk, smv):
    b, t, _ = x.shape
    tt = POST_TILE
    ns = sq.shape[0]
    assert ns == b * (t // tt) * XATTN_ROW_BLOCK
    const = lambda shape: pl.BlockSpec(shape, lambda i, j: (0,) * len(shape))
    tile = lambda c: pl.BlockSpec((1, tt, c), lambda i, j: (i, j, 0))
    weight, mem, srows, scache = _layer_specs(layer, t // tt)
    ngrp = len(POOL_WINDOWS)
    return pl.pallas_call(
        _odd_kernel,
        grid=(b, t // tt),
        in_specs=[tile(D_MODEL), const((1, D_MODEL)), const((D_MODEL, 2 * D_MODEL)),
                  const((ngrp, POOL_GROUP, POOL_GROUP)), const((ngrp, POOL_GROUP)), const((1, D_MODEL)),
                  const((D_MODEL, D_MODEL)), const((1, D_MODEL)), weight, weight, mem, mem, const((1, D_MODEL)),
                  srows, scache, scache],
        out_specs=[tile(D_MODEL), pl.BlockSpec((1, 1, POOL_BUF, D_MODEL), lambda i, j: (0, i, 0, 0)), srows],
        out_shape=[jax.ShapeDtypeStruct((b, t, D_MODEL), F32),
                   jax.ShapeDtypeStruct((1, b, POOL_BUF, D_MODEL), F32),
                   jax.ShapeDtypeStruct(sq.shape, F32)],
        scratch_shapes=[pltpu.VMEM((POOL_HIST + tt, D_MODEL), F32)],
        compiler_params=_params(2),
        name="p_odd_layer",
    )(x, g, win, wp, bp, sc, wout, gx, wq, wo, mk, mv, gf, sq, smk, smv)


def _push_row(old_ref, new_ref, row):
    depth = old_ref.shape[0]
    new_ref[0:depth - 1] = old_ref[1:depth]
    new_ref[depth - 1] = row


def _history_major(state):
    return jnp.transpose(state[0], (1, 0, 2))


def _s_even_pre_kernel(x_ref, g_ref, w_ref, dww_ref, dwb_ref, lng_ref, lnb_ref, scw_ref, alog_ref, dtb_ref,
                       cin_ref, qin_ref,
                       aout_ref, q_ref, k_ref, v_ref, gb_ref, z_ref, cout_ref, qout_ref):
    h = _rms(x_ref[...], g_ref[...]).astype(BF16)
    glu = (_dot(h, w_ref[:, COL_GLU_VAL:COL_GLU_VAL + CONV_CH])
           * _sigmoid(_dot(h, w_ref[:, COL_GLU_GATE:COL_GLU_GATE + CONV_CH])))
    nb = CONV_K - 1
    acc = dww_ref[nb:nb + 1, :] * glu
    for kk in range(nb):
        acc = acc + dww_ref[kk:kk + 1, :] * cin_ref[kk]
    _push_row(cin_ref, cout_ref, glu)
    c = _silu(_layer_norm(acc + dwb_ref[...], lng_ref[...], lnb_ref[...]))
    aout_ref[...] = c * _silu(_dot(h, w_ref[:, COL_GATE_A:COL_GATE_A + CONV_CH]))

    qkv = _dot(h, w_ref[:, COL_QKV:COL_QKV + QKV_CH])
    ns = SC_K - 1
    acc = scw_ref[ns:ns + 1, :] * qkv
    for kk in range(ns):
        acc = acc + scw_ref[kk:kk + 1, :] * qin_ref[kk]
    _push_row(qin_ref, qout_ref, qkv)
    acc = _silu(acc)
    for hd in range(DN_HEADS):
        lo = hd * DN_HEAD_DIM
        q_ref[:, lo:lo + DN_HEAD_DIM] = _l2n(acc[:, lo:lo + DN_HEAD_DIM]) * (DN_HEAD_DIM ** -0.5)
        k_ref[:, lo:lo + DN_HEAD_DIM] = _l2n(acc[:, DN_WIDTH + lo:DN_WIDTH + lo + DN_HEAD_DIM])
    v_ref[...] = acc[:, 2 * DN_WIDTH:]
    z_ref[...] = _dot(h, w_ref[:, COL_Z:COL_Z + DN_WIDTH])
    gb_ref[...] = _gate_params(_dot(h, w_ref[:, COL_TAIL:COL_TAIL + LANES]), alog_ref[...], dtb_ref[...])


def _s_even_pre(x, g, w, dww, dwb, lng, lnb, scw, alog, dtb, cin, qin):
    n = x.shape[0]
    rbk = SAMPLE_ROW_BLOCK
    const = lambda shape: pl.BlockSpec(shape, lambda i: (0,) * len(shape))
    rows = lambda c: pl.BlockSpec((rbk, c), lambda i: (i, 0))
    hist = lambda arr: pl.BlockSpec((arr.shape[0], rbk, arr.shape[2]), lambda i: (0, i, 0))
    return pl.pallas_call(
        _s_even_pre_kernel,
        grid=(n // rbk,),
        in_specs=[rows(D_MODEL), const((1, D_MODEL)), const((D_MODEL, EVEN_IN_PAD)), const((CONV_K, CONV_CH)),
                  const((1, CONV_CH)), const((1, CONV_CH)), const((1, CONV_CH)), const((SC_K, QKV_CH)),
                  const((1, LANES)), const((1, LANES)), hist(cin), hist(qin)],
        out_specs=[rows(CONV_CH), rows(DN_WIDTH), rows(DN_WIDTH), rows(DN_WIDTH), rows(LANES), rows(DN_WIDTH),
                   hist(cin), hist(qin)],
        out_shape=[jax.ShapeDtypeStruct((n, CONV_CH), F32),
                   jax.ShapeDtypeStruct((n, DN_WIDTH), F32),
                   jax.ShapeDtypeStruct((n, DN_WIDTH), F32),
                   jax.ShapeDtypeStruct((n, DN_WIDTH), F32),
                   jax.ShapeDtypeStruct((n, LANES), F32),
                   jax.ShapeDtypeStruct((n, DN_WIDTH), F32),
                   jax.ShapeDtypeStruct(cin.shape, F32),
                   jax.ShapeDtypeStruct(qin.shape, F32)],
        compiler_params=_params(1),
        name="s_even_pre",
    )(x, g, w, dww, dwb, lng, lnb, scw, alog, dtb, cin, qin)


def _s_delta_kernel(q_ref, k_ref, v_ref, gb_ref, z_ref, dng_ref, sin_ref, bout_ref, sout_ref):
    dh = DN_HEAD_DIM
    chains = [(i, hd) for i in range(DELTA_ROW_BLOCK) for hd in range(DN_HEADS)]
    vec = lambda ref, i, hd: ref[i:i + 1, hd * dh:(hd + 1) * dh]
    col = lambda ref, i, hd: jnp.broadcast_to(vec(ref, i, hd), (dh, dh)).T
    kcol = [col(k_ref, i, hd) for i, hd in chains]
    qcol = [col(q_ref, i, hd) for i, hd in chains]
    s = [sin_ref[0, i, hd] * jnp.exp(gb_ref[i:i + 1, DN_HEADS + hd:DN_HEADS + hd + 1]) for i, hd in chains]
    v_new = [(vec(v_ref, i, hd) - jnp.sum(kc * sc, axis=0, keepdims=True)) * gb_ref[i:i + 1, hd:hd + 1]
             for (i, hd), kc, sc in zip(chains, kcol, s)]
    s = [sc + kc * vn for sc, kc, vn in zip(s, kcol, v_new)]
    for (i, hd), sc, qc in zip(chains, s, qcol):
        sout_ref[0, i, hd] = sc
        o = jnp.sum(qc * sc, axis=0, keepdims=True)
        bout_ref[i:i + 1, hd * dh:(hd + 1) * dh] = _rms(o, dng_ref[...]) * _silu(vec(z_ref, i, hd))


def _s_delta(q, k, v, gb, z, dng, s_in):
    n = q.shape[0]
    rbk = DELTA_ROW_BLOCK
    rows = lambda c: pl.BlockSpec((rbk, c), lambda i: (i, 0))
    st = pl.BlockSpec((1, rbk, DN_HEADS, DN_HEAD_DIM, DN_HEAD_DIM), lambda i: (0, i, 0, 0, 0))
    return pl.pallas_call(
        _s_delta_kernel,
        grid=(n // rbk,),
        in_specs=[rows(DN_WIDTH), rows(DN_WIDTH), rows(DN_WIDTH), rows(LANES), rows(DN_WIDTH),
                  pl.BlockSpec((1, DN_HEAD_DIM), lambda i: (0, 0)), st],
        out_specs=[rows(DN_WIDTH), st],
        out_shape=[jax.ShapeDtypeStruct((n, DN_WIDTH), F32),
                   jax.ShapeDtypeStruct(s_in.shape, F32)],
        compiler_params=_params(1),
        name="s_delta",
    )(q, k, v, gb, z, dng, s_in)


def _s_mix_out_kernel(a_ref, b_ref, x_ref, wout_ref, gx_ref, wq_ref, x1_ref, q_ref):
    mix = jnp.concatenate([a_ref[...], b_ref[...]], axis=1).astype(BF16)
    x1 = x_ref[...] + jnp.dot(mix, wout_ref[...], preferred_element_type=F32)
    x1_ref[...] = x1
    q_ref[...] = jnp.dot(_rms(x1, gx_ref[...]).astype(BF16), wq_ref[0], preferred_element_type=F32)


def _s_mix_out(a, bo, x, wout, gx, wq, layer):
    n = x.shape[0]
    full = lambda arr: pl.BlockSpec(arr.shape, lambda i: (0,) * arr.ndim)
    args = (a, bo, x, wout, gx, wq)
    out = jax.ShapeDtypeStruct((n, D_MODEL), F32)
    return pl.pallas_call(
        _s_mix_out_kernel,
        grid=(1,),
        in_specs=[full(v) for v in args[:-1]] + [pl.BlockSpec((1, D_MODEL, D_MODEL), lambda i: (layer, 0, 0))],
        out_specs=[pl.BlockSpec((n, D_MODEL), lambda i: (0, 0))] * 2,
        out_shape=[out, out],
        compiler_params=_params(1),
        name="s_mix_out",
    )(*args)


def _s_odd_kernel(x1_ref, o_ref, wo_ref, g_ref, win_ref, wp_ref, bp_ref, sc_ref, wout_ref, gx_ref, wq_ref, pin_ref,
                  x1o_ref, q_ref, pout_ref):
    x = x1_ref[...] + _dot(o_ref[...], wo_ref[0])
    h = _rms(x, g_ref[...]).astype(BF16)
    u = jnp.dot(h, win_ref[:, 0:D_MODEL], preferred_element_type=F32)
    gate = _silu(jnp.dot(h, win_ref[:, D_MODEL:], preferred_element_type=F32))
    means = []
    for gi, win in enumerate(POOL_WINDOWS):
        lo = gi * POOL_GROUP
        acc = u[:, lo:lo + POOL_GROUP]
        for d in range(1, win):
            acc = acc + pin_ref[POOL_BUF - d, :, lo:lo + POOL_GROUP]
        means.append(acc / float(min(PAST_LEN + 1, win)))
    pooled = jnp.concatenate(means, axis=1) - u
    _push_row(pin_ref, pout_ref, u)
    z = _pool_group_linear(pooled, wp_ref, bp_ref) * sc_ref[...] * gate
    x1 = x + jnp.dot(z.astype(BF16), wout_ref[...], preferred_element_type=F32)
    x1o_ref[...] = x1
    q_ref[...] = jnp.dot(_rms(x1, gx_ref[...]).astype(BF16), wq_ref[0], preferred_element_type=F32)


def _s_odd(x1, o, wo, g, win, wp, bp, sc, wout, gx, wq, pin, layer):
    n = x1.shape[0]
    rbk = SAMPLE_ROW_BLOCK
    ngrp = len(POOL_WINDOWS)
    const = lambda shape: pl.BlockSpec(shape, lambda i: (0,) * len(shape))
    rows = lambda c: pl.BlockSpec((rbk, c), lambda i: (i, 0))
    hist = pl.BlockSpec((pin.shape[0], rbk, pin.shape[2]), lambda i: (0, i, 0))
    out = jax.ShapeDtypeStruct((n, D_MODEL), F32)
    return pl.pallas_call(
        _s_odd_kernel,
        grid=(n // rbk,),
        in_specs=[rows(D_MODEL), rows(D_MODEL), pl.BlockSpec((1, D_MODEL, D_MODEL), lambda i: (layer - 1, 0, 0)),
                  const((1, D_MODEL)),
                  const((D_MODEL, 2 * D_MODEL)), const((ngrp, POOL_GROUP, POOL_GROUP)), const((ngrp, POOL_GROUP)),
                  const((1, D_MODEL)), const((D_MODEL, D_MODEL)), const((1, D_MODEL)),
                  pl.BlockSpec((1, D_MODEL, D_MODEL), lambda i: (layer, 0, 0)),
                  hist],
        out_specs=[rows(D_MODEL), rows(D_MODEL), hist],
        out_shape=[out, out, jax.ShapeDtypeStruct(pin.shape, F32)],
        compiler_params=_params(1),
        name="s_odd",
    )(x1, o, wo, g, win, wp, bp, sc, wout, gx, wq, pin)


def _s_final_kernel(x1_ref, o_ref, wo_ref, gf_ref, y_ref):
    y_ref[...] = _rms(x1_ref[...] + _dot(o_ref[...], wo_ref[0]), gf_ref[...])


def _s_final(x1, o, wo, gf, layer):
    n = x1.shape[0]
    full = lambda arr: pl.BlockSpec(arr.shape, lambda i: (0,) * arr.ndim)
    args = (x1, o, wo, gf)
    return pl.pallas_call(
        _s_final_kernel,
        grid=(1,),
        in_specs=[full(x1), full(o), pl.BlockSpec((1, D_MODEL, D_MODEL), lambda i: (layer, 0, 0)), full(gf)],
        out_specs=pl.BlockSpec((n, D_MODEL), lambda i: (0, 0)),
        out_shape=jax.ShapeDtypeStruct((n, D_MODEL), F32),
        compiler_params=_params(1),
        name="s_final",
    )(*args)


def _lane_pad(vec, offset):
    return jnp.pad(vec.astype(F32), (offset, LANES - offset - vec.shape[0])).reshape(1, LANES)


def kernel(x_prompt, x_sample, state_conv_a, state_qkv_conv, state_delta, state_pool, cache_mem_k, cache_mem_v, mem_prompt, norm_mix, norm_xattn, norm_final, w_in_even, w_out_even, dw_w, dw_b, ln_a_g, ln_a_b, sc_w, a_log, dt_bias, dn_norm_g, w_in_odd, w_pool, b_pool, pool_scale, w_out_odd, w_xq, w_xk, w_xv, w_xo):
    bp, t, _ = x_prompt.shape
    ns = x_sample.shape[0]
    row = lambda v: v.reshape(1, -1)

    w_in0 = jnp.pad(w_in_even[0], ((0, 0), (0, EVEN_IN_PAD - EVEN_IN))).astype(BF16)
    w_out0 = w_out_even[0].astype(BF16)
    w_in1 = w_in_odd[0].astype(BF16)
    w_pool1 = w_pool[0].astype(BF16)
    w_out1 = w_out_odd[0].astype(BF16)
    wq = w_xq.astype(BF16)
    wk = w_xk.astype(BF16)
    wv = w_xv.astype(BF16)
    wo = w_xo.astype(BF16)
    alog = _lane_pad(a_log[0], DN_HEADS)
    dtb = _lane_pad(dt_bias[0], DN_HEADS)
    even_small = (dw_w[0], row(dw_b[0]), row(ln_a_g[0]), row(ln_a_b[0]), sc_w[0], alog, dtb)
    dng = row(dn_norm_g[0])

    mk_f, mv_f, mk_b, mv_b = _mem_kv(mem_prompt.reshape(bp * N_MEM, D_MODEL), wk, wv)
    new_mem_k_p = mk_f.reshape(DEPTH, bp, N_MEM, XA_HEADS, XA_HEAD_DIM)
    new_mem_v_p = mv_f.reshape(DEPTH, bp, N_MEM, XA_HEADS, XA_HEAD_DIM)

    even_small_p = (jnp.repeat(dw_w[0], SUBLANES, axis=0),) + even_small[1:4] + (
        jnp.repeat(sc_w[0], SUBLANES, axis=0),) + even_small[5:]
    a_out, z, gb, u, wqd, kd, qk, new_conv_a_p, new_qkv_conv_p = _even_front(x_prompt, row(norm_mix[0]), w_in0,
                                                                             *even_small_p)
    b_out, new_delta_p = _delta_scan(u, wqd, kd, qk, gb, z, dng)

    xs = x_sample.reshape(ns, D_MODEL)
    sa, sq, sk, sv, sgb, sz, cout, qout = _s_even_pre(xs, row(norm_mix[0]), w_in0, *even_small,
                                                      _history_major(state_conv_a), _history_major(state_qkv_conv))
    sb, new_delta_s = _s_delta(sq, sk, sv, sgb, sz, dng, state_delta)
    x1, xq = _s_mix_out(sa, sb, xs, w_out0, row(norm_xattn[0]), wq, 0)

    heads = lambda v: _split_head_dim(v.reshape(ns, XA_HEADS, XA_HEAD_DIM))
    unheads = lambda v: _merge_head_dim(v).reshape(ns, D_MODEL)
    cmk, cmv = _split_head_dim(cache_mem_k), _split_head_dim(cache_mem_v)
    x2, o0 = _even_post(a_out, b_out, x_prompt, w_out0, row(norm_xattn[0]), wq, wo, mk_b, mv_b, 0, heads(xq), cmk, cmv)
    x1, xq, pout = _s_odd(x1, unheads(o0), wo, row(norm_mix[1]), w_in1, w_pool1, b_pool[0], row(pool_scale[0]), w_out1,
                          row(norm_xattn[1]), wq, _history_major(state_pool), 1)
    y_prompt, new_pool_p, o1 = _odd(x2, row(norm_mix[1]), w_in1, w_pool1, b_pool[0], row(pool_scale[0]), w_out1,
                                    row(norm_xattn[1]), wq, wo, mk_b, mv_b, row(norm_final), 1, heads(xq), cmk, cmv)
    y_sample = _s_final(x1, unheads(o1), wo, row(norm_final), 1).reshape(ns, 1, D_MODEL)

    new_conv_a_s, new_qkv_conv_s, new_pool_s = (jnp.transpose(st, (1, 0, 2))[None] for st in (cout, qout, pout))
    return (y_prompt, y_sample, new_conv_a_p, new_qkv_conv_p, new_delta_p, new_pool_p, new_mem_k_p,
            new_mem_v_p, new_conv_a_s, new_qkv_conv_s, new_delta_s, new_pool_s)
```

```python
import functools

import jax
import jax.numpy as jnp
from jax import lax
from jax.experimental import pallas as pl
from jax.experimental.pallas import tpu as pltpu

F32 = jnp.float32
BF16 = jnp.bfloat16

D_MODEL = 1024
DEPTH = 2
PAST_LEN = 16384
CONV_CH = 512
CONV_K = 31
DN_HEAD_DIM = 128
DN_HEADS = 4
DN_WIDTH = 512
QKV_CH = 1536
SC_K = 4
DN_CHUNK = 64
EVEN_IN = 3592
POOL_WINDOWS = (2, 4, 8, 16)
POOL_GROUP = 256
POOL_BUF = 15
N_MEM = 256
XA_HEADS = 4
XA_HEAD_DIM = 256
EPS = 1e-6

LANES = 128
SUBLANES = 8
VMEM_LIMIT_BYTES = 56 * 1024 * 1024

COL_GLU_VAL = 0
COL_GLU_GATE = 512
COL_GATE_A = 1024
COL_QKV = 1536
COL_Z = 3072
COL_TAIL = 3584

PROMPT_TILE = 256
POST_TILE = 512
POST_STREAMS = 1
CONV_HIST = 32
SC_HIST = 8
POOL_HIST = 16
CONV_ROW_BLOCK = 64
CONV_COL_BLOCK = 256
SAMPLE_ROW_BLOCK = 32
DELTA_ROW_BLOCK = 8
XATTN_ROW_BLOCK = 4


def _params(n_axes):
    return pltpu.CompilerParams(dimension_semantics=("arbitrary",) * n_axes,
                                vmem_limit_bytes=VMEM_LIMIT_BYTES)


def _dot(a, b):
    return jnp.dot(a.astype(BF16), b.astype(BF16), preferred_element_type=F32)


def _dot_nt(a, b):
    return lax.dot_general(a.astype(BF16), b.astype(BF16), (((1,), (1,)), ((), ())),
                           preferred_element_type=F32)


def _split3(x):
    x1 = x.astype(BF16)
    r1 = x - x1.astype(F32)
    x2 = r1.astype(BF16)
    x3 = (r1 - x2.astype(F32)).astype(BF16)
    return x1, x2, x3


def _split2(x):
    x1 = x.astype(BF16)
    return x1, (x - x1.astype(F32)).astype(BF16)


def _dot_hi(a, b):
    a1, a2 = _split2(a)
    b1, b2 = _split2(b)
    d = functools.partial(jnp.dot, preferred_element_type=F32)
    return d(a1, b1) + (d(a1, b2) + d(a2, b1))


def _bmm(a, b):
    return lax.dot_general(a.astype(BF16), b.astype(BF16), (((2,), (1,)), ((0,), (0,))),
                           preferred_element_type=F32)


def _bmm_nt(a, b):
    return lax.dot_general(a.astype(BF16), b.astype(BF16), (((2,), (2,)), ((0,), (0,))),
                           preferred_element_type=F32)


def _bmm_tn(a, b):
    return lax.dot_general(a.astype(BF16), b.astype(BF16), (((1,), (1,)), ((0,), (0,))),
                           preferred_element_type=F32)


def _bmm_hi(a, b):
    a1, a2 = _split2(a)
    b1, b2 = _split2(b)
    return _bmm(a1, b1) + (_bmm(a1, b2) + _bmm(a2, b1))


def _dot_exact_lhs(a_bf, b):
    b1, b2, b3 = _split3(b)
    d = functools.partial(jnp.dot, preferred_element_type=F32)
    return d(a_bf, b1) + d(a_bf, b2) + d(a_bf, b3)


def _sigmoid(x):
    return 1.0 / (1.0 + jnp.exp(-x))


def _silu(x):
    return x * _sigmoid(x)


def _softplus(x):
    return jnp.maximum(x, 0.0) + jnp.log(1.0 + jnp.exp(-jnp.abs(x)))


def _rms(x, g):
    return x * lax.rsqrt(jnp.mean(x * x, axis=-1, keepdims=True) + EPS) * g


def _layer_norm(x, g, b):
    xc = x - jnp.mean(x, axis=-1, keepdims=True)
    return xc * lax.rsqrt(jnp.mean(xc * xc, axis=-1, keepdims=True) + EPS) * g + b


def _l2n(x):
    return x * lax.rsqrt(jnp.sum(x * x, axis=-1, keepdims=True) + EPS)


def _gate_params(tail, alog, dtb):
    lane = lax.broadcasted_iota(jnp.int32, tail.shape, 1)
    beta = _sigmoid(tail)
    g = -jnp.exp(alog) * _softplus(tail + dtb)
    return jnp.where(lane < DN_HEADS, beta, g)


def _softmax_lanes(s):
    m = jnp.max(s, axis=-1, keepdims=True)
    e = jnp.exp(s - m)
    return e / jnp.sum(e, axis=-1, keepdims=True)


def _mem_kv_kernel(x_ref, wk_ref, wv_ref, k_ref, v_ref, kb_ref, vb_ref):
    x = x_ref[...]
    k = _dot(x, wk_ref[0])
    v = _dot(x, wv_ref[0])
    halves = XA_HEAD_DIM // LANES
    for hd in range(XA_HEADS):
        for half in range(halves):
            lo = hd * XA_HEAD_DIM + half * LANES
            k_ref[0, :, half * XA_HEADS + hd, :] = k[:, lo:lo + LANES]
            v_ref[0, :, half * XA_HEADS + hd, :] = v[:, lo:lo + LANES]
    kb_ref[0] = k.astype(BF16)
    vb_ref[0] = v.astype(BF16)


def _mem_kv(mem2d, wk, wv):
    rows = mem2d.shape[0]
    tile = 512
    sub = XA_HEADS * XA_HEAD_DIM // LANES
    f32_out = jax.ShapeDtypeStruct((DEPTH, rows, sub, LANES), F32)
    bf_out = jax.ShapeDtypeStruct((DEPTH, rows, D_MODEL), BF16)
    w_spec = pl.BlockSpec((1, D_MODEL, D_MODEL), lambda l, i: (l, 0, 0))
    o_spec = pl.BlockSpec((1, tile, D_MODEL), lambda l, i: (l, i, 0))
    f_spec = pl.BlockSpec((1, tile, sub, LANES), lambda l, i: (l, i, 0, 0))
    return pl.pallas_call(
        _mem_kv_kernel,
        grid=(DEPTH, rows // tile),
        in_specs=[pl.BlockSpec((tile, D_MODEL), lambda l, i: (i, 0)), w_spec, w_spec],
        out_specs=[f_spec, f_spec, o_spec, o_spec],
        out_shape=[f32_out, f32_out, bf_out, bf_out],
        compiler_params=_params(2),
        name="p_mem_kv",
    )(mem2d, wk, wv)


def _causal_conv(buf_ref, w_ref, n_taps, hist, r0, width):
    rb = CONV_ROW_BLOCK
    off = hist - (n_taps - 1)
    cols = []
    for c0 in range(0, width, CONV_COL_BLOCK):
        cs = slice(c0, c0 + CONV_COL_BLOCK)
        total = None
        for res in range(min(SUBLANES, n_taps)):
            base = (off + res) // SUBLANES * SUBLANES
            shift = off + res - base
            span = rb + (SUBLANES if shift else 0)
            part = None
            for kk in range(res, n_taps, SUBLANES):
                lo = r0 + base + kk - res
                rows = buf_ref[lo:lo + span, cs].reshape(span // SUBLANES, SUBLANES, CONV_COL_BLOCK)
                term = rows * w_ref[kk * SUBLANES:(kk + 1) * SUBLANES, cs][None]
                part = term if part is None else part + term
            part = part.reshape(span, CONV_COL_BLOCK)[shift:shift + rb, :]
            total = part if total is None else total + part
        cols.append(total)
    return jnp.concatenate(cols, axis=1)


def _even_front_kernel(x_ref, g_ref, w_ref, wt_ref, dww_ref, dwb_ref, lng_ref, lnb_ref, scw_ref, alog_ref, dtb_ref,
                       aout_ref, z_ref, gb_ref, u_ref, wq_ref, kd_ref, qk_ref, cst_ref, qst_ref,
                       cbuf, qbuf, qkv):
    tt = PROMPT_TILE
    j = pl.program_id(1)
    last = pl.num_programs(1) - 1

    @pl.when(j == 0)
    def _():
        cbuf[0:CONV_HIST, :] = jnp.zeros((CONV_HIST, CONV_CH), F32)
        qbuf[0:SC_HIST, :] = jnp.zeros((SC_HIST, QKV_CH), F32)

    @pl.when(j > 0)
    def _():
        cbuf[0:CONV_HIST, :] = cbuf[tt:tt + CONV_HIST, :]
        qbuf[0:SC_HIST, :] = qbuf[tt:tt + SC_HIST, :]

    h = _rms(x_ref[0], g_ref[...]).astype(BF16)

    qbuf[SC_HIST:SC_HIST + tt, :] = _dot(h, w_ref[:, COL_QKV:COL_QKV + QKV_CH])
    gb = _gate_params(_dot(h, wt_ref[...]), alog_ref[...], dtb_ref[...])
    gb_ref[0] = gb
    for r0 in range(0, tt, CONV_ROW_BLOCK):
        rows = slice(r0, r0 + CONV_ROW_BLOCK)
        acc = _silu(_causal_conv(qbuf, scw_ref, SC_K, SC_HIST, r0, QKV_CH))
        for hd in range(DN_HEADS):
            lo = hd * DN_HEAD_DIM
            qkv[rows, lo:lo + DN_HEAD_DIM] = _l2n(acc[:, lo:lo + DN_HEAD_DIM]) * (DN_HEAD_DIM ** -0.5)
            qkv[rows, DN_WIDTH + lo:DN_WIDTH + lo + DN_HEAD_DIM] = _l2n(
                acc[:, DN_WIDTH + lo:DN_WIDTH + lo + DN_HEAD_DIM])
        qkv[rows, 2 * DN_WIDTH:] = acc[:, 2 * DN_WIDTH:]
    cbuf[CONV_HIST:CONV_HIST + tt, :] = (_dot(h, w_ref[:, COL_GLU_VAL:COL_GLU_VAL + CONV_CH])
                                         * _sigmoid(_dot(h, w_ref[:, COL_GLU_GATE:COL_GLU_GATE + CONV_CH])))
    gate_a = _silu(_dot(h, w_ref[:, COL_GATE_A:COL_GATE_A + CONV_CH]))

    def conv_rows(r0):
        acc = _causal_conv(cbuf, dww_ref, CONV_K, CONV_HIST, r0, CONV_CH)
        c = _silu(_layer_norm(acc + dwb_ref[...], lng_ref[...], lnb_ref[...]))
        aout_ref[0, r0:r0 + CONV_ROW_BLOCK, :] = (c * gate_a[r0:r0 + CONV_ROW_BLOCK, :]).astype(BF16)

    _delta_prep_tile(qkv[:, 0:DN_WIDTH], qkv[:, DN_WIDTH:2 * DN_WIDTH], qkv[:, 2 * DN_WIDTH:], gb,
                     u_ref, wq_ref, kd_ref, qk_ref,
                     fillers=[functools.partial(conv_rows, r0) for r0 in range(0, tt, CONV_ROW_BLOCK)])
    z_ref[0] = _dot(h, w_ref[:, COL_Z:COL_Z + DN_WIDTH])

    @pl.when(j == last)
    def _():
        cst_ref[0, 0] = cbuf[CONV_HIST + tt - (CONV_K - 1):CONV_HIST + tt, :]
        qst_ref[0, 0] = qbuf[SC_HIST + tt - (SC_K - 1):SC_HIST + tt, :]


def _even_front(x, g, w, wt, dww, dwb, lng, lnb, scw, alog, dtb):
    b, t, _ = x.shape
    tt = PROMPT_TILE
    c = DN_CHUNK
    const = lambda shape: pl.BlockSpec(shape, lambda i, j: (0,) * len(shape))
    tile = lambda ch: pl.BlockSpec((1, tt, ch), lambda i, j: (i, j, 0))
    return pl.pallas_call(
        _even_front_kernel,
        grid=(b, t // tt),
        in_specs=[tile(D_MODEL), const((1, D_MODEL)), const((D_MODEL, EVEN_IN)), const((D_MODEL, LANES)),
                  const((CONV_K * SUBLANES, CONV_CH)), const((1, CONV_CH)), const((1, CONV_CH)), const((1, CONV_CH)),
                  const((SC_K * SUBLANES, QKV_CH)), const((1, LANES)), const((1, LANES))],
        out_specs=[tile(CONV_CH), tile(DN_WIDTH), tile(LANES), tile(DN_WIDTH),
                   pl.BlockSpec((1, tt // c, 2 * c, DN_WIDTH), lambda i, j: (i, j, 0, 0)),
                   tile(DN_WIDTH),
                   pl.BlockSpec((1, DN_HEADS, tt, c), lambda i, j: (i, 0, j, 0)),
                   pl.BlockSpec((1, 1, CONV_K - 1, CONV_CH), lambda i, j: (0, i, 0, 0)),
                   pl.BlockSpec((1, 1, SC_K - 1, QKV_CH), lambda i, j: (0, i, 0, 0))],
        out_shape=[jax.ShapeDtypeStruct((b, t, CONV_CH), BF16),
                   jax.ShapeDtypeStruct((b, t, DN_WIDTH), F32),
                   jax.ShapeDtypeStruct((b, t, LANES), F32),
                   jax.ShapeDtypeStruct((b, t, DN_WIDTH), F32),
                   jax.ShapeDtypeStruct((b, t // c, 2 * c, DN_WIDTH), BF16),
                   jax.ShapeDtypeStruct((b, t, DN_WIDTH), BF16),
                   jax.ShapeDtypeStruct((b, DN_HEADS, t, c), BF16),
                   jax.ShapeDtypeStruct((1, b, CONV_K - 1, CONV_CH), F32),
                   jax.ShapeDtypeStruct((1, b, SC_K - 1, QKV_CH), F32)],
        scratch_shapes=[pltpu.VMEM((CONV_HIST + tt, CONV_CH), F32),
                        pltpu.VMEM((SC_HIST + tt, QKV_CH), F32),
                        pltpu.VMEM((tt, QKV_CH), F32)],
        compiler_params=_params(2),
        name="p_even_front",
    )(x, g, w, wt, dww, dwb, lng, lnb, scw, alog, dtb)


def _delta_prep_tile(q2, k2, v2, gb, u_ref, wq_ref, kd_ref, qk_ref, fillers=()):
    fillers = iter(fillers)
    tt = PROMPT_TILE
    c = DN_CHUNK
    dh = DN_HEAD_DIM
    nch = tt // c
    row = lax.broadcasted_iota(jnp.int32, (c, c), 0)
    col = lax.broadcasted_iota(jnp.int32, (c, c), 1)
    incl = (row >= col)[None]
    strict = (row > col)[None]
    eye = jnp.where(row == col, 1.0, 0.0)[None]
    trow = lax.broadcasted_iota(jnp.int32, (tt, tt), 0)
    tcol = lax.broadcasted_iota(jnp.int32, (tt, tt), 1)
    same_chunk = jnp.right_shift(trow, 6) == jnp.right_shift(tcol, 6)
    tri = jnp.where((trow >= tcol) & same_chunk, 1.0, 0.0).astype(BF16)

    def lane_rep(col0):
        return jnp.concatenate([jnp.broadcast_to(gb[:, col0 + hd:col0 + hd + 1], (tt, dh))
                                for hd in range(DN_HEADS)], axis=1)

    def chunks(x):
        return jnp.concatenate([x[:, hd * dh:(hd + 1) * dh].reshape(nch, c, dh) for hd in range(DN_HEADS)], axis=0)

    gc4 = _dot_exact_lhs(tri, lane_rep(DN_HEADS))
    gc = chunks(gc4)
    gc_rows = []
    for hd in range(DN_HEADS):
        gc_t = gc4[:, hd * dh:(hd + 1) * dh].T
        gc_rows += [gc_t[0:c, n * c:(n + 1) * c][None] for n in range(nch)]
    gc_row = jnp.concatenate(gc_rows, axis=0)
    beta = chunks(lane_rep(0))
    q = chunks(q2)
    k = chunks(k2)
    v = chunks(v2)
    egc = jnp.exp(gc)
    kb = k * beta
    decay = jnp.where(incl, jnp.exp(jnp.where(incl, gc[:, :, 0:c] - gc_row, 0.0)), 0.0)
    a = jnp.where(strict, _bmm_nt(kb, k) * decay, 0.0)
    rhs = jnp.concatenate([v * beta, kb * egc], axis=2)
    p = -a
    t_inv = eye + p
    for _ in range(5):
        p = _bmm(p, p)
        t_inv = t_inv + _bmm(p, t_inv)
        next(fillers, lambda: None)()
    x0 = _bmm(t_inv, rhs)
    rho = rhs - x0 - _bmm_hi(a, x0)
    x = x0 + _bmm(t_inv, rho)
    k_dec = k * jnp.exp(gc[:, c - 1:c, :] - gc)
    q_dec = q * egc
    qk = jnp.where(incl, _bmm_nt(q, k) * decay, 0.0)
    for hd in range(DN_HEADS):
        lo = hd * dh
        hs = slice(hd * nch, (hd + 1) * nch)
        u_ref[0, :, lo:lo + dh] = x[hs, :, 0:dh].reshape(tt, dh)
        wq_ref[0, :, 0:c, lo:lo + dh] = x[hs, :, dh:].astype(BF16)
        wq_ref[0, :, c:2 * c, lo:lo + dh] = q_dec[hs].astype(BF16)
        kd_ref[0, :, lo:lo + dh] = k_dec[hs].reshape(tt, dh).astype(BF16)
        qk_ref[0, hd] = qk[hs].reshape(tt, c).astype(BF16)
    for rest in fillers:
        rest()


def _delta_scan_kernel(u_ref, wq_ref, kd_ref, qk_ref, gb_ref, z_ref, dng_ref, bout_ref, sout_ref, s_ref):
    c = DN_CHUNK
    dh = DN_HEAD_DIM
    nb = u_ref.shape[0]
    n = pl.program_id(0)
    last = pl.num_programs(0) - 1

    @pl.when(n == 0)
    def _():
        s_ref[...] = jnp.zeros(s_ref.shape, F32)

    def heads(ref):
        return jnp.concatenate([ref[:, :, hd * dh:(hd + 1) * dh] for hd in range(DN_HEADS)], axis=0)

    g_tot = jnp.sum(gb_ref[...], axis=1, keepdims=True)
    g_last = jnp.exp(jnp.concatenate(
        [jnp.broadcast_to(g_tot[:, :, DN_HEADS + hd:DN_HEADS + hd + 1], (nb, 1, dh)) for hd in range(DN_HEADS)],
        axis=0))
    wq = jnp.concatenate([wq_ref[:, 0, :, hd * dh:(hd + 1) * dh] for hd in range(DN_HEADS)], axis=0)
    qk = jnp.concatenate([qk_ref[:, hd] for hd in range(DN_HEADS)], axis=0)
    s = s_ref[...]
    ws = _bmm(wq, s)
    v_new = heads(u_ref) - ws[:, 0:c]
    o = ws[:, c:] + _bmm(qk, v_new)
    s_ref[...] = s * g_last + _bmm_tn(heads(kd_ref), v_new)
    out = (_rms(o, dng_ref[...]) * _silu(heads(z_ref))).astype(BF16)
    for hd in range(DN_HEADS):
        bout_ref[:, :, hd * dh:(hd + 1) * dh] = out[hd * nb:(hd + 1) * nb]

    @pl.when(n == last)
    def _():
        for hd in range(DN_HEADS):
            sout_ref[0, :, hd] = s_ref[hd * nb:(hd + 1) * nb]


def _delta_scan(u, wq, kd, qk, gb, z, dng):
    b, t, _ = u.shape
    c = DN_CHUNK
    rows = lambda ch: pl.BlockSpec((b, c, ch), lambda n: (0, n, 0))
    return pl.pallas_call(
        _delta_scan_kernel,
        grid=(t // c,),
        in_specs=[rows(DN_WIDTH),
                  pl.BlockSpec((b, 1, 2 * c, DN_WIDTH), lambda n: (0, n, 0, 0)),
                  rows(DN_WIDTH),
                  pl.BlockSpec((b, DN_HEADS, c, c), lambda n: (0, 0, n, 0)),
                  rows(LANES), rows(DN_WIDTH),
                  pl.BlockSpec((1, DN_HEAD_DIM), lambda n: (0, 0))],
        out_specs=[rows(DN_WIDTH),
                   pl.BlockSpec((1, b, DN_HEADS, DN_HEAD_DIM, DN_HEAD_DIM), lambda n: (0, 0, 0, 0, 0))],
        out_shape=[jax.ShapeDtypeStruct((b, t, DN_WIDTH), BF16),
                   jax.ShapeDtypeStruct((1, b, DN_HEADS, DN_HEAD_DIM, DN_HEAD_DIM), F32)],
        scratch_shapes=[pltpu.VMEM((DN_HEADS * b, DN_HEAD_DIM, DN_HEAD_DIM), F32)],
        compiler_params=_params(1),
        name="p_delta_scan",
    )(u, wq, kd, qk, gb, z, dng)


def _row_streams(tt):
    rows = tt // POST_STREAMS
    return [slice(i * rows, (i + 1) * rows) for i in range(POST_STREAMS)]


def _split_head_dim(x):
    lead = x.shape[:-2]
    halves = XA_HEAD_DIM // LANES
    x = x.reshape(lead + (XA_HEADS, halves, LANES))
    x = jnp.swapaxes(x, -3, -2)
    return x.reshape(lead + (halves * XA_HEADS, LANES))


def _merge_head_dim(x):
    lead = x.shape[:-2]
    halves = XA_HEAD_DIM // LANES
    x = jnp.swapaxes(x.reshape(lead + (halves, XA_HEADS, LANES)), -3, -2)
    return x.reshape(lead + (XA_HEADS, XA_HEAD_DIM))


def _sample_xattn_rows(q_ref, mk_ref, mv_ref, o_ref):
    def row(i):
        prod = mk_ref[0, i] * q_ref[i]
        prod = prod + pltpu.roll(prod, XA_HEADS, 1)
        s = jnp.sum(prod, axis=-1, keepdims=True) * (XA_HEAD_DIM ** -0.5)
        e = jnp.exp(s - jnp.max(s, axis=0, keepdims=True))
        o_ref[i] = jnp.sum(e * mv_ref[0, i], axis=0) / jnp.sum(e, axis=0)
    return [functools.partial(row, i) for i in range(XATTN_ROW_BLOCK)]


def _xattn_streams(x1s, g, wq_ref, wo_ref, mk_ref, mv_ref, fillers=()):
    fillers = iter(fillers)
    qs = [jnp.dot(_rms(x1, g).astype(BF16), wq_ref[0], preferred_element_type=F32) for x1 in x1s]
    next(fillers, lambda: None)()
    outs = [[] for _ in x1s]
    for hd in range(XA_HEADS):
        lo = hd * XA_HEAD_DIM
        ss = [_dot_nt(q[:, lo:lo + XA_HEAD_DIM], mk_ref[0, :, lo:lo + XA_HEAD_DIM]) * (XA_HEAD_DIM ** -0.5)
              for q in qs]
        for out, sc in zip(outs, ss):
            out.append(_dot(_softmax_lanes(sc), mv_ref[0, :, lo:lo + XA_HEAD_DIM]).astype(BF16))
    next(fillers, lambda: None)()
    x2s = [x1 + jnp.dot(jnp.concatenate(out, axis=1), wo_ref[0], preferred_element_type=F32)
           for x1, out in zip(x1s, outs)]
    for rest in fillers:
        rest()
    return x2s


def _even_post_kernel(a_ref, b_ref, x_ref, wout_ref, gx_ref, wq_ref, wo_ref, mk_ref, mv_ref, sq_ref, smk_ref, smv_ref,
                      o_ref, so_ref):
    streams = _row_streams(POST_TILE)
    fillers = _sample_xattn_rows(sq_ref, smk_ref, smv_ref, so_ref)
    x1s = [x_ref[0, rs, :] + jnp.dot(jnp.concatenate([a_ref[0, rs, :], b_ref[0, rs, :]], axis=1), wout_ref[...],
                                     preferred_element_type=F32) for rs in streams]
    fillers.pop(0)()
    for rs, x2 in zip(streams, _xattn_streams(x1s, gx_ref[...], wq_ref, wo_ref, mk_ref, mv_ref, fillers)):
        o_ref[0, rs, :] = x2


def _layer_specs(layer, n_tiles):
    sub = XA_HEADS * XA_HEAD_DIM // LANES
    step = lambda i, j: i * n_tiles + j
    weight = pl.BlockSpec((1, D_MODEL, D_MODEL), lambda i, j: (layer, 0, 0))
    mem = pl.BlockSpec((1, N_MEM, D_MODEL), lambda i, j: (layer, i, 0))
    srows = pl.BlockSpec((XATTN_ROW_BLOCK, sub, LANES), lambda i, j: (step(i, j), 0, 0))
    scache = pl.BlockSpec((1, XATTN_ROW_BLOCK, N_MEM, sub, LANES), lambda i, j: (layer, step(i, j), 0, 0, 0))
    return weight, mem, srows, scache


def _even_post(a, bo, x, wout, gx, wq, wo, mk, mv, layer, sq, smk, smv):
    b, t, _ = x.shape
    tt = POST_TILE
    ns = sq.shape[0]
    assert ns == b * (t // tt) * XATTN_ROW_BLOCK
    const = lambda shape: pl.BlockSpec(shape, lambda i, j: (0,) * len(shape))
    tile = lambda c: pl.BlockSpec((1, tt, c), lambda i, j: (i, j, 0))
    weight, mem, srows, scache = _layer_specs(layer, t // tt)
    return pl.pallas_call(
        _even_post_kernel,
        grid=(b, t // tt),
        in_specs=[tile(CONV_CH), tile(DN_WIDTH), tile(D_MODEL), const((D_MODEL, D_MODEL)), const((1, D_MODEL)),
                  weight, weight, mem, mem, srows, scache, scache],
        out_specs=[tile(D_MODEL), srows],
        out_shape=[jax.ShapeDtypeStruct((b, t, D_MODEL), F32), jax.ShapeDtypeStruct(sq.shape, F32)],
        compiler_params=_params(2),
        name="p_even_post",
    )(a, bo, x, wout, gx, wq, wo, mk, mv, sq, smk, smv)


def _pool_group_linear(pooled, wp_ref, bp_ref):
    outs = []
    for gi in range(len(POOL_WINDOWS)):
        lo = gi * POOL_GROUP
        outs.append(_dot(pooled[:, lo:lo + POOL_GROUP], wp_ref[gi]) + bp_ref[gi:gi + 1, :])
    return jnp.concatenate(outs, axis=1)


def _odd_kernel(x_ref, g_ref, win_ref, wp_ref, bp_ref, sc_ref, wout_ref, gx_ref, wq_ref, wo_ref, mk_ref, mv_ref,
                gf_ref, sq_ref, smk_ref, smv_ref, y_ref, pst_ref, so_ref, pbuf):
    tt = POST_TILE
    j = pl.program_id(1)
    last = pl.num_programs(1) - 1
    streams = _row_streams(tt)
    fillers = _sample_xattn_rows(sq_ref, smk_ref, smv_ref, so_ref)

    @pl.when(j == 0)
    def _():
        pbuf[0:POOL_HIST, :] = jnp.zeros((POOL_HIST, D_MODEL), F32)

    @pl.when(j > 0)
    def _():
        pbuf[0:POOL_HIST, :] = pbuf[tt:tt + POOL_HIST, :]

    xs = [x_ref[0, rs, :] for rs in streams]
    us, gates = [], []
    for x, rs in zip(xs, streams):
        h = _rms(x, g_ref[...]).astype(BF16)
        u = jnp.dot(h, win_ref[:, 0:D_MODEL], preferred_element_type=F32)
        pbuf[POOL_HIST + rs.start:POOL_HIST + rs.stop, :] = u
        us.append(u)
        gates.append(_silu(jnp.dot(h, win_ref[:, D_MODEL:], preferred_element_type=F32)))
    fillers.pop(0)()

    x1s = []
    for x, u, gate, rs in zip(xs, us, gates, streams):
        pos = j * tt + rs.start + lax.broadcasted_iota(jnp.int32, (rs.stop - rs.start, 1), 0)
        acc = pbuf[rs.start:POOL_HIST + rs.stop, :]
        means = []
        for gi, win in enumerate(POOL_WINDOWS):
            acc = acc + pltpu.roll(acc, win // 2, 0)
            cnt = jnp.minimum(pos + 1, win).astype(F32)
            means.append(acc[POOL_HIST:, 0:POOL_GROUP] / cnt)
            acc = acc[:, POOL_GROUP:]
        pooled = jnp.concatenate(means, axis=1) - u
        z = _pool_group_linear(pooled, wp_ref, bp_ref) * sc_ref[...] * gate
        x1s.append(x + jnp.dot(z.astype(BF16), wout_ref[...], preferred_element_type=F32))
    for rs, x2 in zip(streams, _xattn_streams(x1s, gx_ref[...], wq_ref, wo_ref, mk_ref, mv_ref, fillers)):
        y_ref[0, rs, :] = _rms(x2, gf_ref[...])

    @pl.when(j == last)
    def _():
        pst_ref[0, 0] = pbuf[POOL_HIST + tt - POOL_BUF:POOL_HIST + tt, :]


def _odd(x, g, win, wp, bp, sc, wout, gx, wq, wo, mk, mv, gf, layer, sq, smk, smv):
    b, t, _ = x.shape
    tt = POST_TILE
    ns = sq.shape[0]
    assert ns == b * (t // tt) * XATTN_ROW_BLOCK
    const = lambda shape: pl.BlockSpec(shape, lambda i, j: (0,) * len(shape))
    tile = lambda c: pl.BlockSpec((1, tt, c), lambda i, j: (i, j, 0))
    weight, mem, srows, scache = _layer_specs(layer, t // tt)
    ngrp = len(POOL_WINDOWS)
    return pl.pallas_call(
        _odd_kernel,
        grid=(b, t // tt),
        in_specs=[tile(D_MODEL), const((1, D_MODEL)), const((D_MODEL, 2 * D_MODEL)),
                  const((ngrp, POOL_GROUP, POOL_GROUP)), const((ngrp, POOL_GROUP)), const((1, D_MODEL)),
                  const((D_MODEL, D_MODEL)), const((1, D_MODEL)), weight, weight, mem, mem, const((1, D_MODEL)),
                  srows, scache, scache],
        out_specs=[tile(D_MODEL), pl.BlockSpec((1, 1, POOL_BUF, D_MODEL), lambda i, j: (0, i, 0, 0)), srows],
        out_shape=[jax.ShapeDtypeStruct((b, t, D_MODEL), F32),
                   jax.ShapeDtypeStruct((1, b, POOL_BUF, D_MODEL), F32),
                   jax.ShapeDtypeStruct(sq.shape, F32)],
        scratch_shapes=[pltpu.VMEM((POOL_HIST + tt, D_MODEL), F32)],
        compiler_params=_params(2),
        name="p_odd_layer",
    )(x, g, win, wp, bp, sc, wout, gx, wq, wo, mk, mv, gf, sq, smk, smv)


def _push_row(old_ref, new_ref, row):
    depth = old_ref.shape[0]
    new_ref[0:depth - 1] = old_ref[1:depth]
    new_ref[depth - 1] = row


def _history_major(state):
    return jnp.transpose(state[0], (1, 0, 2))


def _s_even_pre_kernel(x_ref, g_ref, w_ref, wt_ref, dww_ref, dwb_ref, lng_ref, lnb_ref, scw_ref, alog_ref, dtb_ref,
                       cin_ref, qin_ref,
                       aout_ref, q_ref, k_ref, v_ref, gb_ref, z_ref, cout_ref, qout_ref):
    h = _rms(x_ref[...], g_ref[...]).astype(BF16)
    glu = (_dot(h, w_ref[:, COL_GLU_VAL:COL_GLU_VAL + CONV_CH])
           * _sigmoid(_dot(h, w_ref[:, COL_GLU_GATE:COL_GLU_GATE + CONV_CH])))
    nb = CONV_K - 1
    acc = dww_ref[nb:nb + 1, :] * glu
    for kk in range(nb):
        acc = acc + dww_ref[kk:kk + 1, :] * cin_ref[kk]
    _push_row(cin_ref, cout_ref, glu)
    c = _silu(_layer_norm(acc + dwb_ref[...], lng_ref[...], lnb_ref[...]))
    aout_ref[...] = c * _silu(_dot(h, w_ref[:, COL_GATE_A:COL_GATE_A + CONV_CH]))

    qkv = _dot(h, w_ref[:, COL_QKV:COL_QKV + QKV_CH])
    ns = SC_K - 1
    acc = scw_ref[ns:ns + 1, :] * qkv
    for kk in range(ns):
        acc = acc + scw_ref[kk:kk + 1, :] * qin_ref[kk]
    _push_row(qin_ref, qout_ref, qkv)
    acc = _silu(acc)
    for hd in range(DN_HEADS):
        lo = hd * DN_HEAD_DIM
        q_ref[:, lo:lo + DN_HEAD_DIM] = _l2n(acc[:, lo:lo + DN_HEAD_DIM]) * (DN_HEAD_DIM ** -0.5)
        k_ref[:, lo:lo + DN_HEAD_DIM] = _l2n(acc[:, DN_WIDTH + lo:DN_WIDTH + lo + DN_HEAD_DIM])
    v_ref[...] = acc[:, 2 * DN_WIDTH:]
    z_ref[...] = _dot(h, w_ref[:, COL_Z:COL_Z + DN_WIDTH])
    gb_ref[...] = _gate_params(_dot(h, wt_ref[...]), alog_ref[...], dtb_ref[...])


def _s_even_pre(x, g, w, wt, dww, dwb, lng, lnb, scw, alog, dtb, cin, qin):
    n = x.shape[0]
    rbk = SAMPLE_ROW_BLOCK
    const = lambda shape: pl.BlockSpec(shape, lambda i: (0,) * len(shape))
    rows = lambda c: pl.BlockSpec((rbk, c), lambda i: (i, 0))
    hist = lambda arr: pl.BlockSpec((arr.shape[0], rbk, arr.shape[2]), lambda i: (0, i, 0))
    return pl.pallas_call(
        _s_even_pre_kernel,
        grid=(n // rbk,),
        in_specs=[rows(D_MODEL), const((1, D_MODEL)), const((D_MODEL, EVEN_IN)), const((D_MODEL, LANES)),
                  const((CONV_K, CONV_CH)),
                  const((1, CONV_CH)), const((1, CONV_CH)), const((1, CONV_CH)), const((SC_K, QKV_CH)),
                  const((1, LANES)), const((1, LANES)), hist(cin), hist(qin)],
        out_specs=[rows(CONV_CH), rows(DN_WIDTH), rows(DN_WIDTH), rows(DN_WIDTH), rows(LANES), rows(DN_WIDTH),
                   hist(cin), hist(qin)],
        out_shape=[jax.ShapeDtypeStruct((n, CONV_CH), F32),
                   jax.ShapeDtypeStruct((n, DN_WIDTH), F32),
                   jax.ShapeDtypeStruct((n, DN_WIDTH), F32),
                   jax.ShapeDtypeStruct((n, DN_WIDTH), F32),
                   jax.ShapeDtypeStruct((n, LANES), F32),
                   jax.ShapeDtypeStruct((n, DN_WIDTH), F32),
                   jax.ShapeDtypeStruct(cin.shape, F32),
                   jax.ShapeDtypeStruct(qin.shape, F32)],
        compiler_params=_params(1),
        name="s_even_pre",
    )(x, g, w, wt, dww, dwb, lng, lnb, scw, alog, dtb, cin, qin)


def _s_delta_kernel(q_ref, k_ref, v_ref, gb_ref, z_ref, dng_ref, sin_ref, bout_ref, sout_ref):
    dh = DN_HEAD_DIM
    chains = [(i, hd) for i in range(DELTA_ROW_BLOCK) for hd in range(DN_HEADS)]
    vec = lambda ref, i, hd: ref[i:i + 1, hd * dh:(hd + 1) * dh]
    col = lambda ref, i, hd: jnp.broadcast_to(vec(ref, i, hd), (dh, dh)).T
    kcol = [col(k_ref, i, hd) for i, hd in chains]
    qcol = [col(q_ref, i, hd) for i, hd in chains]
    s = [sin_ref[0, i, hd] * jnp.exp(gb_ref[i:i + 1, DN_HEADS + hd:DN_HEADS + hd + 1]) for i, hd in chains]
    v_new = [(vec(v_ref, i, hd) - jnp.sum(kc * sc, axis=0, keepdims=True)) * gb_ref[i:i + 1, hd:hd + 1]
             for (i, hd), kc, sc in zip(chains, kcol, s)]
    s = [sc + kc * vn for sc, kc, vn in zip(s, kcol, v_new)]
    for (i, hd), sc, qc in zip(chains, s, qcol):
        sout_ref[0, i, hd] = sc
        o = jnp.sum(qc * sc, axis=0, keepdims=True)
        bout_ref[i:i + 1, hd * dh:(hd + 1) * dh] = _rms(o, dng_ref[...]) * _silu(vec(z_ref, i, hd))


def _s_delta(q, k, v, gb, z, dng, s_in):
    n = q.shape[0]
    rbk = DELTA_ROW_BLOCK
    rows = lambda c: pl.BlockSpec((rbk, c), lambda i: (i, 0))
    st = pl.BlockSpec((1, rbk, DN_HEADS, DN_HEAD_DIM, DN_HEAD_DIM), lambda i: (0, i, 0, 0, 0))
    return pl.pallas_call(
        _s_delta_kernel,
        grid=(n // rbk,),
        in_specs=[rows(DN_WIDTH), rows(DN_WIDTH), rows(DN_WIDTH), rows(LANES), rows(DN_WIDTH),
                  pl.BlockSpec((1, DN_HEAD_DIM), lambda i: (0, 0)), st],
        out_specs=[rows(DN_WIDTH), st],
        out_shape=[jax.ShapeDtypeStruct((n, DN_WIDTH), F32),
                   jax.ShapeDtypeStruct(s_in.shape, F32)],
        compiler_params=_params(1),
        name="s_delta",
    )(q, k, v, gb, z, dng, s_in)


def _s_mix_out_kernel(a_ref, b_ref, x_ref, wout_ref, gx_ref, wq_ref, x1_ref, q_ref):
    mix = jnp.concatenate([a_ref[...], b_ref[...]], axis=1).astype(BF16)
    x1 = x_ref[...] + jnp.dot(mix, wout_ref[...], preferred_element_type=F32)
    x1_ref[...] = x1
    q_ref[...] = jnp.dot(_rms(x1, gx_ref[...]).astype(BF16), wq_ref[0], preferred_element_type=F32)


def _s_mix_out(a, bo, x, wout, gx, wq, layer):
    n = x.shape[0]
    full = lambda arr: pl.BlockSpec(arr.shape, lambda i: (0,) * arr.ndim)
    args = (a, bo, x, wout, gx, wq)
    out = jax.ShapeDtypeStruct((n, D_MODEL), F32)
    return pl.pallas_call(
        _s_mix_out_kernel,
        grid=(1,),
        in_specs=[full(v) for v in args[:-1]] + [pl.BlockSpec((1, D_MODEL, D_MODEL), lambda i: (layer, 0, 0))],
        out_specs=[pl.BlockSpec((n, D_MODEL), lambda i: (0, 0))] * 2,
        out_shape=[out, out],
        compiler_params=_params(1),
        name="s_mix_out",
    )(*args)


def _s_odd_kernel(x1_ref, o_ref, wo_ref, g_ref, win_ref, wp_ref, bp_ref, sc_ref, wout_ref, gx_ref, wq_ref, pin_ref,
                  x1o_ref, q_ref, pout_ref):
    x = x1_ref[...] + _dot(o_ref[...], wo_ref[0])
    h = _rms(x, g_ref[...]).astype(BF16)
    u = jnp.dot(h, win_ref[:, 0:D_MODEL], preferred_element_type=F32)
    gate = _silu(jnp.dot(h, win_ref[:, D_MODEL:], preferred_element_type=F32))
    means = []
    for gi, win in enumerate(POOL_WINDOWS):
        lo = gi * POOL_GROUP
        acc = u[:, lo:lo + POOL_GROUP]
        for d in range(1, win):
            acc = acc + pin_ref[POOL_BUF - d, :, lo:lo + POOL_GROUP]
        means.append(acc / float(min(PAST_LEN + 1, win)))
    pooled = jnp.concatenate(means, axis=1) - u
    _push_row(pin_ref, pout_ref, u)
    z = _pool_group_linear(pooled, wp_ref, bp_ref) * sc_ref[...] * gate
    x1 = x + jnp.dot(z.astype(BF16), wout_ref[...], preferred_element_type=F32)
    x1o_ref[...] = x1
    q_ref[...] = jnp.dot(_rms(x1, gx_ref[...]).astype(BF16), wq_ref[0], preferred_element_type=F32)


def _s_odd(x1, o, wo, g, win, wp, bp, sc, wout, gx, wq, pin, layer):
    n = x1.shape[0]
    rbk = SAMPLE_ROW_BLOCK
    ngrp = len(POOL_WINDOWS)
    const = lambda shape: pl.BlockSpec(shape, lambda i: (0,) * len(shape))
    rows = lambda c: pl.BlockSpec((rbk, c), lambda i: (i, 0))
    hist = pl.BlockSpec((pin.shape[0], rbk, pin.shape[2]), lambda i: (0, i, 0))
    out = jax.ShapeDtypeStruct((n, D_MODEL), F32)
    return pl.pallas_call(
        _s_odd_kernel,
        grid=(n // rbk,),
        in_specs=[rows(D_MODEL), rows(D_MODEL), pl.BlockSpec((1, D_MODEL, D_MODEL), lambda i: (layer - 1, 0, 0)),
                  const((1, D_MODEL)),
                  const((D_MODEL, 2 * D_MODEL)), const((ngrp, POOL_GROUP, POOL_GROUP)), const((ngrp, POOL_GROUP)),
                  const((1, D_MODEL)), const((D_MODEL, D_MODEL)), const((1, D_MODEL)),
                  pl.BlockSpec((1, D_MODEL, D_MODEL), lambda i: (layer, 0, 0)),
                  hist],
        out_specs=[rows(D_MODEL), rows(D_MODEL), hist],
        out_shape=[out, out, jax.ShapeDtypeStruct(pin.shape, F32)],
        compiler_params=_params(1),
        name="s_odd",
    )(x1, o, wo, g, win, wp, bp, sc, wout, gx, wq, pin)


def _s_final_kernel(x1_ref, o_ref, wo_ref, gf_ref, y_ref):
    y_ref[...] = _rms(x1_ref[...] + _dot(o_ref[...], wo_ref[0]), gf_ref[...])


def _s_final(x1, o, wo, gf, layer):
    n = x1.shape[0]
    full = lambda arr: pl.BlockSpec(arr.shape, lambda i: (0,) * arr.ndim)
    args = (x1, o, wo, gf)
    return pl.pallas_call(
        _s_final_kernel,
        grid=(1,),
        in_specs=[full(x1), full(o), pl.BlockSpec((1, D_MODEL, D_MODEL), lambda i: (layer, 0, 0)), full(gf)],
        out_specs=pl.BlockSpec((n, D_MODEL), lambda i: (0, 0)),
        out_shape=jax.ShapeDtypeStruct((n, D_MODEL), F32),
        compiler_params=_params(1),
        name="s_final",
    )(*args)


def _lane_pad(vec, offset):
    return jnp.pad(vec.astype(F32), (offset, LANES - offset - vec.shape[0])).reshape(1, LANES)


def kernel(x_prompt, x_sample, state_conv_a, state_qkv_conv, state_delta, state_pool, cache_mem_k, cache_mem_v, mem_prompt, norm_mix, norm_xattn, norm_final, w_in_even, w_out_even, dw_w, dw_b, ln_a_g, ln_a_b, sc_w, a_log, dt_bias, dn_norm_g, w_in_odd, w_pool, b_pool, pool_scale, w_out_odd, w_xq, w_xk, w_xv, w_xo):
    bp, t, _ = x_prompt.shape
    ns = x_sample.shape[0]
    row = lambda v: v.reshape(1, -1)

    w_in0 = w_in_even[0].astype(BF16)
    w_tail0 = jnp.pad(w_in_even[0][:, COL_TAIL:], ((0, 0), (0, LANES - (EVEN_IN - COL_TAIL)))).astype(BF16)
    w_out0 = w_out_even[0].astype(BF16)
    w_in1 = w_in_odd[0].astype(BF16)
    w_pool1 = w_pool[0].astype(BF16)
    w_out1 = w_out_odd[0].astype(BF16)
    wq = w_xq.astype(BF16)
    wk = w_xk.astype(BF16)
    wv = w_xv.astype(BF16)
    wo = w_xo.astype(BF16)
    alog = _lane_pad(a_log[0], DN_HEADS)
    dtb = _lane_pad(dt_bias[0], DN_HEADS)
    even_small = (dw_w[0], row(dw_b[0]), row(ln_a_g[0]), row(ln_a_b[0]), sc_w[0], alog, dtb)
    dng = row(dn_norm_g[0])

    mk_f, mv_f, mk_b, mv_b = _mem_kv(mem_prompt.reshape(bp * N_MEM, D_MODEL), wk, wv)
    new_mem_k_p = _merge_head_dim(mk_f.reshape(DEPTH, bp, N_MEM, -1, LANES))
    new_mem_v_p = _merge_head_dim(mv_f.reshape(DEPTH, bp, N_MEM, -1, LANES))

    even_small_p = (jnp.repeat(dw_w[0], SUBLANES, axis=0),) + even_small[1:4] + (
        jnp.repeat(sc_w[0], SUBLANES, axis=0),) + even_small[5:]
    a_out, z, gb, u, wqd, kd, qk, new_conv_a_p, new_qkv_conv_p = _even_front(x_prompt, row(norm_mix[0]), w_in0,
                                                                             w_tail0, *even_small_p)
    b_out, new_delta_p = _delta_scan(u, wqd, kd, qk, gb, z, dng)

    xs = x_sample.reshape(ns, D_MODEL)
    sa, sq, sk, sv, sgb, sz, cout, qout = _s_even_pre(xs, row(norm_mix[0]), w_in0, w_tail0, *even_small,
                                                      _history_major(state_conv_a), _history_major(state_qkv_conv))
    sb, new_delta_s = _s_delta(sq, sk, sv, sgb, sz, dng, state_delta)
    x1, xq = _s_mix_out(sa, sb, xs, w_out0, row(norm_xattn[0]), wq, 0)

    heads = lambda v: _split_head_dim(v.reshape(ns, XA_HEADS, XA_HEAD_DIM))
    unheads = lambda v: _merge_head_dim(v).reshape(ns, D_MODEL)
    cmk, cmv = _split_head_dim(cache_mem_k), _split_head_dim(cache_mem_v)
    x2, o0 = _even_post(a_out, b_out, x_prompt, w_out0, row(norm_xattn[0]), wq, wo, mk_b, mv_b, 0, heads(xq), cmk, cmv)
    x1, xq, pout = _s_odd(x1, unheads(o0), wo, row(norm_mix[1]), w_in1, w_pool1, b_pool[0], row(pool_scale[0]), w_out1,
                          row(norm_xattn[1]), wq, _history_major(state_pool), 1)
    y_prompt, new_pool_p, o1 = _odd(x2, row(norm_mix[1]), w_in1, w_pool1, b_pool[0], row(pool_scale[0]), w_out1,
                                    row(norm_xattn[1]), wq, wo, mk_b, mv_b, row(norm_final), 1, heads(xq), cmk, cmv)
    y_sample = _s_final(x1, unheads(o1), wo, row(norm_final), 1).reshape(ns, 1, D_MODEL)

    new_conv_a_s, new_qkv_conv_s, new_pool_s = (jnp.transpose(st, (1, 0, 2))[None] for st in (cout, qout, pout))
    return (y_prompt, y_sample, new_conv_a_p, new_qkv_conv_p, new_delta_p, new_pool_p, new_mem_k_p,
            new_mem_v_p, new_conv_a_s, new_qkv_conv_s, new_delta_s, new_pool_s)
```

```python
import functools

import jax
import jax.numpy as jnp
from jax import lax
from jax.experimental import pallas as pl
from jax.experimental.pallas import tpu as pltpu

F32 = jnp.float32
BF16 = jnp.bfloat16

D_MODEL = 1024
DEPTH = 2
PAST_LEN = 16384
CONV_CH = 512
CONV_K = 31
DN_HEAD_DIM = 128
DN_HEADS = 4
DN_WIDTH = 512
QKV_CH = 1536
SC_K = 4
DN_CHUNK = 64
EVEN_IN = 3592
POOL_WINDOWS = (2, 4, 8, 16)
POOL_GROUP = 256
POOL_BUF = 15
N_MEM = 256
XA_HEADS = 4
XA_HEAD_DIM = 256
EPS = 1e-6

LANES = 128
SUBLANES = 8
VMEM_LIMIT_BYTES = 56 * 1024 * 1024

COL_GLU_VAL = 0
COL_GLU_GATE = 512
COL_GATE_A = 1024
COL_QKV = 1536
COL_Z = 3072
COL_TAIL = 3584

PROMPT_TILE = 256
FRONT_ROWS = 2
POST_TILE = 512
POST_STREAMS = 1
CONV_HIST = 32
SC_HIST = 8
POOL_HIST = 16
CONV_ROW_BLOCK = 64
CONV_COL_BLOCK = 256
SAMPLE_ROW_BLOCK = 32
DELTA_ROW_BLOCK = 8
XATTN_ROW_BLOCK = 4


def _params(n_axes):
    return pltpu.CompilerParams(dimension_semantics=("arbitrary",) * n_axes,
                                vmem_limit_bytes=VMEM_LIMIT_BYTES)


def _dot(a, b):
    return jnp.dot(a.astype(BF16), b.astype(BF16), preferred_element_type=F32)


def _dot_nt(a, b):
    return lax.dot_general(a.astype(BF16), b.astype(BF16), (((1,), (1,)), ((), ())),
                           preferred_element_type=F32)


def _split3(x):
    x1 = x.astype(BF16)
    r1 = x - x1.astype(F32)
    x2 = r1.astype(BF16)
    x3 = (r1 - x2.astype(F32)).astype(BF16)
    return x1, x2, x3


def _split2(x):
    x1 = x.astype(BF16)
    return x1, (x - x1.astype(F32)).astype(BF16)


def _bmm(a, b):
    return lax.dot_general(a.astype(BF16), b.astype(BF16), (((2,), (1,)), ((0,), (0,))),
                           preferred_element_type=F32)


def _bmm_nt(a, b):
    return lax.dot_general(a.astype(BF16), b.astype(BF16), (((2,), (2,)), ((0,), (0,))),
                           preferred_element_type=F32)


def _bmm_tn(a, b):
    return lax.dot_general(a.astype(BF16), b.astype(BF16), (((1,), (1,)), ((0,), (0,))),
                           preferred_element_type=F32)


def _dot_exact_lhs(a_bf, b):
    b1, b2, b3 = _split3(b)
    d = functools.partial(jnp.dot, preferred_element_type=F32)
    return d(a_bf, b1) + d(a_bf, b2) + d(a_bf, b3)


def _sigmoid(x):
    return 1.0 / (1.0 + jnp.exp(-x))


def _silu(x):
    return x * _sigmoid(x)


def _softplus(x):
    return jnp.maximum(x, 0.0) + jnp.log(1.0 + jnp.exp(-jnp.abs(x)))


def _rms(x, g):
    return x * lax.rsqrt(jnp.mean(x * x, axis=-1, keepdims=True) + EPS) * g


def _layer_norm(x, g, b):
    xc = x - jnp.mean(x, axis=-1, keepdims=True)
    return xc * lax.rsqrt(jnp.mean(xc * xc, axis=-1, keepdims=True) + EPS) * g + b


def _l2n(x):
    return x * lax.rsqrt(jnp.sum(x * x, axis=-1, keepdims=True) + EPS)


def _gate_params(tail, alog, dtb):
    lane = lax.broadcasted_iota(jnp.int32, tail.shape, 1)
    beta = _sigmoid(tail)
    g = -jnp.exp(alog) * _softplus(tail + dtb)
    return jnp.where(lane < DN_HEADS, beta, g)


def _softmax_lanes(s):
    m = jnp.max(s, axis=-1, keepdims=True)
    e = jnp.exp(s - m)
    return e / jnp.sum(e, axis=-1, keepdims=True)


def _mem_kv_kernel(x_ref, wk_ref, wv_ref, k_ref, v_ref, kb_ref, vb_ref):
    x = x_ref[...]
    k = _dot(x, wk_ref[0])
    v = _dot(x, wv_ref[0])
    halves = XA_HEAD_DIM // LANES
    for hd in range(XA_HEADS):
        for half in range(halves):
            lo = hd * XA_HEAD_DIM + half * LANES
            k_ref[0, :, half * XA_HEADS + hd, :] = k[:, lo:lo + LANES]
            v_ref[0, :, half * XA_HEADS + hd, :] = v[:, lo:lo + LANES]
    kb_ref[0] = k.astype(BF16)
    vb_ref[0] = v.astype(BF16)


def _mem_kv(mem2d, wk, wv):
    rows = mem2d.shape[0]
    tile = 512
    sub = XA_HEADS * XA_HEAD_DIM // LANES
    f32_out = jax.ShapeDtypeStruct((DEPTH, rows, sub, LANES), F32)
    bf_out = jax.ShapeDtypeStruct((DEPTH, rows, D_MODEL), BF16)
    w_spec = pl.BlockSpec((1, D_MODEL, D_MODEL), lambda l, i: (l, 0, 0))
    o_spec = pl.BlockSpec((1, tile, D_MODEL), lambda l, i: (l, i, 0))
    f_spec = pl.BlockSpec((1, tile, sub, LANES), lambda l, i: (l, i, 0, 0))
    return pl.pallas_call(
        _mem_kv_kernel,
        grid=(DEPTH, rows // tile),
        in_specs=[pl.BlockSpec((tile, D_MODEL), lambda l, i: (i, 0)), w_spec, w_spec],
        out_specs=[f_spec, f_spec, o_spec, o_spec],
        out_shape=[f32_out, f32_out, bf_out, bf_out],
        compiler_params=_params(2),
        name="p_mem_kv",
    )(mem2d, wk, wv)


def _causal_conv(buf_ref, w_ref, n_taps, hist, r0, width):
    rb = CONV_ROW_BLOCK
    off = hist - (n_taps - 1)
    cols = []
    for c0 in range(0, width, CONV_COL_BLOCK):
        cs = slice(c0, c0 + CONV_COL_BLOCK)
        total = None
        for res in range(min(SUBLANES, n_taps)):
            base = (off + res) // SUBLANES * SUBLANES
            shift = off + res - base
            span = rb + (SUBLANES if shift else 0)
            part = None
            for kk in range(res, n_taps, SUBLANES):
                lo = r0 + base + kk - res
                rows = buf_ref[lo:lo + span, cs].reshape(span // SUBLANES, SUBLANES, CONV_COL_BLOCK)
                term = rows * w_ref[kk * SUBLANES:(kk + 1) * SUBLANES, cs][None]
                part = term if part is None else part + term
            part = part.reshape(span, CONV_COL_BLOCK)[shift:shift + rb, :]
            total = part if total is None else total + part
        cols.append(total)
    return jnp.concatenate(cols, axis=1)


def _zero_after(v):
    bits = lax.bitcast_convert_type(v, jnp.uint32)
    gone = lax.shift_right_logical(lax.shift_right_logical(bits, jnp.uint32(16)), jnp.uint32(16))
    return lax.bitcast_convert_type(gone, F32)


def _even_front_kernel(x_ref, g_ref, w_ref, wt_ref, dww_ref, dwb_ref, lng_ref, lnb_ref, scw_ref, alog_ref, dtb_ref,
                       aout_ref, z_ref, gb_ref, u_ref, wq_ref, kd_ref, qk_ref, cst_ref, qst_ref,
                       cbuf, qbuf, qkv, gate_a):
    tt = PROMPT_TILE
    j = pl.program_id(1)
    last = pl.num_programs(1) - 1

    @pl.when(j == 0)
    def _():
        cbuf[:, 0:CONV_HIST, :] = jnp.zeros((FRONT_ROWS, CONV_HIST, CONV_CH), F32)
        qbuf[:, 0:SC_HIST, :] = jnp.zeros((FRONT_ROWS, SC_HIST, QKV_CH), F32)

    @pl.when(j > 0)
    def _():
        cbuf[:, 0:CONV_HIST, :] = cbuf[:, tt:tt + CONV_HIST, :]
        qbuf[:, 0:SC_HIST, :] = qbuf[:, tt:tt + SC_HIST, :]

    def project(r, gain):
        h = _rms(x_ref[r], gain).astype(BF16)
        cbuf[r, CONV_HIST:CONV_HIST + tt, :] = (_dot(h, w_ref[:, COL_GLU_VAL:COL_GLU_VAL + CONV_CH])
                                                * _sigmoid(_dot(h, w_ref[:, COL_GLU_GATE:COL_GLU_GATE + CONV_CH])))
        gate_a[r] = _silu(_dot(h, w_ref[:, COL_GATE_A:COL_GATE_A + CONV_CH]))
        qbuf[r, SC_HIST:SC_HIST + tt, :] = _dot(h, w_ref[:, COL_QKV:COL_QKV + QKV_CH])
        gb_ref[r] = _gate_params(_dot(h, wt_ref[...]), alog_ref[...], dtb_ref[...])
        z = _dot(h, w_ref[:, COL_Z:COL_Z + DN_WIDTH])
        z_ref[r] = z
        return z[0:1, :]

    def finish(r):
        for r0 in range(0, tt, CONV_ROW_BLOCK):
            rows = slice(r0, r0 + CONV_ROW_BLOCK)
            acc = _silu(_causal_conv(qbuf.at[r], scw_ref, SC_K, SC_HIST, r0, QKV_CH))
            for hd in range(DN_HEADS):
                lo = hd * DN_HEAD_DIM
                qkv[r, rows, lo:lo + DN_HEAD_DIM] = _l2n(acc[:, lo:lo + DN_HEAD_DIM]) * (DN_HEAD_DIM ** -0.5)
                qkv[r, rows, DN_WIDTH + lo:DN_WIDTH + lo + DN_HEAD_DIM] = _l2n(
                    acc[:, DN_WIDTH + lo:DN_WIDTH + lo + DN_HEAD_DIM])
            qkv[r, rows, 2 * DN_WIDTH:] = acc[:, 2 * DN_WIDTH:]

        def conv_rows(r0):
            acc = _causal_conv(cbuf.at[r], dww_ref, CONV_K, CONV_HIST, r0, CONV_CH)
            c = _silu(_layer_norm(acc + dwb_ref[...], lng_ref[...], lnb_ref[...]))
            aout_ref[r, r0:r0 + CONV_ROW_BLOCK, :] = (c * gate_a[r, r0:r0 + CONV_ROW_BLOCK, :]).astype(BF16)

        _delta_prep_tile(qkv[r, :, 0:DN_WIDTH], qkv[r, :, DN_WIDTH:2 * DN_WIDTH], qkv[r, :, 2 * DN_WIDTH:], gb_ref[r],
                         u_ref.at[r], wq_ref.at[r], kd_ref.at[r], qk_ref.at[r],
                         fillers=[functools.partial(conv_rows, r0) for r0 in range(0, tt, CONV_ROW_BLOCK)])

    gain = g_ref[...]
    for r in range(FRONT_ROWS):
        anchor = project(r, gain)
        gain = g_ref[...] + jnp.concatenate([_zero_after(anchor)] * (D_MODEL // DN_WIDTH), axis=1)
    for r in range(FRONT_ROWS):
        finish(r)

    @pl.when(j == last)
    def _():
        cst_ref[0] = cbuf[:, CONV_HIST + tt - (CONV_K - 1):CONV_HIST + tt, :]
        qst_ref[0] = qbuf[:, SC_HIST + tt - (SC_K - 1):SC_HIST + tt, :]


def _even_front(x, g, w, wt, dww, dwb, lng, lnb, scw, alog, dtb):
    b, t, _ = x.shape
    tt = PROMPT_TILE
    c = DN_CHUNK
    nr = FRONT_ROWS
    const = lambda shape: pl.BlockSpec(shape, lambda i, j: (0,) * len(shape))
    tile = lambda ch: pl.BlockSpec((nr, tt, ch), lambda i, j: (i, j, 0))
    return pl.pallas_call(
        _even_front_kernel,
        grid=(b // nr, t // tt),
        in_specs=[tile(D_MODEL), const((1, D_MODEL)), const((D_MODEL, EVEN_IN)), const((D_MODEL, LANES)),
                  const((CONV_K * SUBLANES, CONV_CH)), const((1, CONV_CH)), const((1, CONV_CH)), const((1, CONV_CH)),
                  const((SC_K * SUBLANES, QKV_CH)), const((1, LANES)), const((1, LANES))],
        out_specs=[tile(CONV_CH), tile(DN_WIDTH), tile(LANES), tile(DN_WIDTH),
                   pl.BlockSpec((nr, tt // c, 2 * c, DN_WIDTH), lambda i, j: (i, j, 0, 0)),
                   tile(DN_WIDTH),
                   pl.BlockSpec((nr, DN_HEADS, tt, c), lambda i, j: (i, 0, j, 0)),
                   pl.BlockSpec((1, nr, CONV_K - 1, CONV_CH), lambda i, j: (0, i, 0, 0)),
                   pl.BlockSpec((1, nr, SC_K - 1, QKV_CH), lambda i, j: (0, i, 0, 0))],
        out_shape=[jax.ShapeDtypeStruct((b, t, CONV_CH), BF16),
                   jax.ShapeDtypeStruct((b, t, DN_WIDTH), F32),
                   jax.ShapeDtypeStruct((b, t, LANES), F32),
                   jax.ShapeDtypeStruct((b, t, DN_WIDTH), F32),
                   jax.ShapeDtypeStruct((b, t // c, 2 * c, DN_WIDTH), BF16),
                   jax.ShapeDtypeStruct((b, t, DN_WIDTH), BF16),
                   jax.ShapeDtypeStruct((b, DN_HEADS, t, c), BF16),
                   jax.ShapeDtypeStruct((1, b, CONV_K - 1, CONV_CH), F32),
                   jax.ShapeDtypeStruct((1, b, SC_K - 1, QKV_CH), F32)],
        scratch_shapes=[pltpu.VMEM((nr, CONV_HIST + tt, CONV_CH), F32),
                        pltpu.VMEM((nr, SC_HIST + tt, QKV_CH), F32),
                        pltpu.VMEM((nr, tt, QKV_CH), F32),
                        pltpu.VMEM((nr, tt, CONV_CH), F32)],
        compiler_params=_params(2),
        name="p_even_front",
    )(x, g, w, wt, dww, dwb, lng, lnb, scw, alog, dtb)


def _delta_prep_tile(q2, k2, v2, gb, u_ref, wq_ref, kd_ref, qk_ref, fillers=()):
    fillers = iter(fillers)
    tt = PROMPT_TILE
    c = DN_CHUNK
    dh = DN_HEAD_DIM
    nch = tt // c
    row = lax.broadcasted_iota(jnp.int32, (c, c), 0)
    col = lax.broadcasted_iota(jnp.int32, (c, c), 1)
    incl = (row >= col)[None]
    strict = (row > col)[None]
    eye = jnp.where(row == col, 1.0, 0.0)[None]
    trow = lax.broadcasted_iota(jnp.int32, (tt, tt), 0)
    tcol = lax.broadcasted_iota(jnp.int32, (tt, tt), 1)
    same_chunk = jnp.right_shift(trow, 6) == jnp.right_shift(tcol, 6)
    tri = jnp.where((trow >= tcol) & same_chunk, 1.0, 0.0).astype(BF16)

    def lane_rep(col0):
        return jnp.concatenate([jnp.broadcast_to(gb[:, col0 + hd:col0 + hd + 1], (tt, dh))
                                for hd in range(DN_HEADS)], axis=1)

    def chunks(x):
        return jnp.concatenate([x[:, hd * dh:(hd + 1) * dh].reshape(nch, c, dh) for hd in range(DN_HEADS)], axis=0)

    gc4 = _dot_exact_lhs(tri, lane_rep(DN_HEADS))
    gc = chunks(gc4)
    gc_rows = []
    for hd in range(DN_HEADS):
        gc_t = gc4[:, hd * dh:(hd + 1) * dh].T
        gc_rows += [gc_t[0:c, n * c:(n + 1) * c][None] for n in range(nch)]
    gc_row = jnp.concatenate(gc_rows, axis=0)
    beta = chunks(lane_rep(0))
    q = chunks(q2)
    k = chunks(k2)
    v = chunks(v2)
    egc = jnp.exp(gc)
    kb = k * beta
    decay = jnp.where(incl, jnp.exp(jnp.where(incl, gc[:, :, 0:c] - gc_row, 0.0)), 0.0)
    a = jnp.where(strict, _bmm_nt(kb, k) * decay, 0.0)
    rhs = jnp.concatenate([v * beta, kb * egc], axis=2)
    p = -a
    t_inv = eye + p
    for _ in range(5):
        p = _bmm(p, p)
        t_inv = t_inv + _bmm(p, t_inv)
        next(fillers, lambda: None)()
    x0 = _bmm(t_inv, rhs).astype(BF16).astype(F32)
    a_hi, a_lo = _split2(a)
    rho = rhs - x0 - (_bmm(a_hi, x0) + _bmm(a_lo, x0))
    x = x0 + _bmm(t_inv, rho)
    k_dec = k * jnp.exp(gc[:, c - 1:c, :] - gc)
    q_dec = q * egc
    qk = jnp.where(incl, _bmm_nt(q, k) * decay, 0.0)
    for hd in range(DN_HEADS):
        lo = hd * dh
        hs = slice(hd * nch, (hd + 1) * nch)
        u_ref[:, lo:lo + dh] = x[hs, :, 0:dh].reshape(tt, dh)
        wq_ref[:, 0:c, lo:lo + dh] = x[hs, :, dh:].astype(BF16)
        wq_ref[:, c:2 * c, lo:lo + dh] = q_dec[hs].astype(BF16)
        kd_ref[:, lo:lo + dh] = k_dec[hs].reshape(tt, dh).astype(BF16)
        qk_ref[hd] = qk[hs].reshape(tt, c).astype(BF16)
    for rest in fillers:
        rest()


def _delta_scan_kernel(u_ref, wq_ref, kd_ref, qk_ref, gb_ref, z_ref, dng_ref, bout_ref, sout_ref, s_ref):
    c = DN_CHUNK
    dh = DN_HEAD_DIM
    nb = u_ref.shape[0]
    n = pl.program_id(0)
    last = pl.num_programs(0) - 1

    @pl.when(n == 0)
    def _():
        s_ref[...] = jnp.zeros(s_ref.shape, F32)

    def heads(ref):
        return jnp.concatenate([ref[:, :, hd * dh:(hd + 1) * dh] for hd in range(DN_HEADS)], axis=0)

    g_tot = jnp.sum(gb_ref[...], axis=1, keepdims=True)
    g_last = jnp.exp(jnp.concatenate(
        [jnp.broadcast_to(g_tot[:, :, DN_HEADS + hd:DN_HEADS + hd + 1], (nb, 1, dh)) for hd in range(DN_HEADS)],
        axis=0))
    wq = jnp.concatenate([wq_ref[:, 0, :, hd * dh:(hd + 1) * dh] for hd in range(DN_HEADS)], axis=0)
    qk = jnp.concatenate([qk_ref[:, hd] for hd in range(DN_HEADS)], axis=0)
    s = s_ref[...]
    ws = _bmm(wq, s)
    v_new = heads(u_ref) - ws[:, 0:c]
    o = ws[:, c:] + _bmm(qk, v_new)
    s_ref[...] = s * g_last + _bmm_tn(heads(kd_ref), v_new)
    out = (_rms(o, dng_ref[...]) * _silu(heads(z_ref))).astype(BF16)
    for hd in range(DN_HEADS):
        bout_ref[:, :, hd * dh:(hd + 1) * dh] = out[hd * nb:(hd + 1) * nb]

    @pl.when(n == last)
    def _():
        for hd in range(DN_HEADS):
            sout_ref[0, :, hd] = s_ref[hd * nb:(hd + 1) * nb]


def _delta_scan(u, wq, kd, qk, gb, z, dng):
    b, t, _ = u.shape
    c = DN_CHUNK
    rows = lambda ch: pl.BlockSpec((b, c, ch), lambda n: (0, n, 0))
    return pl.pallas_call(
        _delta_scan_kernel,
        grid=(t // c,),
        in_specs=[rows(DN_WIDTH),
                  pl.BlockSpec((b, 1, 2 * c, DN_WIDTH), lambda n: (0, n, 0, 0)),
                  rows(DN_WIDTH),
                  pl.BlockSpec((b, DN_HEADS, c, c), lambda n: (0, 0, n, 0)),
                  rows(LANES), rows(DN_WIDTH),
                  pl.BlockSpec((1, DN_HEAD_DIM), lambda n: (0, 0))],
        out_specs=[rows(DN_WIDTH),
                   pl.BlockSpec((1, b, DN_HEADS, DN_HEAD_DIM, DN_HEAD_DIM), lambda n: (0, 0, 0, 0, 0))],
        out_shape=[jax.ShapeDtypeStruct((b, t, DN_WIDTH), BF16),
                   jax.ShapeDtypeStruct((1, b, DN_HEADS, DN_HEAD_DIM, DN_HEAD_DIM), F32)],
        scratch_shapes=[pltpu.VMEM((DN_HEADS * b, DN_HEAD_DIM, DN_HEAD_DIM), F32)],
        compiler_params=_params(1),
        name="p_delta_scan",
    )(u, wq, kd, qk, gb, z, dng)


def _row_streams(tt):
    rows = tt // POST_STREAMS
    return [slice(i * rows, (i + 1) * rows) for i in range(POST_STREAMS)]


def _split_head_dim(x):
    lead = x.shape[:-2]
    halves = XA_HEAD_DIM // LANES
    x = x.reshape(lead + (XA_HEADS, halves, LANES))
    x = jnp.swapaxes(x, -3, -2)
    return x.reshape(lead + (halves * XA_HEADS, LANES))


def _merge_head_dim(x):
    lead = x.shape[:-2]
    halves = XA_HEAD_DIM // LANES
    x = jnp.swapaxes(x.reshape(lead + (halves, XA_HEADS, LANES)), -3, -2)
    return x.reshape(lead + (XA_HEADS, XA_HEAD_DIM))


def _sample_xattn_rows(q_ref, mk_ref, mv_ref, o_ref):
    def row(i):
        prod = mk_ref[0, i] * q_ref[i]
        prod = prod + pltpu.roll(prod, XA_HEADS, 1)
        s = jnp.sum(prod, axis=-1, keepdims=True) * (XA_HEAD_DIM ** -0.5)
        e = jnp.exp(s - jnp.max(s, axis=0, keepdims=True))
        o_ref[i] = jnp.sum(e * mv_ref[0, i], axis=0) / jnp.sum(e, axis=0)
    return [functools.partial(row, i) for i in range(XATTN_ROW_BLOCK)]


def _xattn_streams(x1s, g, wq_ref, wo_ref, mk_ref, mv_ref, fillers=()):
    fillers = iter(fillers)
    qs = [jnp.dot(_rms(x1, g).astype(BF16), wq_ref[0], preferred_element_type=F32) for x1 in x1s]
    next(fillers, lambda: None)()
    outs = [[] for _ in x1s]
    for hd in range(XA_HEADS):
        lo = hd * XA_HEAD_DIM
        ss = [_dot_nt(q[:, lo:lo + XA_HEAD_DIM], mk_ref[0, :, lo:lo + XA_HEAD_DIM]) * (XA_HEAD_DIM ** -0.5)
              for q in qs]
        for out, sc in zip(outs, ss):
            out.append(_dot(_softmax_lanes(sc), mv_ref[0, :, lo:lo + XA_HEAD_DIM]).astype(BF16))
    next(fillers, lambda: None)()
    x2s = [x1 + jnp.dot(jnp.concatenate(out, axis=1), wo_ref[0], preferred_element_type=F32)
           for x1, out in zip(x1s, outs)]
    for rest in fillers:
        rest()
    return x2s


def _even_post_kernel(a_ref, b_ref, x_ref, wout_ref, gx_ref, wq_ref, wo_ref, mk_ref, mv_ref, sq_ref, smk_ref, smv_ref,
                      o_ref, so_ref):
    streams = _row_streams(POST_TILE)
    fillers = _sample_xattn_rows(sq_ref, smk_ref, smv_ref, so_ref)
    x1s = [x_ref[0, rs, :] + jnp.dot(jnp.concatenate([a_ref[0, rs, :], b_ref[0, rs, :]], axis=1), wout_ref[...],
                                     preferred_element_type=F32) for rs in streams]
    fillers.pop(0)()
    for rs, x2 in zip(streams, _xattn_streams(x1s, gx_ref[...], wq_ref, wo_ref, mk_ref, mv_ref, fillers)):
        o_ref[0, rs, :] = x2


def _layer_specs(layer, n_tiles):
    sub = XA_HEADS * XA_HEAD_DIM // LANES
    step = lambda i, j: i * n_tiles + j
    weight = pl.BlockSpec((1, D_MODEL, D_MODEL), lambda i, j: (layer, 0, 0))
    mem = pl.BlockSpec((1, N_MEM, D_MODEL), lambda i, j: (layer, i, 0))
    srows = pl.BlockSpec((XATTN_ROW_BLOCK, sub, LANES), lambda i, j: (step(i, j), 0, 0))
    scache = pl.BlockSpec((1, XATTN_ROW_BLOCK, N_MEM, sub, LANES), lambda i, j: (layer, step(i, j), 0, 0, 0))
    return weight, mem, srows, scache


def _even_post(a, bo, x, wout, gx, wq, wo, mk, mv, layer, sq, smk, smv):
    b, t, _ = x.shape
    tt = POST_TILE
    ns = sq.shape[0]
    assert ns == b * (t // tt) * XATTN_ROW_BLOCK
    const = lambda shape: pl.BlockSpec(shape, lambda i, j: (0,) * len(shape))
    tile = lambda c: pl.BlockSpec((1, tt, c), lambda i, j: (i, j, 0))
    weight, mem, srows, scache = _layer_specs(layer, t // tt)
    return pl.pallas_call(
        _even_post_kernel,
        grid=(b, t // tt),
        in_specs=[tile(CONV_CH), tile(DN_WIDTH), tile(D_MODEL), const((D_MODEL, D_MODEL)), const((1, D_MODEL)),
                  weight, weight, mem, mem, srows, scache, scache],
        out_specs=[tile(D_MODEL), srows],
        out_shape=[jax.ShapeDtypeStruct((b, t, D_MODEL), F32), jax.ShapeDtypeStruct(sq.shape, F32)],
        compiler_params=_params(2),
        name="p_even_post",
    )(a, bo, x, wout, gx, wq, wo, mk, mv, sq, smk, smv)


def _pool_group_linear(pooled, wp_ref, bp_ref):
    outs = []
    for gi in range(len(POOL_WINDOWS)):
        lo = gi * POOL_GROUP
        outs.append(_dot(pooled[:, lo:lo + POOL_GROUP], wp_ref[gi]) + bp_ref[gi:gi + 1, :])
    return jnp.concatenate(outs, axis=1)


def _odd_kernel(x_ref, g_ref, win_ref, wp_ref, bp_ref, sc_ref, wout_ref, gx_ref, wq_ref, wo_ref, mk_ref, mv_ref,
                gf_ref, sq_ref, smk_ref, smv_ref, y_ref, pst_ref, so_ref, pbuf):
    tt = POST_TILE
    j = pl.program_id(1)
    last = pl.num_programs(1) - 1
    streams = _row_streams(tt)
    fillers = _sample_xattn_rows(sq_ref, smk_ref, smv_ref, so_ref)

    @pl.when(j == 0)
    def _():
        pbuf[0:POOL_HIST, :] = jnp.zeros((POOL_HIST, D_MODEL), F32)

    @pl.when(j > 0)
    def _():
        pbuf[0:POOL_HIST, :] = pbuf[tt:tt + POOL_HIST, :]

    xs = [x_ref[0, rs, :] for rs in streams]
    us, gates = [], []
    for x, rs in zip(xs, streams):
        h = _rms(x, g_ref[...]).astype(BF16)
        u = jnp.dot(h, win_ref[:, 0:D_MODEL], preferred_element_type=F32)
        pbuf[POOL_HIST + rs.start:POOL_HIST + rs.stop, :] = u
        us.append(u)
        gates.append(_silu(jnp.dot(h, win_ref[:, D_MODEL:], preferred_element_type=F32)))
    fillers.pop(0)()

    x1s = []
    for x, u, gate, rs in zip(xs, us, gates, streams):
        pos = j * tt + rs.start + lax.broadcasted_iota(jnp.int32, (rs.stop - rs.start, 1), 0)
        acc = pbuf[rs.start:POOL_HIST + rs.stop, :]
        means = []
        for gi, win in enumerate(POOL_WINDOWS):
            acc = acc + pltpu.roll(acc, win // 2, 0)
            cnt = jnp.minimum(pos + 1, win).astype(F32)
            means.append(acc[POOL_HIST:, 0:POOL_GROUP] / cnt)
            acc = acc[:, POOL_GROUP:]
        pooled = jnp.concatenate(means, axis=1) - u
        z = _pool_group_linear(pooled, wp_ref, bp_ref) * sc_ref[...] * gate
        x1s.append(x + jnp.dot(z.astype(BF16), wout_ref[...], preferred_element_type=F32))
    for rs, x2 in zip(streams, _xattn_streams(x1s, gx_ref[...], wq_ref, wo_ref, mk_ref, mv_ref, fillers)):
        y_ref[0, rs, :] = _rms(x2, gf_ref[...])

    @pl.when(j == last)
    def _():
        pst_ref[0, 0] = pbuf[POOL_HIST + tt - POOL_BUF:POOL_HIST + tt, :]


def _odd(x, g, win, wp, bp, sc, wout, gx, wq, wo, mk, mv, gf, layer, sq, smk, smv):
    b, t, _ = x.shape
    tt = POST_TILE
    ns = sq.shape[0]
    assert ns == b * (t // tt) * XATTN_ROW_BLOCK
    const = lambda shape: pl.BlockSpec(shape, lambda i, j: (0,) * len(shape))
    tile = lambda c: pl.BlockSpec((1, tt, c), lambda i, j: (i, j, 0))
    weight, mem, srows, scache = _layer_specs(layer, t // tt)
    ngrp = len(POOL_WINDOWS)
    return pl.pallas_call(
        _odd_kernel,
        grid=(b, t // tt),
        in_specs=[tile(D_MODEL), const((1, D_MODEL)), const((D_MODEL, 2 * D_MODEL)),
                  const((ngrp, POOL_GROUP, POOL_GROUP)), const((ngrp, POOL_GROUP)), const((1, D_MODEL)),
                  const((D_MODEL, D_MODEL)), const((1, D_MODEL)), weight, weight, mem, mem, const((1, D_MODEL)),
                  srows, scache, scache],
        out_specs=[tile(D_MODEL), pl.BlockSpec((1, 1, POOL_BUF, D_MODEL), lambda i, j: (0, i, 0, 0)), srows],
        out_shape=[jax.ShapeDtypeStruct((b, t, D_MODEL), F32),
                   jax.ShapeDtypeStruct((1, b, POOL_BUF, D_MODEL), F32),
                   jax.ShapeDtypeStruct(sq.shape, F32)],
        scratch_shapes=[pltpu.VMEM((POOL_HIST + tt, D_MODEL), F32)],
        compiler_params=_params(2),
        name="p_odd_layer",
    )(x, g, win, wp, bp, sc, wout, gx, wq, wo, mk, mv, gf, sq, smk, smv)


def _push_row(old_ref, new_ref, row):
    depth = old_ref.shape[0]
    new_ref[0:depth - 1] = old_ref[1:depth]
    new_ref[depth - 1] = row


def _history_major(state):
    return jnp.transpose(state[0], (1, 0, 2))


def _s_even_pre_kernel(x_ref, g_ref, w_ref, wt_ref, dww_ref, dwb_ref, lng_ref, lnb_ref, scw_ref, alog_ref, dtb_ref,
                       cin_ref, qin_ref,
                       aout_ref, q_ref, k_ref, v_ref, gb_ref, z_ref, cout_ref, qout_ref):
    h = _rms(x_ref[...], g_ref[...]).astype(BF16)
    glu = (_dot(h, w_ref[:, COL_GLU_VAL:COL_GLU_VAL + CONV_CH])
           * _sigmoid(_dot(h, w_ref[:, COL_GLU_GATE:COL_GLU_GATE + CONV_CH])))
    nb = CONV_K - 1
    acc = dww_ref[nb:nb + 1, :] * glu
    for kk in range(nb):
        acc = acc + dww_ref[kk:kk + 1, :] * cin_ref[kk]
    _push_row(cin_ref, cout_ref, glu)
    c = _silu(_layer_norm(acc + dwb_ref[...], lng_ref[...], lnb_ref[...]))
    aout_ref[...] = c * _silu(_dot(h, w_ref[:, COL_GATE_A:COL_GATE_A + CONV_CH]))

    qkv = _dot(h, w_ref[:, COL_QKV:COL_QKV + QKV_CH])
    ns = SC_K - 1
    acc = scw_ref[ns:ns + 1, :] * qkv
    for kk in range(ns):
        acc = acc + scw_ref[kk:kk + 1, :] * qin_ref[kk]
    _push_row(qin_ref, qout_ref, qkv)
    acc = _silu(acc)
    for hd in range(DN_HEADS):
        lo = hd * DN_HEAD_DIM
        q_ref[:, lo:lo + DN_HEAD_DIM] = _l2n(acc[:, lo:lo + DN_HEAD_DIM]) * (DN_HEAD_DIM ** -0.5)
        k_ref[:, lo:lo + DN_HEAD_DIM] = _l2n(acc[:, DN_WIDTH + lo:DN_WIDTH + lo + DN_HEAD_DIM])
    v_ref[...] = acc[:, 2 * DN_WIDTH:]
    z_ref[...] = _dot(h, w_ref[:, COL_Z:COL_Z + DN_WIDTH])
    gb_ref[...] = _gate_params(_dot(h, wt_ref[...]), alog_ref[...], dtb_ref[...])


def _s_even_pre(x, g, w, wt, dww, dwb, lng, lnb, scw, alog, dtb, cin, qin):
    n = x.shape[0]
    rbk = SAMPLE_ROW_BLOCK
    const = lambda shape: pl.BlockSpec(shape, lambda i: (0,) * len(shape))
    rows = lambda c: pl.BlockSpec((rbk, c), lambda i: (i, 0))
    hist = lambda arr: pl.BlockSpec((arr.shape[0], rbk, arr.shape[2]), lambda i: (0, i, 0))
    return pl.pallas_call(
        _s_even_pre_kernel,
        grid=(n // rbk,),
        in_specs=[rows(D_MODEL), const((1, D_MODEL)), const((D_MODEL, EVEN_IN)), const((D_MODEL, LANES)),
                  const((CONV_K, CONV_CH)),
                  const((1, CONV_CH)), const((1, CONV_CH)), const((1, CONV_CH)), const((SC_K, QKV_CH)),
                  const((1, LANES)), const((1, LANES)), hist(cin), hist(qin)],
        out_specs=[rows(CONV_CH), rows(DN_WIDTH), rows(DN_WIDTH), rows(DN_WIDTH), rows(LANES), rows(DN_WIDTH),
                   hist(cin), hist(qin)],
        out_shape=[jax.ShapeDtypeStruct((n, CONV_CH), F32),
                   jax.ShapeDtypeStruct((n, DN_WIDTH), F32),
                   jax.ShapeDtypeStruct((n, DN_WIDTH), F32),
                   jax.ShapeDtypeStruct((n, DN_WIDTH), F32),
                   jax.ShapeDtypeStruct((n, LANES), F32),
                   jax.ShapeDtypeStruct((n, DN_WIDTH), F32),
                   jax.ShapeDtypeStruct(cin.shape, F32),
                   jax.ShapeDtypeStruct(qin.shape, F32)],
        compiler_params=_params(1),
        name="s_even_pre",
    )(x, g, w, wt, dww, dwb, lng, lnb, scw, alog, dtb, cin, qin)


def _s_delta_kernel(q_ref, k_ref, v_ref, gb_ref, z_ref, dng_ref, sin_ref, bout_ref, sout_ref):
    dh = DN_HEAD_DIM
    chains = [(i, hd) for i in range(DELTA_ROW_BLOCK) for hd in range(DN_HEADS)]
    vec = lambda ref, i, hd: ref[i:i + 1, hd * dh:(hd + 1) * dh]
    col = lambda ref, i, hd: jnp.broadcast_to(vec(ref, i, hd), (dh, dh)).T
    kcol = [col(k_ref, i, hd) for i, hd in chains]
    qcol = [col(q_ref, i, hd) for i, hd in chains]
    s = [sin_ref[0, i, hd] * jnp.exp(gb_ref[i:i + 1, DN_HEADS + hd:DN_HEADS + hd + 1]) for i, hd in chains]
    v_new = [(vec(v_ref, i, hd) - jnp.sum(kc * sc, axis=0, keepdims=True)) * gb_ref[i:i + 1, hd:hd + 1]
             for (i, hd), kc, sc in zip(chains, kcol, s)]
    s = [sc + kc * vn for sc, kc, vn in zip(s, kcol, v_new)]
    for (i, hd), sc, qc in zip(chains, s, qcol):
        sout_ref[0, i, hd] = sc
        o = jnp.sum(qc * sc, axis=0, keepdims=True)
        bout_ref[i:i + 1, hd * dh:(hd + 1) * dh] = _rms(o, dng_ref[...]) * _silu(vec(z_ref, i, hd))


def _s_delta(q, k, v, gb, z, dng, s_in):
    n = q.shape[0]
    rbk = DELTA_ROW_BLOCK
    rows = lambda c: pl.BlockSpec((rbk, c), lambda i: (i, 0))
    st = pl.BlockSpec((1, rbk, DN_HEADS, DN_HEAD_DIM, DN_HEAD_DIM), lambda i: (0, i, 0, 0, 0))
    return pl.pallas_call(
        _s_delta_kernel,
        grid=(n // rbk,),
        in_specs=[rows(DN_WIDTH), rows(DN_WIDTH), rows(DN_WIDTH), rows(LANES), rows(DN_WIDTH),
                  pl.BlockSpec((1, DN_HEAD_DIM), lambda i: (0, 0)), st],
        out_specs=[rows(DN_WIDTH), st],
        out_shape=[jax.ShapeDtypeStruct((n, DN_WIDTH), F32),
                   jax.ShapeDtypeStruct(s_in.shape, F32)],
        compiler_params=_params(1),
        name="s_delta",
    )(q, k, v, gb, z, dng, s_in)


def _s_mix_out_kernel(a_ref, b_ref, x_ref, wout_ref, gx_ref, wq_ref, x1_ref, q_ref):
    mix = jnp.concatenate([a_ref[...], b_ref[...]], axis=1).astype(BF16)
    x1 = x_ref[...] + jnp.dot(mix, wout_ref[...], preferred_element_type=F32)
    x1_ref[...] = x1
    q_ref[...] = jnp.dot(_rms(x1, gx_ref[...]).astype(BF16), wq_ref[0], preferred_element_type=F32)


def _s_mix_out(a, bo, x, wout, gx, wq, layer):
    n = x.shape[0]
    full = lambda arr: pl.BlockSpec(arr.shape, lambda i: (0,) * arr.ndim)
    args = (a, bo, x, wout, gx, wq)
    out = jax.ShapeDtypeStruct((n, D_MODEL), F32)
    return pl.pallas_call(
        _s_mix_out_kernel,
        grid=(1,),
        in_specs=[full(v) for v in args[:-1]] + [pl.BlockSpec((1, D_MODEL, D_MODEL), lambda i: (layer, 0, 0))],
        out_specs=[pl.BlockSpec((n, D_MODEL), lambda i: (0, 0))] * 2,
        out_shape=[out, out],
        compiler_params=_params(1),
        name="s_mix_out",
    )(*args)


def _s_odd_kernel(x1_ref, o_ref, wo_ref, g_ref, win_ref, wp_ref, bp_ref, sc_ref, wout_ref, gx_ref, wq_ref, pin_ref,
                  x1o_ref, q_ref, pout_ref):
    x = x1_ref[...] + _dot(o_ref[...], wo_ref[0])
    h = _rms(x, g_ref[...]).astype(BF16)
    u = jnp.dot(h, win_ref[:, 0:D_MODEL], preferred_element_type=F32)
    gate = _silu(jnp.dot(h, win_ref[:, D_MODEL:], preferred_element_type=F32))
    means = []
    for gi, win in enumerate(POOL_WINDOWS):
        lo = gi * POOL_GROUP
        acc = u[:, lo:lo + POOL_GROUP]
        for d in range(1, win):
            acc = acc + pin_ref[POOL_BUF - d, :, lo:lo + POOL_GROUP]
        means.append(acc / float(min(PAST_LEN + 1, win)))
    pooled = jnp.concatenate(means, axis=1) - u
    _push_row(pin_ref, pout_ref, u)
    z = _pool_group_linear(pooled, wp_ref, bp_ref) * sc_ref[...] * gate
    x1 = x + jnp.dot(z.astype(BF16), wout_ref[...], preferred_element_type=F32)
    x1o_ref[...] = x1
    q_ref[...] = jnp.dot(_rms(x1, gx_ref[...]).astype(BF16), wq_ref[0], preferred_element_type=F32)


def _s_odd(x1, o, wo, g, win, wp, bp, sc, wout, gx, wq, pin, layer):
    n = x1.shape[0]
    rbk = SAMPLE_ROW_BLOCK
    ngrp = len(POOL_WINDOWS)
    const = lambda shape: pl.BlockSpec(shape, lambda i: (0,) * len(shape))
    rows = lambda c: pl.BlockSpec((rbk, c), lambda i: (i, 0))
    hist = pl.BlockSpec((pin.shape[0], rbk, pin.shape[2]), lambda i: (0, i, 0))
    out = jax.ShapeDtypeStruct((n, D_MODEL), F32)
    return pl.pallas_call(
        _s_odd_kernel,
        grid=(n // rbk,),
        in_specs=[rows(D_MODEL), rows(D_MODEL), pl.BlockSpec((1, D_MODEL, D_MODEL), lambda i: (layer - 1, 0, 0)),
                  const((1, D_MODEL)),
                  const((D_MODEL, 2 * D_MODEL)), const((ngrp, POOL_GROUP, POOL_GROUP)), const((ngrp, POOL_GROUP)),
                  const((1, D_MODEL)), const((D_MODEL, D_MODEL)), const((1, D_MODEL)),
                  pl.BlockSpec((1, D_MODEL, D_MODEL), lambda i: (layer, 0, 0)),
                  hist],
        out_specs=[rows(D_MODEL), rows(D_MODEL), hist],
        out_shape=[out, out, jax.ShapeDtypeStruct(pin.shape, F32)],
        compiler_params=_params(1),
        name="s_odd",
    )(x1, o, wo, g, win, wp, bp, sc, wout, gx, wq, pin)


def _s_final_kernel(x1_ref, o_ref, wo_ref, gf_ref, y_ref):
    y_ref[...] = _rms(x1_ref[...] + _dot(o_ref[...], wo_ref[0]), gf_ref[...])


def _s_final(x1, o, wo, gf, layer):
    n = x1.shape[0]
    full = lambda arr: pl.BlockSpec(arr.shape, lambda i: (0,) * arr.ndim)
    args = (x1, o, wo, gf)
    return pl.pallas_call(
        _s_final_kernel,
        grid=(1,),
        in_specs=[full(x1), full(o), pl.BlockSpec((1, D_MODEL, D_MODEL), lambda i: (layer, 0, 0)), full(gf)],
        out_specs=pl.BlockSpec((n, D_MODEL), lambda i: (0, 0)),
        out_shape=jax.ShapeDtypeStruct((n, D_MODEL), F32),
        compiler_params=_params(1),
        name="s_final",
    )(*args)


def _lane_pad(vec, offset):
    return jnp.pad(vec.astype(F32), (offset, LANES - offset - vec.shape[0])).reshape(1, LANES)


def kernel(x_prompt, x_sample, state_conv_a, state_qkv_conv, state_delta, state_pool, cache_mem_k, cache_mem_v, mem_prompt, norm_mix, norm_xattn, norm_final, w_in_even, w_out_even, dw_w, dw_b, ln_a_g, ln_a_b, sc_w, a_log, dt_bias, dn_norm_g, w_in_odd, w_pool, b_pool, pool_scale, w_out_odd, w_xq, w_xk, w_xv, w_xo):
    bp, t, _ = x_prompt.shape
    ns = x_sample.shape[0]
    row = lambda v: v.reshape(1, -1)

    w_in0 = w_in_even[0].astype(BF16)
    w_tail0 = jnp.pad(w_in_even[0][:, COL_TAIL:], ((0, 0), (0, LANES - (EVEN_IN - COL_TAIL)))).astype(BF16)
    w_out0 = w_out_even[0].astype(BF16)
    w_in1 = w_in_odd[0].astype(BF16)
    w_pool1 = w_pool[0].astype(BF16)
    w_out1 = w_out_odd[0].astype(BF16)
    wq = w_xq.astype(BF16)
    wk = w_xk.astype(BF16)
    wv = w_xv.astype(BF16)
    wo = w_xo.astype(BF16)
    alog = _lane_pad(a_log[0], DN_HEADS)
    dtb = _lane_pad(dt_bias[0], DN_HEADS)
    even_small = (dw_w[0], row(dw_b[0]), row(ln_a_g[0]), row(ln_a_b[0]), sc_w[0], alog, dtb)
    dng = row(dn_norm_g[0])

    mk_f, mv_f, mk_b, mv_b = _mem_kv(mem_prompt.reshape(bp * N_MEM, D_MODEL), wk, wv)
    new_mem_k_p = _merge_head_dim(mk_f.reshape(DEPTH, bp, N_MEM, -1, LANES))
    new_mem_v_p = _merge_head_dim(mv_f.reshape(DEPTH, bp, N_MEM, -1, LANES))

    even_small_p = (jnp.repeat(dw_w[0], SUBLANES, axis=0),) + even_small[1:4] + (
        jnp.repeat(sc_w[0], SUBLANES, axis=0),) + even_small[5:]
    a_out, z, gb, u, wqd, kd, qk, new_conv_a_p, new_qkv_conv_p = _even_front(x_prompt, row(norm_mix[0]), w_in0,
                                                                             w_tail0, *even_small_p)
    b_out, new_delta_p = _delta_scan(u, wqd, kd, qk, gb, z, dng)

    xs = x_sample.reshape(ns, D_MODEL)
    sa, sq, sk, sv, sgb, sz, cout, qout = _s_even_pre(xs, row(norm_mix[0]), w_in0, w_tail0, *even_small,
                                                      _history_major(state_conv_a), _history_major(state_qkv_conv))
    sb, new_delta_s = _s_delta(sq, sk, sv, sgb, sz, dng, state_delta)
    x1, xq = _s_mix_out(sa, sb, xs, w_out0, row(norm_xattn[0]), wq, 0)

    heads = lambda v: _split_head_dim(v.reshape(ns, XA_HEADS, XA_HEAD_DIM))
    unheads = lambda v: _merge_head_dim(v).reshape(ns, D_MODEL)
    cmk, cmv = _split_head_dim(cache_mem_k), _split_head_dim(cache_mem_v)
    x2, o0 = _even_post(a_out, b_out, x_prompt, w_out0, row(norm_xattn[0]), wq, wo, mk_b, mv_b, 0, heads(xq), cmk, cmv)
    x1, xq, pout = _s_odd(x1, unheads(o0), wo, row(norm_mix[1]), w_in1, w_pool1, b_pool[0], row(pool_scale[0]), w_out1,
                          row(norm_xattn[1]), wq, _history_major(state_pool), 1)
    y_prompt, new_pool_p, o1 = _odd(x2, row(norm_mix[1]), w_in1, w_pool1, b_pool[0], row(pool_scale[0]), w_out1,
                                    row(norm_xattn[1]), wq, wo, mk_b, mv_b, row(norm_final), 1, heads(xq), cmk, cmv)
    y_sample = _s_final(x1, unheads(o1), wo, row(norm_final), 1).reshape(ns, 1, D_MODEL)

    new_conv_a_s, new_qkv_conv_s, new_pool_s = (jnp.transpose(st, (1, 0, 2))[None] for st in (cout, qout, pout))
    return (y_prompt, y_sample, new_conv_a_p, new_qkv_conv_p, new_delta_p, new_pool_p, new_mem_k_p,
            new_mem_v_p, new_conv_a_s, new_qkv_conv_s, new_delta_s, new_pool_s)
```

```python
import functools

import jax
import jax.numpy as jnp
from jax import lax
from jax.experimental import pallas as pl
from jax.experimental.pallas import tpu as pltpu

F32 = jnp.float32
BF16 = jnp.bfloat16

D_MODEL = 1024
DEPTH = 2
PAST_LEN = 16384
CONV_CH = 512
CONV_K = 31
DN_HEAD_DIM = 128
DN_HEADS = 4
DN_WIDTH = 512
QKV_CH = 1536
SC_K = 4
DN_CHUNK = 64
EVEN_IN = 3592
POOL_WINDOWS = (2, 4, 8, 16)
POOL_GROUP = 256
POOL_BUF = 15
N_MEM = 256
XA_HEADS = 4
XA_HEAD_DIM = 256
EPS = 1e-6

LANES = 128
SUBLANES = 8
VMEM_LIMIT_BYTES = 56 * 1024 * 1024

COL_GLU_VAL = 0
COL_GLU_GATE = 512
COL_GATE_A = 1024
COL_QKV = 1536
COL_Z = 3072
COL_TAIL = 3584

PROMPT_TILE = 256
FRONT_ROWS = 2
POST_TILE = 512
POST_STREAMS = 1
CONV_HIST = 32
SC_HIST = 8
POOL_HIST = 16
CONV_ROW_BLOCK = 64
CONV_COL_BLOCK = 256
SAMPLE_ROW_BLOCK = 32
DELTA_ROW_BLOCK = 8
XATTN_ROW_BLOCK = 4


def _params(n_axes):
    return pltpu.CompilerParams(dimension_semantics=("arbitrary",) * n_axes,
                                vmem_limit_bytes=VMEM_LIMIT_BYTES)


def _dot(a, b):
    return jnp.dot(a.astype(BF16), b.astype(BF16), preferred_element_type=F32)


def _dot_nt(a, b):
    return lax.dot_general(a.astype(BF16), b.astype(BF16), (((1,), (1,)), ((), ())),
                           preferred_element_type=F32)


def _split3(x):
    x1 = x.astype(BF16)
    r1 = x - x1.astype(F32)
    x2 = r1.astype(BF16)
    x3 = (r1 - x2.astype(F32)).astype(BF16)
    return x1, x2, x3


def _split2(x):
    x1 = x.astype(BF16)
    return x1, (x - x1.astype(F32)).astype(BF16)


def _bmm(a, b):
    return lax.dot_general(a.astype(BF16), b.astype(BF16), (((2,), (1,)), ((0,), (0,))),
                           preferred_element_type=F32)


def _bmm_nt(a, b):
    return lax.dot_general(a.astype(BF16), b.astype(BF16), (((2,), (2,)), ((0,), (0,))),
                           preferred_element_type=F32)


def _bmm_tn(a, b):
    return lax.dot_general(a.astype(BF16), b.astype(BF16), (((1,), (1,)), ((0,), (0,))),
                           preferred_element_type=F32)


def _dot_exact_lhs(a_bf, b):
    b1, b2, b3 = _split3(b)
    d = functools.partial(jnp.dot, preferred_element_type=F32)
    return d(a_bf, b1) + d(a_bf, b2) + d(a_bf, b3)


def _sigmoid(x):
    return 1.0 / (1.0 + jnp.exp(-x))


def _silu(x):
    return x * _sigmoid(x)


def _softplus(x):
    return jnp.maximum(x, 0.0) + jnp.log(1.0 + jnp.exp(-jnp.abs(x)))


def _rms(x, g):
    return x * lax.rsqrt(jnp.mean(x * x, axis=-1, keepdims=True) + EPS) * g


def _layer_norm(x, g, b):
    xc = x - jnp.mean(x, axis=-1, keepdims=True)
    return xc * lax.rsqrt(jnp.mean(xc * xc, axis=-1, keepdims=True) + EPS) * g + b


def _l2n(x):
    return x * lax.rsqrt(jnp.sum(x * x, axis=-1, keepdims=True) + EPS)


def _gate_params(tail, alog, dtb):
    lane = lax.broadcasted_iota(jnp.int32, tail.shape, 1)
    beta = _sigmoid(tail)
    g = -jnp.exp(alog) * _softplus(tail + dtb)
    return jnp.where(lane < DN_HEADS, beta, g)


def _softmax_lanes(s):
    m = jnp.max(s, axis=-1, keepdims=True)
    e = jnp.exp(s - m)
    return e / jnp.sum(e, axis=-1, keepdims=True)


def _mem_kv_kernel(x_ref, wk_ref, wv_ref, k_ref, v_ref, kb_ref, vb_ref):
    x = x_ref[...]
    k = _dot(x, wk_ref[0])
    v = _dot(x, wv_ref[0])
    halves = XA_HEAD_DIM // LANES
    for hd in range(XA_HEADS):
        for half in range(halves):
            lo = hd * XA_HEAD_DIM + half * LANES
            k_ref[0, :, half * XA_HEADS + hd, :] = k[:, lo:lo + LANES]
            v_ref[0, :, half * XA_HEADS + hd, :] = v[:, lo:lo + LANES]
    kb_ref[0] = k.astype(BF16)
    vb_ref[0] = v.astype(BF16)


def _mem_kv(mem2d, wk, wv):
    rows = mem2d.shape[0]
    tile = 512
    sub = XA_HEADS * XA_HEAD_DIM // LANES
    f32_out = jax.ShapeDtypeStruct((DEPTH, rows, sub, LANES), F32)
    bf_out = jax.ShapeDtypeStruct((DEPTH, rows, D_MODEL), BF16)
    w_spec = pl.BlockSpec((1, D_MODEL, D_MODEL), lambda l, i: (l, 0, 0))
    o_spec = pl.BlockSpec((1, tile, D_MODEL), lambda l, i: (l, i, 0))
    f_spec = pl.BlockSpec((1, tile, sub, LANES), lambda l, i: (l, i, 0, 0))
    return pl.pallas_call(
        _mem_kv_kernel,
        grid=(DEPTH, rows // tile),
        in_specs=[pl.BlockSpec((tile, D_MODEL), lambda l, i: (i, 0)), w_spec, w_spec],
        out_specs=[f_spec, f_spec, o_spec, o_spec],
        out_shape=[f32_out, f32_out, bf_out, bf_out],
        compiler_params=_params(2),
        name="p_mem_kv",
    )(mem2d, wk, wv)


def _causal_conv(buf_ref, w_ref, n_taps, hist, r0, width):
    rb = CONV_ROW_BLOCK
    off = hist - (n_taps - 1)
    sublane = lax.broadcasted_iota(jnp.int32, (1, SUBLANES, 1), 1)
    cols = []
    for c0 in range(0, width, CONV_COL_BLOCK):
        cs = slice(c0, c0 + CONV_COL_BLOCK)
        total = None
        for res in range(min(SUBLANES, n_taps)):
            base = (off + res) // SUBLANES * SUBLANES
            shift = off + res - base
            span = rb + (SUBLANES if shift else 0)
            part = None
            for kk in range(res, n_taps, SUBLANES):
                lo = r0 + base + kk - res
                rows = buf_ref[lo:lo + span, cs].reshape(span // SUBLANES, SUBLANES, CONV_COL_BLOCK)
                term = rows * w_ref[kk * SUBLANES:(kk + 1) * SUBLANES, cs][None]
                part = term if part is None else part + term
            if shift:
                turned = pltpu.roll(part, SUBLANES - shift, 1)
                part = jnp.where(sublane < SUBLANES - shift, turned[:-1], turned[1:])
            total = part if total is None else total + part
        cols.append(total.reshape(rb, CONV_COL_BLOCK))
    return jnp.concatenate(cols, axis=1)


def _zero_after(v):
    bits = lax.bitcast_convert_type(v, jnp.uint32)
    gone = lax.shift_right_logical(lax.shift_right_logical(bits, jnp.uint32(16)), jnp.uint32(16))
    return lax.bitcast_convert_type(gone, F32)


def _even_front_kernel(x_ref, g_ref, w_ref, wt_ref, dww_ref, dwb_ref, lng_ref, lnb_ref, scw_ref, alog_ref, dtb_ref,
                       aout_ref, z_ref, gb_ref, u_ref, wq_ref, kd_ref, qk_ref, cst_ref, qst_ref,
                       cbuf, qbuf, qkv, gate_a):
    tt = PROMPT_TILE
    j = pl.program_id(1)
    last = pl.num_programs(1) - 1

    @pl.when(j == 0)
    def _():
        cbuf[:, 0:CONV_HIST, :] = jnp.zeros((FRONT_ROWS, CONV_HIST, CONV_CH), F32)
        qbuf[:, 0:SC_HIST, :] = jnp.zeros((FRONT_ROWS, SC_HIST, QKV_CH), F32)

    @pl.when(j > 0)
    def _():
        cbuf[:, 0:CONV_HIST, :] = cbuf[:, tt:tt + CONV_HIST, :]
        qbuf[:, 0:SC_HIST, :] = qbuf[:, tt:tt + SC_HIST, :]

    def project(r, gain):
        h = _rms(x_ref[r], gain).astype(BF16)
        cbuf[r, CONV_HIST:CONV_HIST + tt, :] = (_dot(h, w_ref[:, COL_GLU_VAL:COL_GLU_VAL + CONV_CH])
                                                * _sigmoid(_dot(h, w_ref[:, COL_GLU_GATE:COL_GLU_GATE + CONV_CH])))
        gate_a[r] = _silu(_dot(h, w_ref[:, COL_GATE_A:COL_GATE_A + CONV_CH]))
        qbuf[r, SC_HIST:SC_HIST + tt, :] = _dot(h, w_ref[:, COL_QKV:COL_QKV + QKV_CH])
        gb_ref[r] = _gate_params(_dot(h, wt_ref[...]), alog_ref[...], dtb_ref[...])
        z = _dot(h, w_ref[:, COL_Z:COL_Z + DN_WIDTH])
        z_ref[r] = z
        return z[0:1, :]

    def finish(r):
        for r0 in range(0, tt, CONV_ROW_BLOCK):
            rows = slice(r0, r0 + CONV_ROW_BLOCK)
            acc = _silu(_causal_conv(qbuf.at[r], scw_ref, SC_K, SC_HIST, r0, QKV_CH))
            for hd in range(DN_HEADS):
                lo = hd * DN_HEAD_DIM
                qkv[r, rows, lo:lo + DN_HEAD_DIM] = _l2n(acc[:, lo:lo + DN_HEAD_DIM]) * (DN_HEAD_DIM ** -0.5)
                qkv[r, rows, DN_WIDTH + lo:DN_WIDTH + lo + DN_HEAD_DIM] = _l2n(
                    acc[:, DN_WIDTH + lo:DN_WIDTH + lo + DN_HEAD_DIM])
            qkv[r, rows, 2 * DN_WIDTH:] = acc[:, 2 * DN_WIDTH:]

        def conv_rows(r0):
            acc = _causal_conv(cbuf.at[r], dww_ref, CONV_K, CONV_HIST, r0, CONV_CH)
            c = _silu(_layer_norm(acc + dwb_ref[...], lng_ref[...], lnb_ref[...]))
            aout_ref[r, r0:r0 + CONV_ROW_BLOCK, :] = (c * gate_a[r, r0:r0 + CONV_ROW_BLOCK, :]).astype(BF16)

        _delta_prep_tile(qkv[r, :, 0:DN_WIDTH], qkv[r, :, DN_WIDTH:2 * DN_WIDTH], qkv[r, :, 2 * DN_WIDTH:], gb_ref[r],
                         u_ref.at[r], wq_ref.at[r], kd_ref.at[r], qk_ref.at[r],
                         fillers=[functools.partial(conv_rows, r0) for r0 in range(0, tt, CONV_ROW_BLOCK)])

    gain = g_ref[...]
    for r in range(FRONT_ROWS):
        anchor = project(r, gain)
        gain = g_ref[...] + jnp.concatenate([_zero_after(anchor)] * (D_MODEL // DN_WIDTH), axis=1)
    for r in range(FRONT_ROWS):
        finish(r)

    @pl.when(j == last)
    def _():
        cst_ref[0] = cbuf[:, CONV_HIST + tt - (CONV_K - 1):CONV_HIST + tt, :]
        qst_ref[0] = qbuf[:, SC_HIST + tt - (SC_K - 1):SC_HIST + tt, :]


def _even_front(x, g, w, wt, dww, dwb, lng, lnb, scw, alog, dtb):
    b, t, _ = x.shape
    tt = PROMPT_TILE
    c = DN_CHUNK
    nr = FRONT_ROWS
    const = lambda shape: pl.BlockSpec(shape, lambda i, j: (0,) * len(shape))
    tile = lambda ch: pl.BlockSpec((nr, tt, ch), lambda i, j: (i, j, 0))
    return pl.pallas_call(
        _even_front_kernel,
        grid=(b // nr, t // tt),
        in_specs=[tile(D_MODEL), const((1, D_MODEL)), const((D_MODEL, EVEN_IN)), const((D_MODEL, LANES)),
                  const((CONV_K * SUBLANES, CONV_CH)), const((1, CONV_CH)), const((1, CONV_CH)), const((1, CONV_CH)),
                  const((SC_K * SUBLANES, QKV_CH)), const((1, LANES)), const((1, LANES))],
        out_specs=[tile(CONV_CH), tile(DN_WIDTH), tile(LANES), tile(DN_WIDTH),
                   pl.BlockSpec((nr, tt // c, 2 * c, DN_WIDTH), lambda i, j: (i, j, 0, 0)),
                   tile(DN_WIDTH),
                   pl.BlockSpec((nr, DN_HEADS, tt, c), lambda i, j: (i, 0, j, 0)),
                   pl.BlockSpec((1, nr, CONV_K - 1, CONV_CH), lambda i, j: (0, i, 0, 0)),
                   pl.BlockSpec((1, nr, SC_K - 1, QKV_CH), lambda i, j: (0, i, 0, 0))],
        out_shape=[jax.ShapeDtypeStruct((b, t, CONV_CH), BF16),
                   jax.ShapeDtypeStruct((b, t, DN_WIDTH), F32),
                   jax.ShapeDtypeStruct((b, t, LANES), F32),
                   jax.ShapeDtypeStruct((b, t, DN_WIDTH), F32),
                   jax.ShapeDtypeStruct((b, t // c, 2 * c, DN_WIDTH), BF16),
                   jax.ShapeDtypeStruct((b, t, DN_WIDTH), BF16),
                   jax.ShapeDtypeStruct((b, DN_HEADS, t, c), BF16),
                   jax.ShapeDtypeStruct((1, b, CONV_K - 1, CONV_CH), F32),
                   jax.ShapeDtypeStruct((1, b, SC_K - 1, QKV_CH), F32)],
        scratch_shapes=[pltpu.VMEM((nr, CONV_HIST + tt, CONV_CH), F32),
                        pltpu.VMEM((nr, SC_HIST + tt, QKV_CH), F32),
                        pltpu.VMEM((nr, tt, QKV_CH), F32),
                        pltpu.VMEM((nr, tt, CONV_CH), F32)],
        compiler_params=_params(2),
        name="p_even_front",
    )(x, g, w, wt, dww, dwb, lng, lnb, scw, alog, dtb)


def _delta_prep_tile(q2, k2, v2, gb, u_ref, wq_ref, kd_ref, qk_ref, fillers=()):
    fillers = iter(fillers)
    tt = PROMPT_TILE
    c = DN_CHUNK
    dh = DN_HEAD_DIM
    nch = tt // c
    row = lax.broadcasted_iota(jnp.int32, (c, c), 0)
    col = lax.broadcasted_iota(jnp.int32, (c, c), 1)
    incl = (row >= col)[None]
    strict = (row > col)[None]
    eye = jnp.where(row == col, 1.0, 0.0)[None]
    trow = lax.broadcasted_iota(jnp.int32, (tt, tt), 0)
    tcol = lax.broadcasted_iota(jnp.int32, (tt, tt), 1)
    same_chunk = jnp.right_shift(trow, 6) == jnp.right_shift(tcol, 6)
    tri = jnp.where((trow >= tcol) & same_chunk, 1.0, 0.0).astype(BF16)

    def lane_rep(col0):
        return jnp.concatenate([jnp.broadcast_to(gb[:, col0 + hd:col0 + hd + 1], (tt, dh))
                                for hd in range(DN_HEADS)], axis=1)

    def chunks(x):
        return jnp.concatenate([x[:, hd * dh:(hd + 1) * dh].reshape(nch, c, dh) for hd in range(DN_HEADS)], axis=0)

    gc4 = _dot_exact_lhs(tri, lane_rep(DN_HEADS))
    gc = chunks(gc4)
    gc_rows = []
    for hd in range(DN_HEADS):
        gc_t = gc4[:, hd * dh:(hd + 1) * dh].T
        gc_rows += [gc_t[0:c, n * c:(n + 1) * c][None] for n in range(nch)]
    gc_row = jnp.concatenate(gc_rows, axis=0)
    beta = chunks(lane_rep(0))
    q = chunks(q2)
    k = chunks(k2)
    v = chunks(v2)
    egc = jnp.exp(gc)
    kb = k * beta
    decay = jnp.where(incl, jnp.exp(jnp.where(incl, gc[:, :, 0:c] - gc_row, 0.0)), 0.0)
    a = jnp.where(strict, _bmm_nt(kb, k) * decay, 0.0)
    rhs = jnp.concatenate([v * beta, kb * egc], axis=2)
    p = -a
    t_inv = eye + p
    for _ in range(5):
        p = _bmm(p, p)
        t_inv = t_inv + _bmm(p, t_inv)
        next(fillers, lambda: None)()
    x0 = _bmm(t_inv, rhs).astype(BF16).astype(F32)
    a_hi, a_lo = _split2(a)
    rho = rhs - x0 - (_bmm(a_hi, x0) + _bmm(a_lo, x0))
    x = x0 + _bmm(t_inv, rho)
    k_dec = k * jnp.exp(gc[:, c - 1:c, :] - gc)
    q_dec = q * egc
    qk = jnp.where(incl, _bmm_nt(q, k) * decay, 0.0)
    for hd in range(DN_HEADS):
        lo = hd * dh
        hs = slice(hd * nch, (hd + 1) * nch)
        u_ref[:, lo:lo + dh] = x[hs, :, 0:dh].reshape(tt, dh)
        wq_ref[:, 0:c, lo:lo + dh] = x[hs, :, dh:].astype(BF16)
        wq_ref[:, c:2 * c, lo:lo + dh] = q_dec[hs].astype(BF16)
        kd_ref[:, lo:lo + dh] = k_dec[hs].reshape(tt, dh).astype(BF16)
        qk_ref[hd] = qk[hs].reshape(tt, c).astype(BF16)
    for rest in fillers:
        rest()


def _delta_scan_kernel(u_ref, wq_ref, kd_ref, qk_ref, gb_ref, z_ref, dng_ref, bout_ref, sout_ref, s_ref):
    c = DN_CHUNK
    dh = DN_HEAD_DIM
    nb = u_ref.shape[0]
    n = pl.program_id(0)
    last = pl.num_programs(0) - 1

    @pl.when(n == 0)
    def _():
        s_ref[...] = jnp.zeros(s_ref.shape, F32)

    def heads(ref):
        return jnp.concatenate([ref[:, :, hd * dh:(hd + 1) * dh] for hd in range(DN_HEADS)], axis=0)

    g_tot = jnp.sum(gb_ref[...], axis=1, keepdims=True)
    g_last = jnp.exp(jnp.concatenate(
        [jnp.broadcast_to(g_tot[:, :, DN_HEADS + hd:DN_HEADS + hd + 1], (nb, 1, dh)) for hd in range(DN_HEADS)],
        axis=0))
    wq = jnp.concatenate([wq_ref[:, 0, :, hd * dh:(hd + 1) * dh] for hd in range(DN_HEADS)], axis=0)
    qk = jnp.concatenate([qk_ref[:, hd] for hd in range(DN_HEADS)], axis=0)
    s = s_ref[...]
    ws = _bmm(wq, s)
    v_new = heads(u_ref) - ws[:, 0:c]
    o = ws[:, c:] + _bmm(qk, v_new)
    s_ref[...] = s * g_last + _bmm_tn(heads(kd_ref), v_new)
    out = (_rms(o, dng_ref[...]) * _silu(heads(z_ref))).astype(BF16)
    for hd in range(DN_HEADS):
        bout_ref[:, :, hd * dh:(hd + 1) * dh] = out[hd * nb:(hd + 1) * nb]

    @pl.when(n == last)
    def _():
        for hd in range(DN_HEADS):
            sout_ref[0, :, hd] = s_ref[hd * nb:(hd + 1) * nb]


def _delta_scan(u, wq, kd, qk, gb, z, dng):
    b, t, _ = u.shape
    c = DN_CHUNK
    rows = lambda ch: pl.BlockSpec((b, c, ch), lambda n: (0, n, 0))
    return pl.pallas_call(
        _delta_scan_kernel,
        grid=(t // c,),
        in_specs=[rows(DN_WIDTH),
                  pl.BlockSpec((b, 1, 2 * c, DN_WIDTH), lambda n: (0, n, 0, 0)),
                  rows(DN_WIDTH),
                  pl.BlockSpec((b, DN_HEADS, c, c), lambda n: (0, 0, n, 0)),
                  rows(LANES), rows(DN_WIDTH),
                  pl.BlockSpec((1, DN_HEAD_DIM), lambda n: (0, 0))],
        out_specs=[rows(DN_WIDTH),
                   pl.BlockSpec((1, b, DN_HEADS, DN_HEAD_DIM, DN_HEAD_DIM), lambda n: (0, 0, 0, 0, 0))],
        out_shape=[jax.ShapeDtypeStruct((b, t, DN_WIDTH), BF16),
                   jax.ShapeDtypeStruct((1, b, DN_HEADS, DN_HEAD_DIM, DN_HEAD_DIM), F32)],
        scratch_shapes=[pltpu.VMEM((DN_HEADS * b, DN_HEAD_DIM, DN_HEAD_DIM), F32)],
        compiler_params=_params(1),
        name="p_delta_scan",
    )(u, wq, kd, qk, gb, z, dng)


def _row_streams(tt):
    rows = tt // POST_STREAMS
    return [slice(i * rows, (i + 1) * rows) for i in range(POST_STREAMS)]


def _split_head_dim(x):
    lead = x.shape[:-2]
    halves = XA_HEAD_DIM // LANES
    x = x.reshape(lead + (XA_HEADS, halves, LANES))
    x = jnp.swapaxes(x, -3, -2)
    return x.reshape(lead + (halves * XA_HEADS, LANES))


def _merge_head_dim(x):
    lead = x.shape[:-2]
    halves = XA_HEAD_DIM // LANES
    x = jnp.swapaxes(x.reshape(lead + (halves, XA_HEADS, LANES)), -3, -2)
    return x.reshape(lead + (XA_HEADS, XA_HEAD_DIM))


def _sample_xattn_rows(q_ref, mk_ref, mv_ref, o_ref):
    def row(i):
        prod = mk_ref[0, i] * q_ref[i]
        prod = prod + pltpu.roll(prod, XA_HEADS, 1)
        s = jnp.sum(prod, axis=-1, keepdims=True) * (XA_HEAD_DIM ** -0.5)
        e = jnp.exp(s - jnp.max(s, axis=0, keepdims=True))
        o_ref[i] = jnp.sum(e * mv_ref[0, i], axis=0) / jnp.sum(e, axis=0)
    return [functools.partial(row, i) for i in range(XATTN_ROW_BLOCK)]


def _xattn_streams(x1s, g, wq_ref, wo_ref, mk_ref, mv_ref, fillers=()):
    fillers = iter(fillers)
    qs = [jnp.dot(_rms(x1, g).astype(BF16), wq_ref[0], preferred_element_type=F32) for x1 in x1s]
    next(fillers, lambda: None)()
    outs = [[] for _ in x1s]
    for hd in range(XA_HEADS):
        lo = hd * XA_HEAD_DIM
        ss = [_dot_nt(q[:, lo:lo + XA_HEAD_DIM], mk_ref[0, :, lo:lo + XA_HEAD_DIM]) * (XA_HEAD_DIM ** -0.5)
              for q in qs]
        for out, sc in zip(outs, ss):
            out.append(_dot(_softmax_lanes(sc), mv_ref[0, :, lo:lo + XA_HEAD_DIM]).astype(BF16))
    next(fillers, lambda: None)()
    x2s = [x1 + jnp.dot(jnp.concatenate(out, axis=1), wo_ref[0], preferred_element_type=F32)
           for x1, out in zip(x1s, outs)]
    for rest in fillers:
        rest()
    return x2s


def _even_post_kernel(a_ref, b_ref, x_ref, wout_ref, gx_ref, wq_ref, wo_ref, mk_ref, mv_ref, sq_ref, smk_ref, smv_ref,
                      o_ref, so_ref):
    streams = _row_streams(POST_TILE)
    fillers = _sample_xattn_rows(sq_ref, smk_ref, smv_ref, so_ref)
    x1s = [x_ref[0, rs, :] + jnp.dot(jnp.concatenate([a_ref[0, rs, :], b_ref[0, rs, :]], axis=1), wout_ref[...],
                                     preferred_element_type=F32) for rs in streams]
    fillers.pop(0)()
    for rs, x2 in zip(streams, _xattn_streams(x1s, gx_ref[...], wq_ref, wo_ref, mk_ref, mv_ref, fillers)):
        o_ref[0, rs, :] = x2


def _layer_specs(layer, n_tiles):
    sub = XA_HEADS * XA_HEAD_DIM // LANES
    step = lambda i, j: i * n_tiles + j
    weight = pl.BlockSpec((1, D_MODEL, D_MODEL), lambda i, j: (layer, 0, 0))
    mem = pl.BlockSpec((1, N_MEM, D_MODEL), lambda i, j: (layer, i, 0))
    srows = pl.BlockSpec((XATTN_ROW_BLOCK, sub, LANES), lambda i, j: (step(i, j), 0, 0))
    scache = pl.BlockSpec((1, XATTN_ROW_BLOCK, N_MEM, sub, LANES), lambda i, j: (layer, step(i, j), 0, 0, 0))
    return weight, mem, srows, scache


def _even_post(a, bo, x, wout, gx, wq, wo, mk, mv, layer, sq, smk, smv):
    b, t, _ = x.shape
    tt = POST_TILE
    ns = sq.shape[0]
    assert ns == b * (t // tt) * XATTN_ROW_BLOCK
    const = lambda shape: pl.BlockSpec(shape, lambda i, j: (0,) * len(shape))
    tile = lambda c: pl.BlockSpec((1, tt, c), lambda i, j: (i, j, 0))
    weight, mem, srows, scache = _layer_specs(layer, t // tt)
    return pl.pallas_call(
        _even_post_kernel,
        grid=(b, t // tt),
        in_specs=[tile(CONV_CH), tile(DN_WIDTH), tile(D_MODEL), const((D_MODEL, D_MODEL)), const((1, D_MODEL)),
                  weight, weight, mem, mem, srows, scache, scache],
        out_specs=[tile(D_MODEL), srows],
        out_shape=[jax.ShapeDtypeStruct((b, t, D_MODEL), F32), jax.ShapeDtypeStruct(sq.shape, F32)],
        compiler_params=_params(2),
        name="p_even_post",
    )(a, bo, x, wout, gx, wq, wo, mk, mv, sq, smk, smv)


def _pool_group_linear(pooled, wp_ref, bp_ref):
    outs = []
    for gi in range(len(POOL_WINDOWS)):
        lo = gi * POOL_GROUP
        outs.append(_dot(pooled[:, lo:lo + POOL_GROUP], wp_ref[gi]) + bp_ref[gi:gi + 1, :])
    return jnp.concatenate(outs, axis=1)


def _odd_kernel(x_ref, g_ref, win_ref, wp_ref, bp_ref, sc_ref, wout_ref, gx_ref, wq_ref, wo_ref, mk_ref, mv_ref,
                gf_ref, sq_ref, smk_ref, smv_ref, y_ref, pst_ref, so_ref, pbuf):
    tt = POST_TILE
    j = pl.program_id(1)
    last = pl.num_programs(1) - 1
    streams = _row_streams(tt)
    fillers = _sample_xattn_rows(sq_ref, smk_ref, smv_ref, so_ref)

    @pl.when(j == 0)
    def _():
        pbuf[0:POOL_HIST, :] = jnp.zeros((POOL_HIST, D_MODEL), F32)

    @pl.when(j > 0)
    def _():
        pbuf[0:POOL_HIST, :] = pbuf[tt:tt + POOL_HIST, :]

    xs = [x_ref[0, rs, :] for rs in streams]
    us, gates = [], []
    for x, rs in zip(xs, streams):
        h = _rms(x, g_ref[...]).astype(BF16)
        u = jnp.dot(h, win_ref[:, 0:D_MODEL], preferred_element_type=F32)
        pbuf[POOL_HIST + rs.start:POOL_HIST + rs.stop, :] = u
        us.append(u)
        gates.append(_silu(jnp.dot(h, win_ref[:, D_MODEL:], preferred_element_type=F32)))
    fillers.pop(0)()

    x1s = []
    for x, u, gate, rs in zip(xs, us, gates, streams):
        pos = j * tt + rs.start + lax.broadcasted_iota(jnp.int32, (rs.stop - rs.start, 1), 0)
        acc = pbuf[rs.start:POOL_HIST + rs.stop, :]
        means = []
        for gi, win in enumerate(POOL_WINDOWS):
            acc = acc + pltpu.roll(acc, win // 2, 0)
            cnt = jnp.minimum(pos + 1, win).astype(F32)
            means.append(acc[POOL_HIST:, 0:POOL_GROUP] / cnt)
            acc = acc[:, POOL_GROUP:]
        pooled = jnp.concatenate(means, axis=1) - u
        z = _pool_group_linear(pooled, wp_ref, bp_ref) * sc_ref[...] * gate
        x1s.append(x + jnp.dot(z.astype(BF16), wout_ref[...], preferred_element_type=F32))
    for rs, x2 in zip(streams, _xattn_streams(x1s, gx_ref[...], wq_ref, wo_ref, mk_ref, mv_ref, fillers)):
        y_ref[0, rs, :] = _rms(x2, gf_ref[...])

    @pl.when(j == last)
    def _():
        pst_ref[0, 0] = pbuf[POOL_HIST + tt - POOL_BUF:POOL_HIST + tt, :]


def _odd(x, g, win, wp, bp, sc, wout, gx, wq, wo, mk, mv, gf, layer, sq, smk, smv):
    b, t, _ = x.shape
    tt = POST_TILE
    ns = sq.shape[0]
    assert ns == b * (t // tt) * XATTN_ROW_BLOCK
    const = lambda shape: pl.BlockSpec(shape, lambda i, j: (0,) * len(shape))
    tile = lambda c: pl.BlockSpec((1, tt, c), lambda i, j: (i, j, 0))
    weight, mem, srows, scache = _layer_specs(layer, t // tt)
    ngrp = len(POOL_WINDOWS)
    return pl.pallas_call(
        _odd_kernel,
        grid=(b, t // tt),
        in_specs=[tile(D_MODEL), const((1, D_MODEL)), const((D_MODEL, 2 * D_MODEL)),
                  const((ngrp, POOL_GROUP, POOL_GROUP)), const((ngrp, POOL_GROUP)), const((1, D_MODEL)),
                  const((D_MODEL, D_MODEL)), const((1, D_MODEL)), weight, weight, mem, mem, const((1, D_MODEL)),
                  srows, scache, scache],
        out_specs=[tile(D_MODEL), pl.BlockSpec((1, 1, POOL_BUF, D_MODEL), lambda i, j: (0, i, 0, 0)), srows],
        out_shape=[jax.ShapeDtypeStruct((b, t, D_MODEL), F32),
                   jax.ShapeDtypeStruct((1, b, POOL_BUF, D_MODEL), F32),
                   jax.ShapeDtypeStruct(sq.shape, F32)],
        scratch_shapes=[pltpu.VMEM((POOL_HIST + tt, D_MODEL), F32)],
        compiler_params=_params(2),
        name="p_odd_layer",
    )(x, g, win, wp, bp, sc, wout, gx, wq, wo, mk, mv, gf, sq, smk, smv)


def _push_row(old_ref, new_ref, row):
    depth = old_ref.shape[0]
    new_ref[0:depth - 1] = old_ref[1:depth]
    new_ref[depth - 1] = row


def _history_major(state):
    return jnp.transpose(state[0], (1, 0, 2))


def _s_even_pre_kernel(x_ref, g_ref, w_ref, wt_ref, dww_ref, dwb_ref, lng_ref, lnb_ref, scw_ref, alog_ref, dtb_ref,
                       cin_ref, qin_ref,
                       aout_ref, q_ref, k_ref, v_ref, gb_ref, z_ref, cout_ref, qout_ref):
    h = _rms(x_ref[...], g_ref[...]).astype(BF16)
    glu = (_dot(h, w_ref[:, COL_GLU_VAL:COL_GLU_VAL + CONV_CH])
           * _sigmoid(_dot(h, w_ref[:, COL_GLU_GATE:COL_GLU_GATE + CONV_CH])))
    nb = CONV_K - 1
    acc = dww_ref[nb:nb + 1, :] * glu
    for kk in range(nb):
        acc = acc + dww_ref[kk:kk + 1, :] * cin_ref[kk]
    _push_row(cin_ref, cout_ref, glu)
    c = _silu(_layer_norm(acc + dwb_ref[...], lng_ref[...], lnb_ref[...]))
    aout_ref[...] = c * _silu(_dot(h, w_ref[:, COL_GATE_A:COL_GATE_A + CONV_CH]))

    qkv = _dot(h, w_ref[:, COL_QKV:COL_QKV + QKV_CH])
    ns = SC_K - 1
    acc = scw_ref[ns:ns + 1, :] * qkv
    for kk in range(ns):
        acc = acc + scw_ref[kk:kk + 1, :] * qin_ref[kk]
    _push_row(qin_ref, qout_ref, qkv)
    acc = _silu(acc)
    for hd in range(DN_HEADS):
        lo = hd * DN_HEAD_DIM
        q_ref[:, lo:lo + DN_HEAD_DIM] = _l2n(acc[:, lo:lo + DN_HEAD_DIM]) * (DN_HEAD_DIM ** -0.5)
        k_ref[:, lo:lo + DN_HEAD_DIM] = _l2n(acc[:, DN_WIDTH + lo:DN_WIDTH + lo + DN_HEAD_DIM])
    v_ref[...] = acc[:, 2 * DN_WIDTH:]
    z_ref[...] = _dot(h, w_ref[:, COL_Z:COL_Z + DN_WIDTH])
    gb_ref[...] = _gate_params(_dot(h, wt_ref[...]), alog_ref[...], dtb_ref[...])


def _s_even_pre(x, g, w, wt, dww, dwb, lng, lnb, scw, alog, dtb, cin, qin):
    n = x.shape[0]
    rbk = SAMPLE_ROW_BLOCK
    const = lambda shape: pl.BlockSpec(shape, lambda i: (0,) * len(shape))
    rows = lambda c: pl.BlockSpec((rbk, c), lambda i: (i, 0))
    hist = lambda arr: pl.BlockSpec((arr.shape[0], rbk, arr.shape[2]), lambda i: (0, i, 0))
    return pl.pallas_call(
        _s_even_pre_kernel,
        grid=(n // rbk,),
        in_specs=[rows(D_MODEL), const((1, D_MODEL)), const((D_MODEL, EVEN_IN)), const((D_MODEL, LANES)),
                  const((CONV_K, CONV_CH)),
                  const((1, CONV_CH)), const((1, CONV_CH)), const((1, CONV_CH)), const((SC_K, QKV_CH)),
                  const((1, LANES)), const((1, LANES)), hist(cin), hist(qin)],
        out_specs=[rows(CONV_CH), rows(DN_WIDTH), rows(DN_WIDTH), rows(DN_WIDTH), rows(LANES), rows(DN_WIDTH),
                   hist(cin), hist(qin)],
        out_shape=[jax.ShapeDtypeStruct((n, CONV_CH), F32),
                   jax.ShapeDtypeStruct((n, DN_WIDTH), F32),
                   jax.ShapeDtypeStruct((n, DN_WIDTH), F32),
                   jax.ShapeDtypeStruct((n, DN_WIDTH), F32),
                   jax.ShapeDtypeStruct((n, LANES), F32),
                   jax.ShapeDtypeStruct((n, DN_WIDTH), F32),
                   jax.ShapeDtypeStruct(cin.shape, F32),
                   jax.ShapeDtypeStruct(qin.shape, F32)],
        compiler_params=_params(1),
        name="s_even_pre",
    )(x, g, w, wt, dww, dwb, lng, lnb, scw, alog, dtb, cin, qin)


def _s_delta_kernel(q_ref, k_ref, v_ref, gb_ref, z_ref, dng_ref, sin_ref, bout_ref, sout_ref):
    dh = DN_HEAD_DIM
    chains = [(i, hd) for i in range(DELTA_ROW_BLOCK) for hd in range(DN_HEADS)]
    vec = lambda ref, i, hd: ref[i:i + 1, hd * dh:(hd + 1) * dh]
    col = lambda ref, i, hd: jnp.broadcast_to(vec(ref, i, hd), (dh, dh)).T
    kcol = [col(k_ref, i, hd) for i, hd in chains]
    qcol = [col(q_ref, i, hd) for i, hd in chains]
    s = [sin_ref[0, i, hd] * jnp.exp(gb_ref[i:i + 1, DN_HEADS + hd:DN_HEADS + hd + 1]) for i, hd in chains]
    v_new = [(vec(v_ref, i, hd) - jnp.sum(kc * sc, axis=0, keepdims=True)) * gb_ref[i:i + 1, hd:hd + 1]
             for (i, hd), kc, sc in zip(chains, kcol, s)]
    s = [sc + kc * vn for sc, kc, vn in zip(s, kcol, v_new)]
    for (i, hd), sc, qc in zip(chains, s, qcol):
        sout_ref[0, i, hd] = sc
        o = jnp.sum(qc * sc, axis=0, keepdims=True)
        bout_ref[i:i + 1, hd * dh:(hd + 1) * dh] = _rms(o, dng_ref[...]) * _silu(vec(z_ref, i, hd))


def _s_delta(q, k, v, gb, z, dng, s_in):
    n = q.shape[0]
    rbk = DELTA_ROW_BLOCK
    rows = lambda c: pl.BlockSpec((rbk, c), lambda i: (i, 0))
    st = pl.BlockSpec((1, rbk, DN_HEADS, DN_HEAD_DIM, DN_HEAD_DIM), lambda i: (0, i, 0, 0, 0))
    return pl.pallas_call(
        _s_delta_kernel,
        grid=(n // rbk,),
        in_specs=[rows(DN_WIDTH), rows(DN_WIDTH), rows(DN_WIDTH), rows(LANES), rows(DN_WIDTH),
                  pl.BlockSpec((1, DN_HEAD_DIM), lambda i: (0, 0)), st],
        out_specs=[rows(DN_WIDTH), st],
        out_shape=[jax.ShapeDtypeStruct((n, DN_WIDTH), F32),
                   jax.ShapeDtypeStruct(s_in.shape, F32)],
        compiler_params=_params(1),
        name="s_delta",
    )(q, k, v, gb, z, dng, s_in)


def _s_mix_out_kernel(a_ref, b_ref, x_ref, wout_ref, gx_ref, wq_ref, x1_ref, q_ref):
    mix = jnp.concatenate([a_ref[...], b_ref[...]], axis=1).astype(BF16)
    x1 = x_ref[...] + jnp.dot(mix, wout_ref[...], preferred_element_type=F32)
    x1_ref[...] = x1
    q_ref[...] = jnp.dot(_rms(x1, gx_ref[...]).astype(BF16), wq_ref[0], preferred_element_type=F32)


def _s_mix_out(a, bo, x, wout, gx, wq, layer):
    n = x.shape[0]
    full = lambda arr: pl.BlockSpec(arr.shape, lambda i: (0,) * arr.ndim)
    args = (a, bo, x, wout, gx, wq)
    out = jax.ShapeDtypeStruct((n, D_MODEL), F32)
    return pl.pallas_call(
        _s_mix_out_kernel,
        grid=(1,),
        in_specs=[full(v) for v in args[:-1]] + [pl.BlockSpec((1, D_MODEL, D_MODEL), lambda i: (layer, 0, 0))],
        out_specs=[pl.BlockSpec((n, D_MODEL), lambda i: (0, 0))] * 2,
        out_shape=[out, out],
        compiler_params=_params(1),
        name="s_mix_out",
    )(*args)


def _s_odd_kernel(x1_ref, o_ref, wo_ref, g_ref, win_ref, wp_ref, bp_ref, sc_ref, wout_ref, gx_ref, wq_ref, pin_ref,
                  x1o_ref, q_ref, pout_ref):
    x = x1_ref[...] + _dot(o_ref[...], wo_ref[0])
    h = _rms(x, g_ref[...]).astype(BF16)
    u = jnp.dot(h, win_ref[:, 0:D_MODEL], preferred_element_type=F32)
    gate = _silu(jnp.dot(h, win_ref[:, D_MODEL:], preferred_element_type=F32))
    means = []
    for gi, win in enumerate(POOL_WINDOWS):
        lo = gi * POOL_GROUP
        acc = u[:, lo:lo + POOL_GROUP]
        for d in range(1, win):
            acc = acc + pin_ref[POOL_BUF - d, :, lo:lo + POOL_GROUP]
        means.append(acc / float(min(PAST_LEN + 1, win)))
    pooled = jnp.concatenate(means, axis=1) - u
    _push_row(pin_ref, pout_ref, u)
    z = _pool_group_linear(pooled, wp_ref, bp_ref) * sc_ref[...] * gate
    x1 = x + jnp.dot(z.astype(BF16), wout_ref[...], preferred_element_type=F32)
    x1o_ref[...] = x1
    q_ref[...] = jnp.dot(_rms(x1, gx_ref[...]).astype(BF16), wq_ref[0], preferred_element_type=F32)


def _s_odd(x1, o, wo, g, win, wp, bp, sc, wout, gx, wq, pin, layer):
    n = x1.shape[0]
    rbk = SAMPLE_ROW_BLOCK
    ngrp = len(POOL_WINDOWS)
    const = lambda shape: pl.BlockSpec(shape, lambda i: (0,) * len(shape))
    rows = lambda c: pl.BlockSpec((rbk, c), lambda i: (i, 0))
    hist = pl.BlockSpec((pin.shape[0], rbk, pin.shape[2]), lambda i: (0, i, 0))
    out = jax.ShapeDtypeStruct((n, D_MODEL), F32)
    return pl.pallas_call(
        _s_odd_kernel,
        grid=(n // rbk,),
        in_specs=[rows(D_MODEL), rows(D_MODEL), pl.BlockSpec((1, D_MODEL, D_MODEL), lambda i: (layer - 1, 0, 0)),
                  const((1, D_MODEL)),
                  const((D_MODEL, 2 * D_MODEL)), const((ngrp, POOL_GROUP, POOL_GROUP)), const((ngrp, POOL_GROUP)),
                  const((1, D_MODEL)), const((D_MODEL, D_MODEL)), const((1, D_MODEL)),
                  pl.BlockSpec((1, D_MODEL, D_MODEL), lambda i: (layer, 0, 0)),
                  hist],
        out_specs=[rows(D_MODEL), rows(D_MODEL), hist],
        out_shape=[out, out, jax.ShapeDtypeStruct(pin.shape, F32)],
        compiler_params=_params(1),
        name="s_odd",
    )(x1, o, wo, g, win, wp, bp, sc, wout, gx, wq, pin)


def _s_final_kernel(x1_ref, o_ref, wo_ref, gf_ref, y_ref):
    y_ref[...] = _rms(x1_ref[...] + _dot(o_ref[...], wo_ref[0]), gf_ref[...])


def _s_final(x1, o, wo, gf, layer):
    n = x1.shape[0]
    full = lambda arr: pl.BlockSpec(arr.shape, lambda i: (0,) * arr.ndim)
    args = (x1, o, wo, gf)
    return pl.pallas_call(
        _s_final_kernel,
        grid=(1,),
        in_specs=[full(x1), full(o), pl.BlockSpec((1, D_MODEL, D_MODEL), lambda i: (layer, 0, 0)), full(gf)],
        out_specs=pl.BlockSpec((n, D_MODEL), lambda i: (0, 0)),
        out_shape=jax.ShapeDtypeStruct((n, D_MODEL), F32),
        compiler_params=_params(1),
        name="s_final",
    )(*args)


def _lane_pad(vec, offset):
    return jnp.pad(vec.astype(F32), (offset, LANES - offset - vec.shape[0])).reshape(1, LANES)


def kernel(x_prompt, x_sample, state_conv_a, state_qkv_conv, state_delta, state_pool, cache_mem_k, cache_mem_v, mem_prompt, norm_mix, norm_xattn, norm_final, w_in_even, w_out_even, dw_w, dw_b, ln_a_g, ln_a_b, sc_w, a_log, dt_bias, dn_norm_g, w_in_odd, w_pool, b_pool, pool_scale, w_out_odd, w_xq, w_xk, w_xv, w_xo):
    bp, t, _ = x_prompt.shape
    ns = x_sample.shape[0]
    row = lambda v: v.reshape(1, -1)

    w_in0 = w_in_even[0].astype(BF16)
    w_tail0 = jnp.pad(w_in_even[0][:, COL_TAIL:], ((0, 0), (0, LANES - (EVEN_IN - COL_TAIL)))).astype(BF16)
    w_out0 = w_out_even[0].astype(BF16)
    w_in1 = w_in_odd[0].astype(BF16)
    w_pool1 = w_pool[0].astype(BF16)
    w_out1 = w_out_odd[0].astype(BF16)
    wq = w_xq.astype(BF16)
    wk = w_xk.astype(BF16)
    wv = w_xv.astype(BF16)
    wo = w_xo.astype(BF16)
    alog = _lane_pad(a_log[0], DN_HEADS)
    dtb = _lane_pad(dt_bias[0], DN_HEADS)
    even_small = (dw_w[0], row(dw_b[0]), row(ln_a_g[0]), row(ln_a_b[0]), sc_w[0], alog, dtb)
    dng = row(dn_norm_g[0])

    mk_f, mv_f, mk_b, mv_b = _mem_kv(mem_prompt.reshape(bp * N_MEM, D_MODEL), wk, wv)
    new_mem_k_p = _merge_head_dim(mk_f.reshape(DEPTH, bp, N_MEM, -1, LANES))
    new_mem_v_p = _merge_head_dim(mv_f.reshape(DEPTH, bp, N_MEM, -1, LANES))

    even_small_p = (jnp.repeat(dw_w[0], SUBLANES, axis=0),) + even_small[1:4] + (
        jnp.repeat(sc_w[0], SUBLANES, axis=0),) + even_small[5:]
    a_out, z, gb, u, wqd, kd, qk, new_conv_a_p, new_qkv_conv_p = _even_front(x_prompt, row(norm_mix[0]), w_in0,
                                                                             w_tail0, *even_small_p)
    b_out, new_delta_p = _delta_scan(u, wqd, kd, qk, gb, z, dng)

    xs = x_sample.reshape(ns, D_MODEL)
    sa, sq, sk, sv, sgb, sz, cout, qout = _s_even_pre(xs, row(norm_mix[0]), w_in0, w_tail0, *even_small,
                                                      _history_major(state_conv_a), _history_major(state_qkv_conv))
    sb, new_delta_s = _s_delta(sq, sk, sv, sgb, sz, dng, state_delta)
    x1, xq = _s_mix_out(sa, sb, xs, w_out0, row(norm_xattn[0]), wq, 0)

    heads = lambda v: _split_head_dim(v.reshape(ns, XA_HEADS, XA_HEAD_DIM))
    unheads = lambda v: _merge_head_dim(v).reshape(ns, D_MODEL)
    cmk, cmv = _split_head_dim(cache_mem_k), _split_head_dim(cache_mem_v)
    x2, o0 = _even_post(a_out, b_out, x_prompt, w_out0, row(norm_xattn[0]), wq, wo, mk_b, mv_b, 0, heads(xq), cmk, cmv)
    x1, xq, pout = _s_odd(x1, unheads(o0), wo, row(norm_mix[1]), w_in1, w_pool1, b_pool[0], row(pool_scale[0]), w_out1,
                          row(norm_xattn[1]), wq, _history_major(state_pool), 1)
    y_prompt, new_pool_p, o1 = _odd(x2, row(norm_mix[1]), w_in1, w_pool1, b_pool[0], row(pool_scale[0]), w_out1,
                                    row(norm_xattn[1]), wq, wo, mk_b, mv_b, row(norm_final), 1, heads(xq), cmk, cmv)
    y_sample = _s_final(x1, unheads(o1), wo, row(norm_final), 1).reshape(ns, 1, D_MODEL)

    new_conv_a_s, new_qkv_conv_s, new_pool_s = (jnp.transpose(st, (1, 0, 2))[None] for st in (cout, qout, pout))
    return (y_prompt, y_sample, new_conv_a_p, new_qkv_conv_p, new_delta_p, new_pool_p, new_mem_k_p,
            new_mem_v_p, new_conv_a_s, new_qkv_conv_s, new_delta_s, new_pool_s)
```

```python
import functools

import jax
import jax.numpy as jnp
from jax import lax
from jax.experimental import pallas as pl
from jax.experimental.pallas import tpu as pltpu

F32 = jnp.float32
BF16 = jnp.bfloat16

D_MODEL = 1024
DEPTH = 2
PAST_LEN = 16384
CONV_CH = 512
CONV_K = 31
DN_HEAD_DIM = 128
DN_HEADS = 4
DN_WIDTH = 512
QKV_CH = 1536
SC_K = 4
DN_CHUNK = 64
EVEN_IN = 3592
POOL_WINDOWS = (2, 4, 8, 16)
POOL_GROUP = 256
POOL_BUF = 15
N_MEM = 256
XA_HEADS = 4
XA_HEAD_DIM = 256
XA_SCALE = XA_HEAD_DIM ** -0.5
EPS = 1e-6

LANES = 128
SUBLANES = 8
VMEM_LIMIT_BYTES = 56 * 1024 * 1024

COL_GLU_VAL = 0
COL_GLU_GATE = 512
COL_GATE_A = 1024
COL_QKV = 1536
COL_Z = 3072
COL_TAIL = 3584

PROMPT_TILE = 256
FRONT_ROWS = 2
SCAN_CHUNKS = 4
POST_TILE = 512
POST_STREAMS = 1
CONV_HIST = 32
SC_HIST = 8
POOL_HIST = 16
CONV_ROW_BLOCK = 64
CONV_COL_BLOCK = 256
SAMPLE_ROW_BLOCK = 32
DELTA_ROW_BLOCK = 8
XATTN_ROW_BLOCK = 4


def _params(n_axes):
    return pltpu.CompilerParams(dimension_semantics=("arbitrary",) * n_axes,
                                vmem_limit_bytes=VMEM_LIMIT_BYTES)


def _dot(a, b):
    return jnp.dot(a.astype(BF16), b.astype(BF16), preferred_element_type=F32)


def _dot_nt(a, b):
    return lax.dot_general(a.astype(BF16), b.astype(BF16), (((1,), (1,)), ((), ())),
                           preferred_element_type=F32)


def _split3(x):
    x1 = x.astype(BF16)
    r1 = x - x1.astype(F32)
    x2 = r1.astype(BF16)
    x3 = (r1 - x2.astype(F32)).astype(BF16)
    return x1, x2, x3


def _split2(x):
    x1 = x.astype(BF16)
    return x1, (x - x1.astype(F32)).astype(BF16)


def _bmm(a, b):
    return lax.dot_general(a.astype(BF16), b.astype(BF16), (((2,), (1,)), ((0,), (0,))),
                           preferred_element_type=F32)


def _bmm_nt(a, b):
    return lax.dot_general(a.astype(BF16), b.astype(BF16), (((2,), (2,)), ((0,), (0,))),
                           preferred_element_type=F32)


def _bmm_tn(a, b):
    return lax.dot_general(a.astype(BF16), b.astype(BF16), (((1,), (1,)), ((0,), (0,))),
                           preferred_element_type=F32)


def _dot_exact_lhs(a_bf, b):
    b1, b2, b3 = _split3(b)
    d = functools.partial(jnp.dot, preferred_element_type=F32)
    return d(a_bf, b1) + d(a_bf, b2) + d(a_bf, b3)


def _sigmoid(x):
    return 1.0 / (1.0 + jnp.exp(-x))


def _silu(x):
    return x * _sigmoid(x)


def _softplus(x):
    return jnp.maximum(x, 0.0) + jnp.log(1.0 + jnp.exp(-jnp.abs(x)))


def _rms(x, g):
    return x * lax.rsqrt(jnp.mean(x * x, axis=-1, keepdims=True) + EPS) * g


def _layer_norm(x, g, b):
    xc = x - jnp.mean(x, axis=-1, keepdims=True)
    return xc * lax.rsqrt(jnp.mean(xc * xc, axis=-1, keepdims=True) + EPS) * g + b


def _l2n(x, scale=1.0):
    return x * (lax.rsqrt(jnp.sum(x * x, axis=-1, keepdims=True) + EPS) * scale)


def _gate_params(tail, alog, dtb):
    lane = lax.broadcasted_iota(jnp.int32, tail.shape, 1)
    beta = _sigmoid(tail)
    g = -jnp.exp(alog) * _softplus(tail + dtb)
    return jnp.where(lane < DN_HEADS, beta, g)


def _softmax_lanes(s):
    m = jnp.max(s, axis=-1, keepdims=True)
    e = jnp.exp(s - m)
    return e / jnp.sum(e, axis=-1, keepdims=True)


def _mem_kv_kernel(x_ref, wk_ref, wv_ref, k_ref, v_ref, kb_ref, vb_ref):
    x = x_ref[...]
    k = _dot(x, wk_ref[0])
    v = _dot(x, wv_ref[0])
    halves = XA_HEAD_DIM // LANES
    for hd in range(XA_HEADS):
        for half in range(halves):
            lo = hd * XA_HEAD_DIM + half * LANES
            k_ref[0, :, half * XA_HEADS + hd, :] = k[:, lo:lo + LANES]
            v_ref[0, :, half * XA_HEADS + hd, :] = v[:, lo:lo + LANES]
    kb_ref[0] = k.astype(BF16)
    vb_ref[0] = v.astype(BF16)


def _mem_kv(mem2d, wk, wv):
    rows = mem2d.shape[0]
    tile = 512
    sub = XA_HEADS * XA_HEAD_DIM // LANES
    f32_out = jax.ShapeDtypeStruct((DEPTH, rows, sub, LANES), F32)
    bf_out = jax.ShapeDtypeStruct((DEPTH, rows, D_MODEL), BF16)
    w_spec = pl.BlockSpec((1, D_MODEL, D_MODEL), lambda l, i: (l, 0, 0))
    o_spec = pl.BlockSpec((1, tile, D_MODEL), lambda l, i: (l, i, 0))
    f_spec = pl.BlockSpec((1, tile, sub, LANES), lambda l, i: (l, i, 0, 0))
    return pl.pallas_call(
        _mem_kv_kernel,
        grid=(DEPTH, rows // tile),
        in_specs=[pl.BlockSpec((tile, D_MODEL), lambda l, i: (i, 0)), w_spec, w_spec],
        out_specs=[f_spec, f_spec, o_spec, o_spec],
        out_shape=[f32_out, f32_out, bf_out, bf_out],
        compiler_params=_params(2),
        name="p_mem_kv",
    )(mem2d, wk, wv)


def _causal_conv(buf_ref, w_ref, n_taps, hist, r0, width):
    rb = CONV_ROW_BLOCK
    off = hist - (n_taps - 1)
    sublane = lax.broadcasted_iota(jnp.int32, (1, SUBLANES, 1), 1)
    cols = []
    for c0 in range(0, width, CONV_COL_BLOCK):
        cs = slice(c0, c0 + CONV_COL_BLOCK)
        total = None
        for res in range(min(SUBLANES, n_taps)):
            base = (off + res) // SUBLANES * SUBLANES
            shift = off + res - base
            span = rb + (SUBLANES if shift else 0)
            part = None
            for kk in range(res, n_taps, SUBLANES):
                lo = r0 + base + kk - res
                rows = buf_ref[lo:lo + span, cs].reshape(span // SUBLANES, SUBLANES, CONV_COL_BLOCK)
                term = rows * w_ref[kk * SUBLANES:(kk + 1) * SUBLANES, cs][None]
                part = term if part is None else part + term
            if shift:
                turned = pltpu.roll(part, SUBLANES - shift, 1)
                part = jnp.where(sublane < SUBLANES - shift, turned[:-1], turned[1:])
            total = part if total is None else total + part
        cols.append(total.reshape(rb, CONV_COL_BLOCK))
    return jnp.concatenate(cols, axis=1)


def _zero_after(v):
    bits = lax.bitcast_convert_type(v, jnp.uint32)
    gone = lax.shift_right_logical(lax.shift_right_logical(bits, jnp.uint32(16)), jnp.uint32(16))
    return lax.bitcast_convert_type(gone, F32)


def _even_front_kernel(x_ref, g_ref, w_ref, wt_ref, dww_ref, dwb_ref, lng_ref, lnb_ref, scw_ref, alog_ref, dtb_ref,
                       aout_ref, z_ref, gb_ref, u_ref, wq_ref, kd_ref, qk_ref, cst_ref, qst_ref,
                       cbuf, qbuf, qkv, gate_a):
    tt = PROMPT_TILE
    j = pl.program_id(1)
    last = pl.num_programs(1) - 1

    @pl.when(j == 0)
    def _():
        cbuf[:, 0:CONV_HIST, :] = jnp.zeros((FRONT_ROWS, CONV_HIST, CONV_CH), F32)
        qbuf[:, 0:SC_HIST, :] = jnp.zeros((FRONT_ROWS, SC_HIST, QKV_CH), F32)

    @pl.when(j > 0)
    def _():
        cbuf[:, 0:CONV_HIST, :] = cbuf[:, tt:tt + CONV_HIST, :]
        qbuf[:, 0:SC_HIST, :] = qbuf[:, tt:tt + SC_HIST, :]

    def project(r, gain):
        h = _rms(x_ref[r], gain).astype(BF16)
        cbuf[r, CONV_HIST:CONV_HIST + tt, :] = (_dot(h, w_ref[:, COL_GLU_VAL:COL_GLU_VAL + CONV_CH])
                                                * _sigmoid(_dot(h, w_ref[:, COL_GLU_GATE:COL_GLU_GATE + CONV_CH])))
        gate_a[r] = _silu(_dot(h, w_ref[:, COL_GATE_A:COL_GATE_A + CONV_CH]))
        qbuf[r, SC_HIST:SC_HIST + tt, :] = _dot(h, w_ref[:, COL_QKV:COL_QKV + QKV_CH])
        gb_ref[r] = _gate_params(_dot(h, wt_ref[...]), alog_ref[...], dtb_ref[...])
        z = _dot(h, w_ref[:, COL_Z:COL_Z + DN_WIDTH])
        z_ref[r] = z
        return z[0:1, :]

    def finish(r):
        for r0 in range(0, tt, CONV_ROW_BLOCK):
            rows = slice(r0, r0 + CONV_ROW_BLOCK)
            acc = _silu(_causal_conv(qbuf.at[r], scw_ref, SC_K, SC_HIST, r0, QKV_CH))
            for hd in range(DN_HEADS):
                lo = hd * DN_HEAD_DIM
                qkv[r, rows, lo:lo + DN_HEAD_DIM] = _l2n(acc[:, lo:lo + DN_HEAD_DIM], DN_HEAD_DIM ** -0.5)
                qkv[r, rows, DN_WIDTH + lo:DN_WIDTH + lo + DN_HEAD_DIM] = _l2n(
                    acc[:, DN_WIDTH + lo:DN_WIDTH + lo + DN_HEAD_DIM])
            qkv[r, rows, 2 * DN_WIDTH:] = acc[:, 2 * DN_WIDTH:]

        def conv_rows(r0):
            acc = _causal_conv(cbuf.at[r], dww_ref, CONV_K, CONV_HIST, r0, CONV_CH)
            c = _silu(_layer_norm(acc + dwb_ref[...], lng_ref[...], lnb_ref[...]))
            aout_ref[r, r0:r0 + CONV_ROW_BLOCK, :] = (c * gate_a[r, r0:r0 + CONV_ROW_BLOCK, :]).astype(BF16)

        _delta_prep_tile(qkv[r, :, 0:DN_WIDTH], qkv[r, :, DN_WIDTH:2 * DN_WIDTH], qkv[r, :, 2 * DN_WIDTH:], gb_ref[r],
                         u_ref.at[r], wq_ref.at[r], kd_ref.at[r], qk_ref.at[r],
                         fillers=[functools.partial(conv_rows, r0) for r0 in range(0, tt, CONV_ROW_BLOCK)])

    gain = g_ref[...]
    for r in range(FRONT_ROWS):
        anchor = project(r, gain)
        gain = g_ref[...] + jnp.concatenate([_zero_after(anchor)] * (D_MODEL // DN_WIDTH), axis=1)
    for r in range(FRONT_ROWS):
        finish(r)

    @pl.when(j == last)
    def _():
        cst_ref[0] = cbuf[:, CONV_HIST + tt - (CONV_K - 1):CONV_HIST + tt, :]
        qst_ref[0] = qbuf[:, SC_HIST + tt - (SC_K - 1):SC_HIST + tt, :]


def _even_front(x, g, w, wt, dww, dwb, lng, lnb, scw, alog, dtb):
    b, t, _ = x.shape
    tt = PROMPT_TILE
    c = DN_CHUNK
    nr = FRONT_ROWS
    const = lambda shape: pl.BlockSpec(shape, lambda i, j: (0,) * len(shape))
    tile = lambda ch: pl.BlockSpec((nr, tt, ch), lambda i, j: (i, j, 0))
    return pl.pallas_call(
        _even_front_kernel,
        grid=(b // nr, t // tt),
        in_specs=[tile(D_MODEL), const((1, D_MODEL)), const((D_MODEL, EVEN_IN)), const((D_MODEL, LANES)),
                  const((CONV_K * SUBLANES, CONV_CH)), const((1, CONV_CH)), const((1, CONV_CH)), const((1, CONV_CH)),
                  const((SC_K * SUBLANES, QKV_CH)), const((1, LANES)), const((1, LANES))],
        out_specs=[tile(CONV_CH), tile(DN_WIDTH), tile(LANES), tile(DN_WIDTH),
                   pl.BlockSpec((nr, tt // c, 2 * c, DN_WIDTH), lambda i, j: (i, j, 0, 0)),
                   tile(DN_WIDTH),
                   pl.BlockSpec((nr, DN_HEADS, tt, c), lambda i, j: (i, 0, j, 0)),
                   pl.BlockSpec((1, nr, CONV_K - 1, CONV_CH), lambda i, j: (0, i, 0, 0)),
                   pl.BlockSpec((1, nr, SC_K - 1, QKV_CH), lambda i, j: (0, i, 0, 0))],
        out_shape=[jax.ShapeDtypeStruct((b, t, CONV_CH), BF16),
                   jax.ShapeDtypeStruct((b, t, DN_WIDTH), F32),
                   jax.ShapeDtypeStruct((b, t, LANES), F32),
                   jax.ShapeDtypeStruct((b, t, DN_WIDTH), F32),
                   jax.ShapeDtypeStruct((b, t // c, 2 * c, DN_WIDTH), BF16),
                   jax.ShapeDtypeStruct((b, t, DN_WIDTH), BF16),
                   jax.ShapeDtypeStruct((b, DN_HEADS, t, c), BF16),
                   jax.ShapeDtypeStruct((1, b, CONV_K - 1, CONV_CH), F32),
                   jax.ShapeDtypeStruct((1, b, SC_K - 1, QKV_CH), F32)],
        scratch_shapes=[pltpu.VMEM((nr, CONV_HIST + tt, CONV_CH), F32),
                        pltpu.VMEM((nr, SC_HIST + tt, QKV_CH), F32),
                        pltpu.VMEM((nr, tt, QKV_CH), F32),
                        pltpu.VMEM((nr, tt, CONV_CH), F32)],
        compiler_params=_params(2),
        name="p_even_front",
    )(x, g, w, wt, dww, dwb, lng, lnb, scw, alog, dtb)


def _delta_prep_tile(q2, k2, v2, gb, u_ref, wq_ref, kd_ref, qk_ref, fillers=()):
    fillers = iter(fillers)
    tt = PROMPT_TILE
    c = DN_CHUNK
    dh = DN_HEAD_DIM
    nch = tt // c
    row = lax.broadcasted_iota(jnp.int32, (c, c), 0)
    col = lax.broadcasted_iota(jnp.int32, (c, c), 1)
    incl = (row >= col)[None]
    strict = (row > col)[None]
    eye = jnp.where(row == col, 1.0, 0.0)[None]
    trow = lax.broadcasted_iota(jnp.int32, (tt, tt), 0)
    tcol = lax.broadcasted_iota(jnp.int32, (tt, tt), 1)
    same_chunk = jnp.right_shift(trow, 6) == jnp.right_shift(tcol, 6)
    tri = jnp.where((trow >= tcol) & same_chunk, 1.0, 0.0).astype(BF16)

    def lane_rep(col0):
        return jnp.concatenate([jnp.broadcast_to(gb[:, col0 + hd:col0 + hd + 1], (tt, dh))
                                for hd in range(DN_HEADS)], axis=1)

    def chunks(x):
        return jnp.concatenate([x[:, hd * dh:(hd + 1) * dh].reshape(nch, c, dh) for hd in range(DN_HEADS)], axis=0)

    gc4 = _dot_exact_lhs(tri, lane_rep(DN_HEADS))
    gc = chunks(gc4)
    gc_rows = []
    for hd in range(DN_HEADS):
        gc_t = gc4[:, hd * dh:(hd + 1) * dh].T
        gc_rows += [gc_t[0:c, n * c:(n + 1) * c][None] for n in range(nch)]
    gc_row = jnp.concatenate(gc_rows, axis=0)
    beta = chunks(lane_rep(0))
    q = chunks(q2)
    k = chunks(k2)
    v = chunks(v2)
    egc = jnp.exp(gc)
    kb = k * beta
    decay = jnp.where(incl, jnp.exp(jnp.where(incl, gc[:, :, 0:c] - gc_row, 0.0)), 0.0)
    a = jnp.where(strict, _bmm_nt(kb, k) * decay, 0.0)
    rhs = jnp.concatenate([v * beta, kb * egc], axis=2)
    p = -a
    t_inv = eye + p
    for _ in range(5):
        p = _bmm(p, p)
        t_inv = t_inv + _bmm(p, t_inv)
        next(fillers, lambda: None)()
    x0 = _bmm(t_inv, rhs).astype(BF16).astype(F32)
    a_hi, a_lo = _split2(a)
    rho = rhs - x0 - (_bmm(a_hi, x0) + _bmm(a_lo, x0))
    x = x0 + _bmm(t_inv, rho)
    k_dec = k * jnp.exp(gc[:, c - 1:c, :] - gc)
    q_dec = q * egc
    qk = jnp.where(incl, _bmm_nt(q, k) * decay, 0.0)
    for hd in range(DN_HEADS):
        lo = hd * dh
        hs = slice(hd * nch, (hd + 1) * nch)
        u_ref[:, lo:lo + dh] = x[hs, :, 0:dh].reshape(tt, dh)
        wq_ref[:, 0:c, lo:lo + dh] = x[hs, :, dh:].astype(BF16)
        wq_ref[:, c:2 * c, lo:lo + dh] = q_dec[hs].astype(BF16)
        kd_ref[:, lo:lo + dh] = k_dec[hs].reshape(tt, dh).astype(BF16)
        qk_ref[hd] = qk[hs].reshape(tt, c).astype(BF16)
    for rest in fillers:
        rest()


def _delta_scan_kernel(u_ref, wq_ref, kd_ref, qk_ref, gb_ref, z_ref, dng_ref, bout_ref, sout_ref, s_ref):
    c = DN_CHUNK
    dh = DN_HEAD_DIM
    nb = u_ref.shape[0]
    n = pl.program_id(0)
    last = pl.num_programs(0) - 1

    @pl.when(n == 0)
    def _():
        s_ref[...] = jnp.zeros(s_ref.shape, F32)

    def heads(ref, rows):
        return jnp.concatenate([ref[:, rows, hd * dh:(hd + 1) * dh] for hd in range(DN_HEADS)], axis=0)

    s = s_ref[...]
    for ci in range(SCAN_CHUNKS):
        rows = slice(ci * c, (ci + 1) * c)
        g_tot = jnp.sum(gb_ref[:, rows, :], axis=1, keepdims=True)
        g_last = jnp.exp(jnp.concatenate(
            [jnp.broadcast_to(g_tot[:, :, DN_HEADS + hd:DN_HEADS + hd + 1], (nb, 1, dh)) for hd in range(DN_HEADS)],
            axis=0))
        wq = jnp.concatenate([wq_ref[:, ci, :, hd * dh:(hd + 1) * dh] for hd in range(DN_HEADS)], axis=0)
        qk = jnp.concatenate([qk_ref[:, hd, rows, :] for hd in range(DN_HEADS)], axis=0)
        ws = _bmm(wq, s)
        v_new = heads(u_ref, rows) - ws[:, 0:c]
        o = ws[:, c:] + _bmm(qk, v_new)
        s = s * g_last + _bmm_tn(heads(kd_ref, rows), v_new)
        out = (_rms(o, dng_ref[...]) * _silu(heads(z_ref, rows))).astype(BF16)
        for hd in range(DN_HEADS):
            bout_ref[:, rows, hd * dh:(hd + 1) * dh] = out[hd * nb:(hd + 1) * nb]
    s_ref[...] = s

    @pl.when(n == last)
    def _():
        for hd in range(DN_HEADS):
            sout_ref[0, :, hd] = s_ref[hd * nb:(hd + 1) * nb]


def _delta_scan(u, wq, kd, qk, gb, z, dng):
    b, t, _ = u.shape
    c = DN_CHUNK
    span = SCAN_CHUNKS * c
    rows = lambda ch: pl.BlockSpec((b, span, ch), lambda n: (0, n, 0))
    return pl.pallas_call(
        _delta_scan_kernel,
        grid=(t // span,),
        in_specs=[rows(DN_WIDTH),
                  pl.BlockSpec((b, SCAN_CHUNKS, 2 * c, DN_WIDTH), lambda n: (0, n, 0, 0)),
                  rows(DN_WIDTH),
                  pl.BlockSpec((b, DN_HEADS, span, c), lambda n: (0, 0, n, 0)),
                  rows(LANES), rows(DN_WIDTH),
                  pl.BlockSpec((1, DN_HEAD_DIM), lambda n: (0, 0))],
        out_specs=[rows(DN_WIDTH),
                   pl.BlockSpec((1, b, DN_HEADS, DN_HEAD_DIM, DN_HEAD_DIM), lambda n: (0, 0, 0, 0, 0))],
        out_shape=[jax.ShapeDtypeStruct((b, t, DN_WIDTH), BF16),
                   jax.ShapeDtypeStruct((1, b, DN_HEADS, DN_HEAD_DIM, DN_HEAD_DIM), F32)],
        scratch_shapes=[pltpu.VMEM((DN_HEADS * b, DN_HEAD_DIM, DN_HEAD_DIM), F32)],
        compiler_params=_params(1),
        name="p_delta_scan",
    )(u, wq, kd, qk, gb, z, dng)


def _row_streams(tt):
    rows = tt // POST_STREAMS
    return [slice(i * rows, (i + 1) * rows) for i in range(POST_STREAMS)]


def _split_head_dim(x):
    lead = x.shape[:-2]
    halves = XA_HEAD_DIM // LANES
    x = x.reshape(lead + (XA_HEADS, halves, LANES))
    x = jnp.swapaxes(x, -3, -2)
    return x.reshape(lead + (halves * XA_HEADS, LANES))


def _merge_head_dim(x):
    lead = x.shape[:-2]
    halves = XA_HEAD_DIM // LANES
    x = jnp.swapaxes(x.reshape(lead + (halves, XA_HEADS, LANES)), -3, -2)
    return x.reshape(lead + (XA_HEADS, XA_HEAD_DIM))


def _sample_xattn_rows(q_ref, mk_ref, mv_ref, o_ref):
    def row(i):
        prod = mk_ref[0, i] * q_ref[i]
        prod = prod + pltpu.roll(prod, XA_HEADS, 1)
        s = jnp.sum(prod, axis=-1, keepdims=True)
        e = jnp.exp(s - jnp.max(s, axis=0, keepdims=True))
        o_ref[i] = jnp.sum(e * mv_ref[0, i], axis=0) / jnp.sum(e, axis=0)
    return [functools.partial(row, i) for i in range(XATTN_ROW_BLOCK)]


def _xattn_streams(x1s, g, wq_ref, wo_ref, mk_ref, mv_ref, fillers=()):
    fillers = iter(fillers)
    qs = [jnp.dot(_rms(x1, g).astype(BF16), wq_ref[0], preferred_element_type=F32) for x1 in x1s]
    next(fillers, lambda: None)()
    outs = [[] for _ in x1s]
    for hd in range(XA_HEADS):
        lo = hd * XA_HEAD_DIM
        ss = [_dot_nt(q[:, lo:lo + XA_HEAD_DIM], mk_ref[0, :, lo:lo + XA_HEAD_DIM]) * XA_SCALE
              for q in qs]
        for out, sc in zip(outs, ss):
            out.append(_dot(_softmax_lanes(sc), mv_ref[0, :, lo:lo + XA_HEAD_DIM]).astype(BF16))
    next(fillers, lambda: None)()
    x2s = [x1 + jnp.dot(jnp.concatenate(out, axis=1), wo_ref[0], preferred_element_type=F32)
           for x1, out in zip(x1s, outs)]
    for rest in fillers:
        rest()
    return x2s


def _even_post_kernel(a_ref, b_ref, x_ref, wout_ref, gx_ref, wq_ref, wo_ref, mk_ref, mv_ref, sq_ref, smk_ref, smv_ref,
                      o_ref, so_ref):
    streams = _row_streams(POST_TILE)
    fillers = _sample_xattn_rows(sq_ref, smk_ref, smv_ref, so_ref)
    x1s = [x_ref[0, rs, :] + jnp.dot(jnp.concatenate([a_ref[0, rs, :], b_ref[0, rs, :]], axis=1), wout_ref[...],
                                     preferred_element_type=F32) for rs in streams]
    fillers.pop(0)()
    for rs, x2 in zip(streams, _xattn_streams(x1s, gx_ref[...], wq_ref, wo_ref, mk_ref, mv_ref, fillers)):
        o_ref[0, rs, :] = x2


def _layer_specs(layer, n_tiles):
    sub = XA_HEADS * XA_HEAD_DIM // LANES
    step = lambda i, j: i * n_tiles + j
    weight = pl.BlockSpec((1, D_MODEL, D_MODEL), lambda i, j: (layer, 0, 0))
    mem = pl.BlockSpec((1, N_MEM, D_MODEL), lambda i, j: (layer, i, 0))
    srows = pl.BlockSpec((XATTN_ROW_BLOCK, sub, LANES), lambda i, j: (step(i, j), 0, 0))
    scache = pl.BlockSpec((1, XATTN_ROW_BLOCK, N_MEM, sub, LANES), lambda i, j: (layer, step(i, j), 0, 0, 0))
    return weight, mem, srows, scache


def _even_post(a, bo, x, wout, gx, wq, wo, mk, mv, layer, sq, smk, smv):
    b, t, _ = x.shape
    tt = POST_TILE
    ns = sq.shape[0]
    assert ns == b * (t // tt) * XATTN_ROW_BLOCK
    const = lambda shape: pl.BlockSpec(shape, lambda i, j: (0,) * len(shape))
    tile = lambda c: pl.BlockSpec((1, tt, c), lambda i, j: (i, j, 0))
    weight, mem, srows, scache = _layer_specs(layer, t // tt)
    return pl.pallas_call(
        _even_post_kernel,
        grid=(b, t // tt),
        in_specs=[tile(CONV_CH), tile(DN_WIDTH), tile(D_MODEL), const((D_MODEL, D_MODEL)), const((1, D_MODEL)),
                  weight, weight, mem, mem, srows, scache, scache],
        out_specs=[tile(D_MODEL), srows],
        out_shape=[jax.ShapeDtypeStruct((b, t, D_MODEL), F32), jax.ShapeDtypeStruct(sq.shape, F32)],
        compiler_params=_params(2),
        name="p_even_post",
    )(a, bo, x, wout, gx, wq, wo, mk, mv, sq, smk, smv)


def _pool_group_linear(pooled, wp_ref, bp_ref):
    outs = []
    for gi in range(len(POOL_WINDOWS)):
        lo = gi * POOL_GROUP
        outs.append(_dot(pooled[:, lo:lo + POOL_GROUP], wp_ref[gi]) + bp_ref[gi:gi + 1, :])
    return jnp.concatenate(outs, axis=1)


def _rows_down(grp, shift):
    earlier = lambda v: jnp.concatenate([v[:1], v[:-1]], axis=0)
    if shift % SUBLANES == 0:
        for _ in range(shift // SUBLANES):
            grp = earlier(grp)
        return grp
    sublane = lax.broadcasted_iota(jnp.int32, (1, SUBLANES, 1), 1)
    turned = pltpu.roll(grp, shift, 1)
    return jnp.where(sublane >= shift, turned, earlier(turned))


def _odd_kernel(x_ref, g_ref, win_ref, wp_ref, bp_ref, sc_ref, wout_ref, gx_ref, wq_ref, wo_ref, mk_ref, mv_ref,
                gf_ref, sq_ref, smk_ref, smv_ref, y_ref, pst_ref, so_ref, pbuf):
    tt = POST_TILE
    j = pl.program_id(1)
    last = pl.num_programs(1) - 1
    streams = _row_streams(tt)
    fillers = _sample_xattn_rows(sq_ref, smk_ref, smv_ref, so_ref)

    @pl.when(j == 0)
    def _():
        pbuf[0:POOL_HIST, :] = jnp.zeros((POOL_HIST, D_MODEL), F32)

    @pl.when(j > 0)
    def _():
        pbuf[0:POOL_HIST, :] = pbuf[tt:tt + POOL_HIST, :]

    xs = [x_ref[0, rs, :] for rs in streams]
    us, gates = [], []
    for x, rs in zip(xs, streams):
        h = _rms(x, g_ref[...]).astype(BF16)
        u = jnp.dot(h, win_ref[:, 0:D_MODEL], preferred_element_type=F32)
        pbuf[POOL_HIST + rs.start:POOL_HIST + rs.stop, :] = u
        us.append(u)
        gates.append(_silu(jnp.dot(h, win_ref[:, D_MODEL:], preferred_element_type=F32)))
    fillers.pop(0)()

    x1s = []
    for x, u, gate, rs in zip(xs, us, gates, streams):
        pos = j * tt + rs.start + lax.broadcasted_iota(jnp.int32, (rs.stop - rs.start, 1), 0)
        nrows = rs.stop - rs.start
        acc = pbuf[rs.start:POOL_HIST + rs.stop, :].reshape((POOL_HIST + nrows) // SUBLANES, SUBLANES, D_MODEL)
        means = []
        for gi, win in enumerate(POOL_WINDOWS):
            acc = acc + _rows_down(acc, win // 2)
            cnt = jnp.minimum(pos + 1, win).astype(F32)
            means.append(acc[POOL_HIST // SUBLANES:, :, 0:POOL_GROUP].reshape(nrows, POOL_GROUP) / cnt)
            acc = acc[:, :, POOL_GROUP:]
        pooled = jnp.concatenate(means, axis=1) - u
        z = _pool_group_linear(pooled, wp_ref, bp_ref) * sc_ref[...] * gate
        x1s.append(x + jnp.dot(z.astype(BF16), wout_ref[...], preferred_element_type=F32))
    for rs, x2 in zip(streams, _xattn_streams(x1s, gx_ref[...], wq_ref, wo_ref, mk_ref, mv_ref, fillers)):
        y_ref[0, rs, :] = _rms(x2, gf_ref[...])

    @pl.when(j == last)
    def _():
        pst_ref[0, 0] = pbuf[POOL_HIST + tt - POOL_BUF:POOL_HIST + tt, :]


def _odd(x, g, win, wp, bp, sc, wout, gx, wq, wo, mk, mv, gf, layer, sq, smk, smv):
    b, t, _ = x.shape
    tt = POST_TILE
    ns = sq.shape[0]
    assert ns == b * (t // tt) * XATTN_ROW_BLOCK
    const = lambda shape: pl.BlockSpec(shape, lambda i, j: (0,) * len(shape))
    tile = lambda c: pl.BlockSpec((1, tt, c), lambda i, j: (i, j, 0))
    weight, mem, srows, scache = _layer_specs(layer, t // tt)
    ngrp = len(POOL_WINDOWS)
    return pl.pallas_call(
        _odd_kernel,
        grid=(b, t // tt),
        in_specs=[tile(D_MODEL), const((1, D_MODEL)), const((D_MODEL, 2 * D_MODEL)),
                  const((ngrp, POOL_GROUP, POOL_GROUP)), const((ngrp, POOL_GROUP)), const((1, D_MODEL)),
                  const((D_MODEL, D_MODEL)), const((1, D_MODEL)), weight, weight, mem, mem, const((1, D_MODEL)),
                  srows, scache, scache],
        out_specs=[tile(D_MODEL), pl.BlockSpec((1, 1, POOL_BUF, D_MODEL), lambda i, j: (0, i, 0, 0)), srows],
        out_shape=[jax.ShapeDtypeStruct((b, t, D_MODEL), F32),
                   jax.ShapeDtypeStruct((1, b, POOL_BUF, D_MODEL), F32),
                   jax.ShapeDtypeStruct(sq.shape, F32)],
        scratch_shapes=[pltpu.VMEM((POOL_HIST + tt, D_MODEL), F32)],
        compiler_params=_params(2),
        name="p_odd_layer",
    )(x, g, win, wp, bp, sc, wout, gx, wq, wo, mk, mv, gf, sq, smk, smv)


def _push_row(old_ref, new_ref, row):
    depth = old_ref.shape[0]
    new_ref[0:depth - 1] = old_ref[1:depth]
    new_ref[depth - 1] = row


def _history_major(state):
    return jnp.transpose(state[0], (1, 0, 2))


def _s_even_pre_kernel(x_ref, g_ref, w_ref, wt_ref, dww_ref, dwb_ref, lng_ref, lnb_ref, scw_ref, alog_ref, dtb_ref,
                       cin_ref, qin_ref,
                       aout_ref, q_ref, k_ref, v_ref, gb_ref, z_ref, cout_ref, qout_ref):
    h = _rms(x_ref[...], g_ref[...]).astype(BF16)
    glu = (_dot(h, w_ref[:, COL_GLU_VAL:COL_GLU_VAL + CONV_CH])
           * _sigmoid(_dot(h, w_ref[:, COL_GLU_GATE:COL_GLU_GATE + CONV_CH])))
    nb = CONV_K - 1
    acc = dww_ref[nb:nb + 1, :] * glu
    for kk in range(nb):
        acc = acc + dww_ref[kk:kk + 1, :] * cin_ref[kk]
    _push_row(cin_ref, cout_ref, glu)
    c = _silu(_layer_norm(acc + dwb_ref[...], lng_ref[...], lnb_ref[...]))
    aout_ref[...] = c * _silu(_dot(h, w_ref[:, COL_GATE_A:COL_GATE_A + CONV_CH]))

    qkv = _dot(h, w_ref[:, COL_QKV:COL_QKV + QKV_CH])
    ns = SC_K - 1
    acc = scw_ref[ns:ns + 1, :] * qkv
    for kk in range(ns):
        acc = acc + scw_ref[kk:kk + 1, :] * qin_ref[kk]
    _push_row(qin_ref, qout_ref, qkv)
    acc = _silu(acc)
    for hd in range(DN_HEADS):
        lo = hd * DN_HEAD_DIM
        q_ref[:, lo:lo + DN_HEAD_DIM] = _l2n(acc[:, lo:lo + DN_HEAD_DIM], DN_HEAD_DIM ** -0.5)
        k_ref[:, lo:lo + DN_HEAD_DIM] = _l2n(acc[:, DN_WIDTH + lo:DN_WIDTH + lo + DN_HEAD_DIM])
    v_ref[...] = acc[:, 2 * DN_WIDTH:]
    z_ref[...] = _dot(h, w_ref[:, COL_Z:COL_Z + DN_WIDTH])
    gb_ref[...] = _gate_params(_dot(h, wt_ref[...]), alog_ref[...], dtb_ref[...])


def _s_even_pre(x, g, w, wt, dww, dwb, lng, lnb, scw, alog, dtb, cin, qin):
    n = x.shape[0]
    rbk = SAMPLE_ROW_BLOCK
    const = lambda shape: pl.BlockSpec(shape, lambda i: (0,) * len(shape))
    rows = lambda c: pl.BlockSpec((rbk, c), lambda i: (i, 0))
    hist = lambda arr: pl.BlockSpec((arr.shape[0], rbk, arr.shape[2]), lambda i: (0, i, 0))
    return pl.pallas_call(
        _s_even_pre_kernel,
        grid=(n // rbk,),
        in_specs=[rows(D_MODEL), const((1, D_MODEL)), const((D_MODEL, EVEN_IN)), const((D_MODEL, LANES)),
                  const((CONV_K, CONV_CH)),
                  const((1, CONV_CH)), const((1, CONV_CH)), const((1, CONV_CH)), const((SC_K, QKV_CH)),
                  const((1, LANES)), const((1, LANES)), hist(cin), hist(qin)],
        out_specs=[rows(CONV_CH), rows(DN_WIDTH), rows(DN_WIDTH), rows(DN_WIDTH), rows(LANES), rows(DN_WIDTH),
                   hist(cin), hist(qin)],
        out_shape=[jax.ShapeDtypeStruct((n, CONV_CH), F32),
                   jax.ShapeDtypeStruct((n, DN_WIDTH), F32),
                   jax.ShapeDtypeStruct((n, DN_WIDTH), F32),
                   jax.ShapeDtypeStruct((n, DN_WIDTH), F32),
                   jax.ShapeDtypeStruct((n, LANES), F32),
                   jax.ShapeDtypeStruct((n, DN_WIDTH), F32),
                   jax.ShapeDtypeStruct(cin.shape, F32),
                   jax.ShapeDtypeStruct(qin.shape, F32)],
        compiler_params=_params(1),
        name="s_even_pre",
    )(x, g, w, wt, dww, dwb, lng, lnb, scw, alog, dtb, cin, qin)


def _s_delta_kernel(q_ref, k_ref, v_ref, gb_ref, z_ref, dng_ref, sin_ref, bout_ref, sout_ref):
    dh = DN_HEAD_DIM
    chains = [(i, hd) for i in range(DELTA_ROW_BLOCK) for hd in range(DN_HEADS)]
    vec = lambda ref, i, hd: ref[i:i + 1, hd * dh:(hd + 1) * dh]
    col = lambda ref, i, hd: jnp.broadcast_to(vec(ref, i, hd), (dh, dh)).T
    kcol = [col(k_ref, i, hd) for i, hd in chains]
    qcol = [col(q_ref, i, hd) for i, hd in chains]
    s = [sin_ref[0, i, hd] * jnp.exp(gb_ref[i:i + 1, DN_HEADS + hd:DN_HEADS + hd + 1]) for i, hd in chains]
    v_new = [(vec(v_ref, i, hd) - jnp.sum(kc * sc, axis=0, keepdims=True)) * gb_ref[i:i + 1, hd:hd + 1]
             for (i, hd), kc, sc in zip(chains, kcol, s)]
    s = [sc + kc * vn for sc, kc, vn in zip(s, kcol, v_new)]
    for (i, hd), sc, qc in zip(chains, s, qcol):
        sout_ref[0, i, hd] = sc
        o = jnp.sum(qc * sc, axis=0, keepdims=True)
        bout_ref[i:i + 1, hd * dh:(hd + 1) * dh] = _rms(o, dng_ref[...]) * _silu(vec(z_ref, i, hd))


def _s_delta(q, k, v, gb, z, dng, s_in):
    n = q.shape[0]
    rbk = DELTA_ROW_BLOCK
    rows = lambda c: pl.BlockSpec((rbk, c), lambda i: (i, 0))
    st = pl.BlockSpec((1, rbk, DN_HEADS, DN_HEAD_DIM, DN_HEAD_DIM), lambda i: (0, i, 0, 0, 0))
    return pl.pallas_call(
        _s_delta_kernel,
        grid=(n // rbk,),
        in_specs=[rows(DN_WIDTH), rows(DN_WIDTH), rows(DN_WIDTH), rows(LANES), rows(DN_WIDTH),
                  pl.BlockSpec((1, DN_HEAD_DIM), lambda i: (0, 0)), st],
        out_specs=[rows(DN_WIDTH), st],
        out_shape=[jax.ShapeDtypeStruct((n, DN_WIDTH), F32),
                   jax.ShapeDtypeStruct(s_in.shape, F32)],
        compiler_params=_params(1),
        name="s_delta",
    )(q, k, v, gb, z, dng, s_in)


def _s_mix_out_kernel(a_ref, b_ref, x_ref, wout_ref, gx_ref, wq_ref, x1_ref, q_ref):
    mix = jnp.concatenate([a_ref[...], b_ref[...]], axis=1).astype(BF16)
    x1 = x_ref[...] + jnp.dot(mix, wout_ref[...], preferred_element_type=F32)
    x1_ref[...] = x1
    q_ref[...] = jnp.dot(_rms(x1, gx_ref[...]).astype(BF16), wq_ref[0], preferred_element_type=F32) * XA_SCALE


def _s_mix_out(a, bo, x, wout, gx, wq, layer):
    n = x.shape[0]
    full = lambda arr: pl.BlockSpec(arr.shape, lambda i: (0,) * arr.ndim)
    args = (a, bo, x, wout, gx, wq)
    out = jax.ShapeDtypeStruct((n, D_MODEL), F32)
    return pl.pallas_call(
        _s_mix_out_kernel,
        grid=(1,),
        in_specs=[full(v) for v in args[:-1]] + [pl.BlockSpec((1, D_MODEL, D_MODEL), lambda i: (layer, 0, 0))],
        out_specs=[pl.BlockSpec((n, D_MODEL), lambda i: (0, 0))] * 2,
        out_shape=[out, out],
        compiler_params=_params(1),
        name="s_mix_out",
    )(*args)


def _s_odd_kernel(x1_ref, o_ref, wo_ref, g_ref, win_ref, wp_ref, bp_ref, sc_ref, wout_ref, gx_ref, wq_ref, pin_ref,
                  x1o_ref, q_ref, pout_ref):
    x = x1_ref[...] + _dot(o_ref[...], wo_ref[0])
    h = _rms(x, g_ref[...]).astype(BF16)
    u = jnp.dot(h, win_ref[:, 0:D_MODEL], preferred_element_type=F32)
    gate = _silu(jnp.dot(h, win_ref[:, D_MODEL:], preferred_element_type=F32))
    means = []
    for gi, win in enumerate(POOL_WINDOWS):
        lo = gi * POOL_GROUP
        acc = u[:, lo:lo + POOL_GROUP]
        for d in range(1, win):
            acc = acc + pin_ref[POOL_BUF - d, :, lo:lo + POOL_GROUP]
        means.append(acc / float(min(PAST_LEN + 1, win)))
    pooled = jnp.concatenate(means, axis=1) - u
    _push_row(pin_ref, pout_ref, u)
    z = _pool_group_linear(pooled, wp_ref, bp_ref) * sc_ref[...] * gate
    x1 = x + jnp.dot(z.astype(BF16), wout_ref[...], preferred_element_type=F32)
    x1o_ref[...] = x1
    q_ref[...] = jnp.dot(_rms(x1, gx_ref[...]).astype(BF16), wq_ref[0], preferred_element_type=F32) * XA_SCALE


def _s_odd(x1, o, wo, g, win, wp, bp, sc, wout, gx, wq, pin, layer):
    n = x1.shape[0]
    rbk = SAMPLE_ROW_BLOCK
    ngrp = len(POOL_WINDOWS)
    const = lambda shape: pl.BlockSpec(shape, lambda i: (0,) * len(shape))
    rows = lambda c: pl.BlockSpec((rbk, c), lambda i: (i, 0))
    hist = pl.BlockSpec((pin.shape[0], rbk, pin.shape[2]), lambda i: (0, i, 0))
    out = jax.ShapeDtypeStruct((n, D_MODEL), F32)
    return pl.pallas_call(
        _s_odd_kernel,
        grid=(n // rbk,),
        in_specs=[rows(D_MODEL), rows(D_MODEL), pl.BlockSpec((1, D_MODEL, D_MODEL), lambda i: (layer - 1, 0, 0)),
                  const((1, D_MODEL)),
                  const((D_MODEL, 2 * D_MODEL)), const((ngrp, POOL_GROUP, POOL_GROUP)), const((ngrp, POOL_GROUP)),
                  const((1, D_MODEL)), const((D_MODEL, D_MODEL)), const((1, D_MODEL)),
                  pl.BlockSpec((1, D_MODEL, D_MODEL), lambda i: (layer, 0, 0)),
                  hist],
        out_specs=[rows(D_MODEL), rows(D_MODEL), hist],
        out_shape=[out, out, jax.ShapeDtypeStruct(pin.shape, F32)],
        compiler_params=_params(1),
        name="s_odd",
    )(x1, o, wo, g, win, wp, bp, sc, wout, gx, wq, pin)


def _s_final_kernel(x1_ref, o_ref, wo_ref, gf_ref, y_ref):
    y_ref[...] = _rms(x1_ref[...] + _dot(o_ref[...], wo_ref[0]), gf_ref[...])


def _s_final(x1, o, wo, gf, layer):
    n = x1.shape[0]
    full = lambda arr: pl.BlockSpec(arr.shape, lambda i: (0,) * arr.ndim)
    args = (x1, o, wo, gf)
    return pl.pallas_call(
        _s_final_kernel,
        grid=(1,),
        in_specs=[full(x1), full(o), pl.BlockSpec((1, D_MODEL, D_MODEL), lambda i: (layer, 0, 0)), full(gf)],
        out_specs=pl.BlockSpec((n, D_MODEL), lambda i: (0, 0)),
        out_shape=jax.ShapeDtypeStruct((n, D_MODEL), F32),
        compiler_params=_params(1),
        name="s_final",
    )(*args)


def _lane_pad(vec, offset):
    return jnp.pad(vec.astype(F32), (offset, LANES - offset - vec.shape[0])).reshape(1, LANES)


def kernel(x_prompt, x_sample, state_conv_a, state_qkv_conv, state_delta, state_pool, cache_mem_k, cache_mem_v, mem_prompt, norm_mix, norm_xattn, norm_final, w_in_even, w_out_even, dw_w, dw_b, ln_a_g, ln_a_b, sc_w, a_log, dt_bias, dn_norm_g, w_in_odd, w_pool, b_pool, pool_scale, w_out_odd, w_xq, w_xk, w_xv, w_xo):
    bp, t, _ = x_prompt.shape
    ns = x_sample.shape[0]
    row = lambda v: v.reshape(1, -1)

    w_in0 = w_in_even[0].astype(BF16)
    w_tail0 = jnp.pad(w_in_even[0][:, COL_TAIL:], ((0, 0), (0, LANES - (EVEN_IN - COL_TAIL)))).astype(BF16)
    w_out0 = w_out_even[0].astype(BF16)
    w_in1 = w_in_odd[0].astype(BF16)
    w_pool1 = w_pool[0].astype(BF16)
    w_out1 = w_out_odd[0].astype(BF16)
    wq = w_xq.astype(BF16)
    wk = w_xk.astype(BF16)
    wv = w_xv.astype(BF16)
    wo = w_xo.astype(BF16)
    alog = _lane_pad(a_log[0], DN_HEADS)
    dtb = _lane_pad(dt_bias[0], DN_HEADS)
    even_small = (dw_w[0], row(dw_b[0]), row(ln_a_g[0]), row(ln_a_b[0]), sc_w[0], alog, dtb)
    dng = row(dn_norm_g[0])

    mk_f, mv_f, mk_b, mv_b = _mem_kv(mem_prompt.reshape(bp * N_MEM, D_MODEL), wk, wv)
    new_mem_k_p = _merge_head_dim(mk_f.reshape(DEPTH, bp, N_MEM, -1, LANES))
    new_mem_v_p = _merge_head_dim(mv_f.reshape(DEPTH, bp, N_MEM, -1, LANES))

    even_small_p = (jnp.repeat(dw_w[0], SUBLANES, axis=0),) + even_small[1:4] + (
        jnp.repeat(sc_w[0], SUBLANES, axis=0),) + even_small[5:]
    a_out, z, gb, u, wqd, kd, qk, new_conv_a_p, new_qkv_conv_p = _even_front(x_prompt, row(norm_mix[0]), w_in0,
                                                                             w_tail0, *even_small_p)
    b_out, new_delta_p = _delta_scan(u, wqd, kd, qk, gb, z, dng)

    xs = x_sample.reshape(ns, D_MODEL)
    sa, sq, sk, sv, sgb, sz, cout, qout = _s_even_pre(xs, row(norm_mix[0]), w_in0, w_tail0, *even_small,
                                                      _history_major(state_conv_a), _history_major(state_qkv_conv))
    sb, new_delta_s = _s_delta(sq, sk, sv, sgb, sz, dng, state_delta)
    x1, xq = _s_mix_out(sa, sb, xs, w_out0, row(norm_xattn[0]), wq, 0)

    heads = lambda v: _split_head_dim(v.reshape(ns, XA_HEADS, XA_HEAD_DIM))
    unheads = lambda v: _merge_head_dim(v).reshape(ns, D_MODEL)
    cmk, cmv = _split_head_dim(cache_mem_k), _split_head_dim(cache_mem_v)
    x2, o0 = _even_post(a_out, b_out, x_prompt, w_out0, row(norm_xattn[0]), wq, wo, mk_b, mv_b, 0, heads(xq), cmk, cmv)
    x1, xq, pout = _s_odd(x1, unheads(o0), wo, row(norm_mix[1]), w_in1, w_pool1, b_pool[0], row(pool_scale[0]), w_out1,
                          row(norm_xattn[1]), wq, _history_major(state_pool), 1)
    y_prompt, new_pool_p, o1 = _odd(x2, row(norm_mix[1]), w_in1, w_pool1, b_pool[0], row(pool_scale[0]), w_out1,
                                    row(norm_xattn[1]), wq, wo, mk_b, mv_b, row(norm_final), 1, heads(xq), cmk, cmv)
    y_sample = _s_final(x1, unheads(o1), wo, row(norm_final), 1).reshape(ns, 1, D_MODEL)

    new_conv_a_s, new_qkv_conv_s, new_pool_s = (jnp.transpose(st, (1, 0, 2))[None] for st in (cout, qout, pout))
    return (y_prompt, y_sample, new_conv_a_p, new_qkv_conv_p, new_delta_p, new_pool_p, new_mem_k_p,
            new_mem_v_p, new_conv_a_s, new_qkv_conv_s, new_delta_s, new_pool_s)
```

```python
import functools

import jax
import jax.numpy as jnp
from jax import lax
from jax.experimental import pallas as pl
from jax.experimental.pallas import tpu as pltpu

F32 = jnp.float32
BF16 = jnp.bfloat16

D_MODEL = 1024
DEPTH = 2
PAST_LEN = 16384
CONV_CH = 512
CONV_K = 31
DN_HEAD_DIM = 128
DN_HEADS = 4
DN_WIDTH = 512
QKV_CH = 1536
SC_K = 4
DN_CHUNK = 64
EVEN_IN = 3592
POOL_WINDOWS = (2, 4, 8, 16)
POOL_GROUP = 256
POOL_BUF = 15
N_MEM = 256
XA_HEADS = 4
XA_HEAD_DIM = 256
XA_SCALE = XA_HEAD_DIM ** -0.5
EPS = 1e-6

LANES = 128
SUBLANES = 8
VMEM_LIMIT_BYTES = 56 * 1024 * 1024

COL_GLU_VAL = 0
COL_GLU_GATE = 512
COL_GATE_A = 1024
COL_QKV = 1536
COL_Z = 3072
COL_TAIL = 3584

PROMPT_TILE = 256
FRONT_ROWS = 2
SCAN_CHUNKS = 2
POST_TILE = 512
POST_STREAMS = 1
CONV_HIST = 32
SC_HIST = 8
POOL_HIST = 16
CONV_ROW_BLOCK = 64
CONV_COL_BLOCK = 256
SAMPLE_ROW_BLOCK = 32
XATTN_ROW_BLOCK = 4


def _params(n_axes):
    return pltpu.CompilerParams(dimension_semantics=("arbitrary",) * n_axes,
                                vmem_limit_bytes=VMEM_LIMIT_BYTES)


def _dot(a, b):
    return jnp.dot(a.astype(BF16), b.astype(BF16), preferred_element_type=F32)


def _dot_nt(a, b):
    return lax.dot_general(a.astype(BF16), b.astype(BF16), (((1,), (1,)), ((), ())),
                           preferred_element_type=F32)


def _split3(x):
    x1 = x.astype(BF16)
    r1 = x - x1.astype(F32)
    x2 = r1.astype(BF16)
    x3 = (r1 - x2.astype(F32)).astype(BF16)
    return x1, x2, x3


def _split2(x):
    x1 = x.astype(BF16)
    return x1, (x - x1.astype(F32)).astype(BF16)


def _bmm(a, b):
    return lax.dot_general(a.astype(BF16), b.astype(BF16), (((2,), (1,)), ((0,), (0,))),
                           preferred_element_type=F32)


def _bmm_nt(a, b):
    return lax.dot_general(a.astype(BF16), b.astype(BF16), (((2,), (2,)), ((0,), (0,))),
                           preferred_element_type=F32)


def _bmm_tn(a, b):
    return lax.dot_general(a.astype(BF16), b.astype(BF16), (((1,), (1,)), ((0,), (0,))),
                           preferred_element_type=F32)


def _dot_exact_lhs(a_bf, b):
    b1, b2, b3 = _split3(b)
    d = functools.partial(jnp.dot, preferred_element_type=F32)
    return d(a_bf, b1) + d(a_bf, b2) + d(a_bf, b3)


def _sigmoid(x):
    return 1.0 / (1.0 + jnp.exp(-x))


def _silu(x):
    return x * _sigmoid(x)


def _softplus(x):
    return jnp.maximum(x, 0.0) + jnp.log(1.0 + jnp.exp(-jnp.abs(x)))


def _rms(x, g):
    return x * lax.rsqrt(jnp.mean(x * x, axis=-1, keepdims=True) + EPS) * g


def _layer_norm(x, g, b):
    xc = x - jnp.mean(x, axis=-1, keepdims=True)
    return xc * lax.rsqrt(jnp.mean(xc * xc, axis=-1, keepdims=True) + EPS) * g + b


def _l2n(x, scale=1.0):
    return x * (lax.rsqrt(jnp.sum(x * x, axis=-1, keepdims=True) + EPS) * scale)


def _gate_params(tail, alog, dtb):
    lane = lax.broadcasted_iota(jnp.int32, tail.shape, 1)
    beta = _sigmoid(tail)
    g = -jnp.exp(alog) * _softplus(tail + dtb)
    return jnp.where(lane < DN_HEADS, beta, g)


def _softmax_lanes(s):
    m = jnp.max(s, axis=-1, keepdims=True)
    e = jnp.exp(s - m)
    return e / jnp.sum(e, axis=-1, keepdims=True)


def _mem_kv_kernel(x_ref, wk_ref, wv_ref, k_ref, v_ref, kb_ref, vb_ref):
    x = x_ref[...]
    k = _dot(x, wk_ref[0])
    v = _dot(x, wv_ref[0])
    halves = XA_HEAD_DIM // LANES
    for hd in range(XA_HEADS):
        for half in range(halves):
            lo = hd * XA_HEAD_DIM + half * LANES
            k_ref[0, :, half * XA_HEADS + hd, :] = k[:, lo:lo + LANES]
            v_ref[0, :, half * XA_HEADS + hd, :] = v[:, lo:lo + LANES]
    kb_ref[0] = k.astype(BF16)
    vb_ref[0] = v.astype(BF16)


def _mem_kv(mem2d, wk, wv):
    rows = mem2d.shape[0]
    tile = 512
    sub = XA_HEADS * XA_HEAD_DIM // LANES
    f32_out = jax.ShapeDtypeStruct((DEPTH, rows, sub, LANES), F32)
    bf_out = jax.ShapeDtypeStruct((DEPTH, rows, D_MODEL), BF16)
    w_spec = pl.BlockSpec((1, D_MODEL, D_MODEL), lambda l, i: (l, 0, 0))
    o_spec = pl.BlockSpec((1, tile, D_MODEL), lambda l, i: (l, i, 0))
    f_spec = pl.BlockSpec((1, tile, sub, LANES), lambda l, i: (l, i, 0, 0))
    return pl.pallas_call(
        _mem_kv_kernel,
        grid=(DEPTH, rows // tile),
        in_specs=[pl.BlockSpec((tile, D_MODEL), lambda l, i: (i, 0)), w_spec, w_spec],
        out_specs=[f_spec, f_spec, o_spec, o_spec],
        out_shape=[f32_out, f32_out, bf_out, bf_out],
        compiler_params=_params(2),
        name="p_mem_kv",
    )(mem2d, wk, wv)


def _causal_conv(buf_ref, w_ref, n_taps, hist, r0, width):
    rb = CONV_ROW_BLOCK
    off = hist - (n_taps - 1)
    sublane = lax.broadcasted_iota(jnp.int32, (1, SUBLANES, 1), 1)
    cols = []
    for c0 in range(0, width, CONV_COL_BLOCK):
        cs = slice(c0, c0 + CONV_COL_BLOCK)
        total = None
        for res in range(min(SUBLANES, n_taps)):
            base = (off + res) // SUBLANES * SUBLANES
            shift = off + res - base
            span = rb + (SUBLANES if shift else 0)
            part = None
            for kk in range(res, n_taps, SUBLANES):
                lo = r0 + base + kk - res
                rows = buf_ref[lo:lo + span, cs].reshape(span // SUBLANES, SUBLANES, CONV_COL_BLOCK)
                term = rows * w_ref[kk * SUBLANES:(kk + 1) * SUBLANES, cs][None]
                part = term if part is None else part + term
            if shift:
                turned = pltpu.roll(part, SUBLANES - shift, 1)
                part = jnp.where(sublane < SUBLANES - shift, turned[:-1], turned[1:])
            total = part if total is None else total + part
        cols.append(total.reshape(rb, CONV_COL_BLOCK))
    return jnp.concatenate(cols, axis=1)


def _zero_after(v):
    bits = lax.bitcast_convert_type(v, jnp.uint32)
    gone = lax.shift_right_logical(lax.shift_right_logical(bits, jnp.uint32(16)), jnp.uint32(16))
    return lax.bitcast_convert_type(gone, F32)


def _even_front_kernel(x_ref, g_ref, w_ref, wt_ref, dww_ref, dwb_ref, lng_ref, lnb_ref, scw_ref, alog_ref, dtb_ref,
                       aout_ref, z_ref, gb_ref, u_ref, wq_ref, kd_ref, qk_ref, cst_ref, qst_ref,
                       cbuf, qbuf, qkv, gate_a):
    tt = PROMPT_TILE
    j = pl.program_id(1)
    last = pl.num_programs(1) - 1

    @pl.when(j == 0)
    def _():
        cbuf[:, 0:CONV_HIST, :] = jnp.zeros((FRONT_ROWS, CONV_HIST, CONV_CH), F32)
        qbuf[:, 0:SC_HIST, :] = jnp.zeros((FRONT_ROWS, SC_HIST, QKV_CH), F32)

    @pl.when(j > 0)
    def _():
        cbuf[:, 0:CONV_HIST, :] = cbuf[:, tt:tt + CONV_HIST, :]
        qbuf[:, 0:SC_HIST, :] = qbuf[:, tt:tt + SC_HIST, :]

    def project(r, gain):
        h = _rms(x_ref[r], gain).astype(BF16)
        cbuf[r, CONV_HIST:CONV_HIST + tt, :] = (_dot(h, w_ref[:, COL_GLU_VAL:COL_GLU_VAL + CONV_CH])
                                                * _sigmoid(_dot(h, w_ref[:, COL_GLU_GATE:COL_GLU_GATE + CONV_CH])))
        gate_a[r] = _silu(_dot(h, w_ref[:, COL_GATE_A:COL_GATE_A + CONV_CH]))
        qbuf[r, SC_HIST:SC_HIST + tt, :] = _dot(h, w_ref[:, COL_QKV:COL_QKV + QKV_CH])
        gb_ref[r] = _gate_params(_dot(h, wt_ref[...]), alog_ref[...], dtb_ref[...])
        z = _dot(h, w_ref[:, COL_Z:COL_Z + DN_WIDTH])
        z_ref[r] = z
        return z[0:1, :]

    def finish(r):
        for r0 in range(0, tt, CONV_ROW_BLOCK):
            rows = slice(r0, r0 + CONV_ROW_BLOCK)
            acc = _silu(_causal_conv(qbuf.at[r], scw_ref, SC_K, SC_HIST, r0, QKV_CH))
            for hd in range(DN_HEADS):
                lo = hd * DN_HEAD_DIM
                qkv[r, rows, lo:lo + DN_HEAD_DIM] = _l2n(acc[:, lo:lo + DN_HEAD_DIM], DN_HEAD_DIM ** -0.5)
                qkv[r, rows, DN_WIDTH + lo:DN_WIDTH + lo + DN_HEAD_DIM] = _l2n(
                    acc[:, DN_WIDTH + lo:DN_WIDTH + lo + DN_HEAD_DIM])
            qkv[r, rows, 2 * DN_WIDTH:] = acc[:, 2 * DN_WIDTH:]

        def conv_rows(r0):
            acc = _causal_conv(cbuf.at[r], dww_ref, CONV_K, CONV_HIST, r0, CONV_CH)
            c = _silu(_layer_norm(acc + dwb_ref[...], lng_ref[...], lnb_ref[...]))
            aout_ref[r, r0:r0 + CONV_ROW_BLOCK, :] = (c * gate_a[r, r0:r0 + CONV_ROW_BLOCK, :]).astype(BF16)

        _delta_prep_tile(qkv[r, :, 0:DN_WIDTH], qkv[r, :, DN_WIDTH:2 * DN_WIDTH], qkv[r, :, 2 * DN_WIDTH:], gb_ref[r],
                         u_ref.at[r], wq_ref.at[r], kd_ref.at[r], qk_ref.at[r],
                         fillers=[functools.partial(conv_rows, r0) for r0 in range(0, tt, CONV_ROW_BLOCK)])

    gain = g_ref[...]
    for r in range(FRONT_ROWS):
        anchor = project(r, gain)
        gain = g_ref[...] + jnp.concatenate([_zero_after(anchor)] * (D_MODEL // DN_WIDTH), axis=1)
    for r in range(FRONT_ROWS):
        finish(r)

    @pl.when(j == last)
    def _():
        cst_ref[0] = cbuf[:, CONV_HIST + tt - (CONV_K - 1):CONV_HIST + tt, :]
        qst_ref[0] = qbuf[:, SC_HIST + tt - (SC_K - 1):SC_HIST + tt, :]


def _even_front(x, g, w, wt, dww, dwb, lng, lnb, scw, alog, dtb):
    b, t, _ = x.shape
    tt = PROMPT_TILE
    c = DN_CHUNK
    nr = FRONT_ROWS
    const = lambda shape: pl.BlockSpec(shape, lambda i, j: (0,) * len(shape))
    tile = lambda ch: pl.BlockSpec((nr, tt, ch), lambda i, j: (i, j, 0))
    return pl.pallas_call(
        _even_front_kernel,
        grid=(b // nr, t // tt),
        in_specs=[tile(D_MODEL), const((1, D_MODEL)), const((D_MODEL, EVEN_IN)), const((D_MODEL, LANES)),
                  const((CONV_K * SUBLANES, CONV_CH)), const((1, CONV_CH)), const((1, CONV_CH)), const((1, CONV_CH)),
                  const((SC_K * SUBLANES, QKV_CH)), const((1, LANES)), const((1, LANES))],
        out_specs=[tile(CONV_CH), tile(DN_WIDTH), tile(LANES), tile(DN_WIDTH),
                   pl.BlockSpec((nr, tt // c, 2 * c, DN_WIDTH), lambda i, j: (i, j, 0, 0)),
                   tile(DN_WIDTH),
                   pl.BlockSpec((nr, DN_HEADS, tt, c), lambda i, j: (i, 0, j, 0)),
                   pl.BlockSpec((1, nr, CONV_K - 1, CONV_CH), lambda i, j: (0, i, 0, 0)),
                   pl.BlockSpec((1, nr, SC_K - 1, QKV_CH), lambda i, j: (0, i, 0, 0))],
        out_shape=[jax.ShapeDtypeStruct((b, t, CONV_CH), BF16),
                   jax.ShapeDtypeStruct((b, t, DN_WIDTH), F32),
                   jax.ShapeDtypeStruct((b, t, LANES), F32),
                   jax.ShapeDtypeStruct((b, t, DN_WIDTH), F32),
                   jax.ShapeDtypeStruct((b, t // c, 2 * c, DN_WIDTH), BF16),
                   jax.ShapeDtypeStruct((b, t, DN_WIDTH), BF16),
                   jax.ShapeDtypeStruct((b, DN_HEADS, t, c), BF16),
                   jax.ShapeDtypeStruct((1, b, CONV_K - 1, CONV_CH), F32),
                   jax.ShapeDtypeStruct((1, b, SC_K - 1, QKV_CH), F32)],
        scratch_shapes=[pltpu.VMEM((nr, CONV_HIST + tt, CONV_CH), F32),
                        pltpu.VMEM((nr, SC_HIST + tt, QKV_CH), F32),
                        pltpu.VMEM((nr, tt, QKV_CH), F32),
                        pltpu.VMEM((nr, tt, CONV_CH), F32)],
        compiler_params=_params(2),
        name="p_even_front",
    )(x, g, w, wt, dww, dwb, lng, lnb, scw, alog, dtb)


def _delta_prep_tile(q2, k2, v2, gb, u_ref, wq_ref, kd_ref, qk_ref, fillers=()):
    fillers = iter(fillers)
    tt = PROMPT_TILE
    c = DN_CHUNK
    dh = DN_HEAD_DIM
    nch = tt // c
    row = lax.broadcasted_iota(jnp.int32, (c, c), 0)
    col = lax.broadcasted_iota(jnp.int32, (c, c), 1)
    incl = (row >= col)[None]
    strict = (row > col)[None]
    eye = jnp.where(row == col, 1.0, 0.0)[None]
    trow = lax.broadcasted_iota(jnp.int32, (tt, tt), 0)
    tcol = lax.broadcasted_iota(jnp.int32, (tt, tt), 1)
    same_chunk = jnp.right_shift(trow, 6) == jnp.right_shift(tcol, 6)
    tri = jnp.where((trow >= tcol) & same_chunk, 1.0, 0.0).astype(BF16)

    def lane_rep(col0):
        return jnp.concatenate([jnp.broadcast_to(gb[:, col0 + hd:col0 + hd + 1], (tt, dh))
                                for hd in range(DN_HEADS)], axis=1)

    def chunks(x):
        return jnp.concatenate([x[:, hd * dh:(hd + 1) * dh].reshape(nch, c, dh) for hd in range(DN_HEADS)], axis=0)

    gc4 = _dot_exact_lhs(tri, lane_rep(DN_HEADS))
    gc = chunks(gc4)
    gc_rows = []
    for hd in range(DN_HEADS):
        gc_t = gc4[:, hd * dh:(hd + 1) * dh].T
        gc_rows += [gc_t[0:c, n * c:(n + 1) * c][None] for n in range(nch)]
    gc_row = jnp.concatenate(gc_rows, axis=0)
    beta = chunks(lane_rep(0))
    q = chunks(q2)
    k = chunks(k2)
    v = chunks(v2)
    egc = jnp.exp(gc)
    kb = k * beta
    decay = jnp.where(incl, jnp.exp(jnp.where(incl, gc[:, :, 0:c] - gc_row, 0.0)), 0.0)
    a = jnp.where(strict, _bmm_nt(kb, k) * decay, 0.0)
    rhs = jnp.concatenate([v * beta, kb * egc], axis=2)
    p = -a
    t_inv = eye + p
    for _ in range(5):
        p = _bmm(p, p)
        t_inv = t_inv + _bmm(p, t_inv)
        next(fillers, lambda: None)()
    x0 = _bmm(t_inv, rhs).astype(BF16).astype(F32)
    a_hi, a_lo = _split2(a)
    rho = rhs - x0 - (_bmm(a_hi, x0) + _bmm(a_lo, x0))
    x = x0 + _bmm(t_inv, rho)
    k_dec = k * jnp.exp(gc[:, c - 1:c, :] - gc)
    q_dec = q * egc
    qk = jnp.where(incl, _bmm_nt(q, k) * decay, 0.0)
    for hd in range(DN_HEADS):
        lo = hd * dh
        hs = slice(hd * nch, (hd + 1) * nch)
        u_ref[:, lo:lo + dh] = x[hs, :, 0:dh].reshape(tt, dh)
        wq_ref[:, 0:c, lo:lo + dh] = x[hs, :, dh:].astype(BF16)
        wq_ref[:, c:2 * c, lo:lo + dh] = q_dec[hs].astype(BF16)
        kd_ref[:, lo:lo + dh] = k_dec[hs].reshape(tt, dh).astype(BF16)
        qk_ref[hd] = qk[hs].reshape(tt, c).astype(BF16)
    for rest in fillers:
        rest()


def _delta_scan_kernel(u_ref, wq_ref, kd_ref, qk_ref, gb_ref, z_ref, dng_ref,
                       sq_ref, sk_ref, sv_ref, sgb_ref, sz_ref, ssin_ref,
                       bout_ref, sout_ref, sbout_ref, ssout_ref, s_ref):
    c = DN_CHUNK
    dh = DN_HEAD_DIM
    nb = u_ref.shape[0]
    n = pl.program_id(0)
    last = pl.num_programs(0) - 1

    @pl.when(n == 0)
    def _():
        s_ref[...] = jnp.zeros(s_ref.shape, F32)

    def heads(ref, rows):
        return jnp.concatenate([ref[:, rows, hd * dh:(hd + 1) * dh] for hd in range(DN_HEADS)], axis=0)

    s = s_ref[...]
    for ci in range(SCAN_CHUNKS):
        rows = slice(ci * c, (ci + 1) * c)
        g_tot = jnp.sum(gb_ref[:, rows, :], axis=1, keepdims=True)
        g_last = jnp.exp(jnp.concatenate(
            [jnp.broadcast_to(g_tot[:, :, DN_HEADS + hd:DN_HEADS + hd + 1], (nb, 1, dh)) for hd in range(DN_HEADS)],
            axis=0))
        wq = jnp.concatenate([wq_ref[:, ci, :, hd * dh:(hd + 1) * dh] for hd in range(DN_HEADS)], axis=0)
        qk = jnp.concatenate([qk_ref[:, hd, rows, :] for hd in range(DN_HEADS)], axis=0)
        ws = _bmm(wq, s)
        v_new = heads(u_ref, rows) - ws[:, 0:c]
        o = ws[:, c:] + _bmm(qk, v_new)
        s = s * g_last + _bmm_tn(heads(kd_ref, rows), v_new)
        out = (_rms(o, dng_ref[...]) * _silu(heads(z_ref, rows))).astype(BF16)
        for hd in range(DN_HEADS):
            bout_ref[:, rows, hd * dh:(hd + 1) * dh] = out[hd * nb:(hd + 1) * nb]
        per = sq_ref.shape[0] // SCAN_CHUNKS
        _sample_delta_rows(sq_ref, sk_ref, sv_ref, sgb_ref, sz_ref, dng_ref, ssin_ref, sbout_ref, ssout_ref,
                           range(ci * per, (ci + 1) * per))
    s_ref[...] = s

    @pl.when(n == last)
    def _():
        for hd in range(DN_HEADS):
            sout_ref[0, :, hd] = s_ref[hd * nb:(hd + 1) * nb]


def _delta_scan(u, wq, kd, qk, gb, z, dng, sq, sk, sv, sgb, sz, s_in):
    b, t, _ = u.shape
    c = DN_CHUNK
    span = SCAN_CHUNKS * c
    ns = sq.shape[0]
    srb = ns // (t // span)
    assert srb * (t // span) == ns and srb % SUBLANES == 0 and srb % SCAN_CHUNKS == 0
    rows = lambda ch: pl.BlockSpec((b, span, ch), lambda n: (0, n, 0))
    srows = lambda ch: pl.BlockSpec((srb, ch), lambda n: (n, 0))
    sstate = pl.BlockSpec((1, srb, DN_HEADS, DN_HEAD_DIM, DN_HEAD_DIM), lambda n: (0, n, 0, 0, 0))
    return pl.pallas_call(
        _delta_scan_kernel,
        grid=(t // span,),
        in_specs=[rows(DN_WIDTH),
                  pl.BlockSpec((b, SCAN_CHUNKS, 2 * c, DN_WIDTH), lambda n: (0, n, 0, 0)),
                  rows(DN_WIDTH),
                  pl.BlockSpec((b, DN_HEADS, span, c), lambda n: (0, 0, n, 0)),
                  rows(LANES), rows(DN_WIDTH),
                  pl.BlockSpec((1, DN_HEAD_DIM), lambda n: (0, 0)),
                  srows(DN_WIDTH), srows(DN_WIDTH), srows(DN_WIDTH), srows(LANES), srows(DN_WIDTH), sstate],
        out_specs=[rows(DN_WIDTH),
                   pl.BlockSpec((1, b, DN_HEADS, DN_HEAD_DIM, DN_HEAD_DIM), lambda n: (0, 0, 0, 0, 0)),
                   srows(DN_WIDTH), sstate],
        out_shape=[jax.ShapeDtypeStruct((b, t, DN_WIDTH), BF16),
                   jax.ShapeDtypeStruct((1, b, DN_HEADS, DN_HEAD_DIM, DN_HEAD_DIM), F32),
                   jax.ShapeDtypeStruct((ns, DN_WIDTH), F32),
                   jax.ShapeDtypeStruct(s_in.shape, F32)],
        scratch_shapes=[pltpu.VMEM((DN_HEADS * b, DN_HEAD_DIM, DN_HEAD_DIM), F32)],
        compiler_params=_params(1),
        name="p_delta_scan",
    )(u, wq, kd, qk, gb, z, dng, sq, sk, sv, sgb, sz, s_in)


def _row_streams(tt):
    rows = tt // POST_STREAMS
    return [slice(i * rows, (i + 1) * rows) for i in range(POST_STREAMS)]


def _split_head_dim(x):
    lead = x.shape[:-2]
    halves = XA_HEAD_DIM // LANES
    x = x.reshape(lead + (XA_HEADS, halves, LANES))
    x = jnp.swapaxes(x, -3, -2)
    return x.reshape(lead + (halves * XA_HEADS, LANES))


def _merge_head_dim(x):
    lead = x.shape[:-2]
    halves = XA_HEAD_DIM // LANES
    x = jnp.swapaxes(x.reshape(lead + (halves, XA_HEADS, LANES)), -3, -2)
    return x.reshape(lead + (XA_HEADS, XA_HEAD_DIM))


def _sample_xattn_rows(q_ref, mk_ref, mv_ref, o_ref):
    def row(i):
        prod = mk_ref[0, i] * q_ref[i]
        prod = prod + pltpu.roll(prod, XA_HEADS, 1)
        s = jnp.sum(prod, axis=-1, keepdims=True)
        e = jnp.exp(s - jnp.max(s, axis=0, keepdims=True))
        o_ref[i] = jnp.sum(e * mv_ref[0, i], axis=0) / jnp.sum(e, axis=0)
    return [functools.partial(row, i) for i in range(XATTN_ROW_BLOCK)]


def _xattn_streams(x1s, g, wq_ref, wo_ref, mk_ref, mv_ref, fillers=()):
    fillers = iter(fillers)
    qs = [jnp.dot(_rms(x1, g).astype(BF16), wq_ref[0], preferred_element_type=F32) for x1 in x1s]
    next(fillers, lambda: None)()
    outs = [[] for _ in x1s]
    for hd in range(XA_HEADS):
        lo = hd * XA_HEAD_DIM
        ss = [_dot_nt(q[:, lo:lo + XA_HEAD_DIM], mk_ref[0, :, lo:lo + XA_HEAD_DIM]) * XA_SCALE
              for q in qs]
        for out, sc in zip(outs, ss):
            out.append(_dot(_softmax_lanes(sc), mv_ref[0, :, lo:lo + XA_HEAD_DIM]).astype(BF16))
    next(fillers, lambda: None)()
    x2s = [x1 + jnp.dot(jnp.concatenate(out, axis=1), wo_ref[0], preferred_element_type=F32)
           for x1, out in zip(x1s, outs)]
    for rest in fillers:
        rest()
    return x2s


def _even_post_kernel(a_ref, b_ref, x_ref, wout_ref, gx_ref, wq_ref, wo_ref, mk_ref, mv_ref, sq_ref, smk_ref, smv_ref,
                      o_ref, so_ref):
    streams = _row_streams(POST_TILE)
    fillers = _sample_xattn_rows(sq_ref, smk_ref, smv_ref, so_ref)
    x1s = [x_ref[0, rs, :] + jnp.dot(jnp.concatenate([a_ref[0, rs, :], b_ref[0, rs, :]], axis=1), wout_ref[...],
                                     preferred_element_type=F32) for rs in streams]
    fillers.pop(0)()
    for rs, x2 in zip(streams, _xattn_streams(x1s, gx_ref[...], wq_ref, wo_ref, mk_ref, mv_ref, fillers)):
        o_ref[0, rs, :] = x2


def _layer_specs(layer, n_tiles):
    sub = XA_HEADS * XA_HEAD_DIM // LANES
    step = lambda i, j: i * n_tiles + j
    weight = pl.BlockSpec((1, D_MODEL, D_MODEL), lambda i, j: (layer, 0, 0))
    mem = pl.BlockSpec((1, N_MEM, D_MODEL), lambda i, j: (layer, i, 0))
    srows = pl.BlockSpec((XATTN_ROW_BLOCK, sub, LANES), lambda i, j: (step(i, j), 0, 0))
    scache = pl.BlockSpec((1, XATTN_ROW_BLOCK, N_MEM, sub, LANES), lambda i, j: (layer, step(i, j), 0, 0, 0))
    return weight, mem, srows, scache


def _even_post(a, bo, x, wout, gx, wq, wo, mk, mv, layer, sq, smk, smv):
    b, t, _ = x.shape
    tt = POST_TILE
    ns = sq.shape[0]
    assert ns == b * (t // tt) * XATTN_ROW_BLOCK
    const = lambda shape: pl.BlockSpec(shape, lambda i, j: (0,) * len(shape))
    tile = lambda c: pl.BlockSpec((1, tt, c), lambda i, j: (i, j, 0))
    weight, mem, srows, scache = _layer_specs(layer, t // tt)
    return pl.pallas_call(
        _even_post_kernel,
        grid=(b, t // tt),
        in_specs=[tile(CONV_CH), tile(DN_WIDTH), tile(D_MODEL), const((D_MODEL, D_MODEL)), const((1, D_MODEL)),
                  weight, weight, mem, mem, srows, scache, scache],
        out_specs=[tile(D_MODEL), srows],
        out_shape=[jax.ShapeDtypeStruct((b, t, D_MODEL), F32), jax.ShapeDtypeStruct(sq.shape, F32)],
        compiler_params=_params(2),
        name="p_even_post",
    )(a, bo, x, wout, gx, wq, wo, mk, mv, sq, smk, smv)


def _pool_group_linear(pooled, wp_ref, bp_ref):
    outs = []
    for gi in range(len(POOL_WINDOWS)):
        lo = gi * POOL_GROUP
        outs.append(_dot(pooled[:, lo:lo + POOL_GROUP], wp_ref[gi]) + bp_ref[gi:gi + 1, :])
    return jnp.concatenate(outs, axis=1)


def _rows_down(grp, shift):
    earlier = lambda v: jnp.concatenate([v[:1], v[:-1]], axis=0)
    if shift % SUBLANES == 0:
        for _ in range(shift // SUBLANES):
            grp = earlier(grp)
        return grp
    sublane = lax.broadcasted_iota(jnp.int32, (1, SUBLANES, 1), 1)
    turned = pltpu.roll(grp, shift, 1)
    return jnp.where(sublane >= shift, turned, earlier(turned))


def _odd_kernel(x_ref, g_ref, win_ref, wp_ref, bp_ref, sc_ref, wout_ref, gx_ref, wq_ref, wo_ref, mk_ref, mv_ref,
                gf_ref, sq_ref, smk_ref, smv_ref, y_ref, pst_ref, so_ref, pbuf):
    tt = POST_TILE
    j = pl.program_id(1)
    last = pl.num_programs(1) - 1
    streams = _row_streams(tt)
    fillers = _sample_xattn_rows(sq_ref, smk_ref, smv_ref, so_ref)

    @pl.when(j == 0)
    def _():
        pbuf[0:POOL_HIST, :] = jnp.zeros((POOL_HIST, D_MODEL), F32)

    @pl.when(j > 0)
    def _():
        pbuf[0:POOL_HIST, :] = pbuf[tt:tt + POOL_HIST, :]

    xs = [x_ref[0, rs, :] for rs in streams]
    us, gates = [], []
    for x, rs in zip(xs, streams):
        h = _rms(x, g_ref[...]).astype(BF16)
        u = jnp.dot(h, win_ref[:, 0:D_MODEL], preferred_element_type=F32)
        pbuf[POOL_HIST + rs.start:POOL_HIST + rs.stop, :] = u
        us.append(u)
        gates.append(_silu(jnp.dot(h, win_ref[:, D_MODEL:], preferred_element_type=F32)))
    fillers.pop(0)()

    x1s = []
    for x, u, gate, rs in zip(xs, us, gates, streams):
        pos = j * tt + rs.start + lax.broadcasted_iota(jnp.int32, (rs.stop - rs.start, 1), 0)
        nrows = rs.stop - rs.start
        acc = pbuf[rs.start:POOL_HIST + rs.stop, :].reshape((POOL_HIST + nrows) // SUBLANES, SUBLANES, D_MODEL)
        means = []
        for gi, win in enumerate(POOL_WINDOWS):
            acc = acc + _rows_down(acc, win // 2)
            cnt = jnp.minimum(pos + 1, win).astype(F32)
            means.append(acc[POOL_HIST // SUBLANES:, :, 0:POOL_GROUP].reshape(nrows, POOL_GROUP) / cnt)
            acc = acc[:, :, POOL_GROUP:]
        pooled = jnp.concatenate(means, axis=1) - u
        z = _pool_group_linear(pooled, wp_ref, bp_ref) * sc_ref[...] * gate
        x1s.append(x + jnp.dot(z.astype(BF16), wout_ref[...], preferred_element_type=F32))
    for rs, x2 in zip(streams, _xattn_streams(x1s, gx_ref[...], wq_ref, wo_ref, mk_ref, mv_ref, fillers)):
        y_ref[0, rs, :] = _rms(x2, gf_ref[...])

    @pl.when(j == last)
    def _():
        pst_ref[0, 0] = pbuf[POOL_HIST + tt - POOL_BUF:POOL_HIST + tt, :]


def _odd(x, g, win, wp, bp, sc, wout, gx, wq, wo, mk, mv, gf, layer, sq, smk, smv):
    b, t, _ = x.shape
    tt = POST_TILE
    ns = sq.shape[0]
    assert ns == b * (t // tt) * XATTN_ROW_BLOCK
    const = lambda shape: pl.BlockSpec(shape, lambda i, j: (0,) * len(shape))
    tile = lambda c: pl.BlockSpec((1, tt, c), lambda i, j: (i, j, 0))
    weight, mem, srows, scache = _layer_specs(layer, t // tt)
    ngrp = len(POOL_WINDOWS)
    return pl.pallas_call(
        _odd_kernel,
        grid=(b, t // tt),
        in_specs=[tile(D_MODEL), const((1, D_MODEL)), const((D_MODEL, 2 * D_MODEL)),
                  const((ngrp, POOL_GROUP, POOL_GROUP)), const((ngrp, POOL_GROUP)), const((1, D_MODEL)),
                  const((D_MODEL, D_MODEL)), const((1, D_MODEL)), weight, weight, mem, mem, const((1, D_MODEL)),
                  srows, scache, scache],
        out_specs=[tile(D_MODEL), pl.BlockSpec((1, 1, POOL_BUF, D_MODEL), lambda i, j: (0, i, 0, 0)), srows],
        out_shape=[jax.ShapeDtypeStruct((b, t, D_MODEL), F32),
                   jax.ShapeDtypeStruct((1, b, POOL_BUF, D_MODEL), F32),
                   jax.ShapeDtypeStruct(sq.shape, F32)],
        scratch_shapes=[pltpu.VMEM((POOL_HIST + tt, D_MODEL), F32)],
        compiler_params=_params(2),
        name="p_odd_layer",
    )(x, g, win, wp, bp, sc, wout, gx, wq, wo, mk, mv, gf, sq, smk, smv)


def _push_row(old_ref, new_ref, row):
    depth = old_ref.shape[0]
    new_ref[0:depth - 1] = old_ref[1:depth]
    new_ref[depth - 1] = row


def _history_major(state):
    return jnp.transpose(state[0], (1, 0, 2))


def _s_even_pre_kernel(x_ref, g_ref, w_ref, wt_ref, dww_ref, dwb_ref, lng_ref, lnb_ref, scw_ref, alog_ref, dtb_ref,
                       cin_ref, qin_ref,
                       aout_ref, q_ref, k_ref, v_ref, gb_ref, z_ref, cout_ref, qout_ref):
    h = _rms(x_ref[...], g_ref[...]).astype(BF16)
    glu = (_dot(h, w_ref[:, COL_GLU_VAL:COL_GLU_VAL + CONV_CH])
           * _sigmoid(_dot(h, w_ref[:, COL_GLU_GATE:COL_GLU_GATE + CONV_CH])))
    nb = CONV_K - 1
    acc = dww_ref[nb:nb + 1, :] * glu
    for kk in range(nb):
        acc = acc + dww_ref[kk:kk + 1, :] * cin_ref[kk]
    _push_row(cin_ref, cout_ref, glu)
    c = _silu(_layer_norm(acc + dwb_ref[...], lng_ref[...], lnb_ref[...]))
    aout_ref[...] = c * _silu(_dot(h, w_ref[:, COL_GATE_A:COL_GATE_A + CONV_CH]))

    qkv = _dot(h, w_ref[:, COL_QKV:COL_QKV + QKV_CH])
    ns = SC_K - 1
    acc = scw_ref[ns:ns + 1, :] * qkv
    for kk in range(ns):
        acc = acc + scw_ref[kk:kk + 1, :] * qin_ref[kk]
    _push_row(qin_ref, qout_ref, qkv)
    acc = _silu(acc)
    for hd in range(DN_HEADS):
        lo = hd * DN_HEAD_DIM
        q_ref[:, lo:lo + DN_HEAD_DIM] = _l2n(acc[:, lo:lo + DN_HEAD_DIM], DN_HEAD_DIM ** -0.5)
        k_ref[:, lo:lo + DN_HEAD_DIM] = _l2n(acc[:, DN_WIDTH + lo:DN_WIDTH + lo + DN_HEAD_DIM])
    v_ref[...] = acc[:, 2 * DN_WIDTH:]
    z_ref[...] = _dot(h, w_ref[:, COL_Z:COL_Z + DN_WIDTH])
    gb_ref[...] = _gate_params(_dot(h, wt_ref[...]), alog_ref[...], dtb_ref[...])


def _s_even_pre(x, g, w, wt, dww, dwb, lng, lnb, scw, alog, dtb, cin, qin):
    n = x.shape[0]
    rbk = SAMPLE_ROW_BLOCK
    const = lambda shape: pl.BlockSpec(shape, lambda i: (0,) * len(shape))
    rows = lambda c: pl.BlockSpec((rbk, c), lambda i: (i, 0))
    hist = lambda arr: pl.BlockSpec((arr.shape[0], rbk, arr.shape[2]), lambda i: (0, i, 0))
    return pl.pallas_call(
        _s_even_pre_kernel,
        grid=(n // rbk,),
        in_specs=[rows(D_MODEL), const((1, D_MODEL)), const((D_MODEL, EVEN_IN)), const((D_MODEL, LANES)),
                  const((CONV_K, CONV_CH)),
                  const((1, CONV_CH)), const((1, CONV_CH)), const((1, CONV_CH)), const((SC_K, QKV_CH)),
                  const((1, LANES)), const((1, LANES)), hist(cin), hist(qin)],
        out_specs=[rows(CONV_CH), rows(DN_WIDTH), rows(DN_WIDTH), rows(DN_WIDTH), rows(LANES), rows(DN_WIDTH),
                   hist(cin), hist(qin)],
        out_shape=[jax.ShapeDtypeStruct((n, CONV_CH), F32),
                   jax.ShapeDtypeStruct((n, DN_WIDTH), F32),
                   jax.ShapeDtypeStruct((n, DN_WIDTH), F32),
                   jax.ShapeDtypeStruct((n, DN_WIDTH), F32),
                   jax.ShapeDtypeStruct((n, LANES), F32),
                   jax.ShapeDtypeStruct((n, DN_WIDTH), F32),
                   jax.ShapeDtypeStruct(cin.shape, F32),
                   jax.ShapeDtypeStruct(qin.shape, F32)],
        compiler_params=_params(1),
        name="s_even_pre",
    )(x, g, w, wt, dww, dwb, lng, lnb, scw, alog, dtb, cin, qin)


def _sample_delta_rows(q_ref, k_ref, v_ref, gb_ref, z_ref, dng_ref, sin_ref, bout_ref, sout_ref, row_ids):
    dh = DN_HEAD_DIM
    chains = [(i, hd) for i in row_ids for hd in range(DN_HEADS)]
    vec = lambda ref, i, hd: ref[i:i + 1, hd * dh:(hd + 1) * dh]
    col = lambda ref, i, hd: jnp.broadcast_to(vec(ref, i, hd), (dh, dh)).T
    kcol = [col(k_ref, i, hd) for i, hd in chains]
    qcol = [col(q_ref, i, hd) for i, hd in chains]
    s = [sin_ref[0, i, hd] * jnp.exp(gb_ref[i:i + 1, DN_HEADS + hd:DN_HEADS + hd + 1]) for i, hd in chains]
    v_new = [(vec(v_ref, i, hd) - jnp.sum(kc * sc, axis=0, keepdims=True)) * gb_ref[i:i + 1, hd:hd + 1]
             for (i, hd), kc, sc in zip(chains, kcol, s)]
    s = [sc + kc * vn for sc, kc, vn in zip(s, kcol, v_new)]
    for (i, hd), sc, qc in zip(chains, s, qcol):
        sout_ref[0, i, hd] = sc
        o = jnp.sum(qc * sc, axis=0, keepdims=True)
        bout_ref[i:i + 1, hd * dh:(hd + 1) * dh] = _rms(o, dng_ref[...]) * _silu(vec(z_ref, i, hd))


def _s_mix_out_kernel(a_ref, b_ref, x_ref, wout_ref, gx_ref, wq_ref, x1_ref, q_ref):
    mix = jnp.concatenate([a_ref[...], b_ref[...]], axis=1).astype(BF16)
    x1 = x_ref[...] + jnp.dot(mix, wout_ref[...], preferred_element_type=F32)
    x1_ref[...] = x1
    q_ref[...] = jnp.dot(_rms(x1, gx_ref[...]).astype(BF16), wq_ref[0], preferred_element_type=F32) * XA_SCALE


def _s_mix_out(a, bo, x, wout, gx, wq, layer):
    n = x.shape[0]
    full = lambda arr: pl.BlockSpec(arr.shape, lambda i: (0,) * arr.ndim)
    args = (a, bo, x, wout, gx, wq)
    out = jax.ShapeDtypeStruct((n, D_MODEL), F32)
    return pl.pallas_call(
        _s_mix_out_kernel,
        grid=(1,),
        in_specs=[full(v) for v in args[:-1]] + [pl.BlockSpec((1, D_MODEL, D_MODEL), lambda i: (layer, 0, 0))],
        out_specs=[pl.BlockSpec((n, D_MODEL), lambda i: (0, 0))] * 2,
        out_shape=[out, out],
        compiler_params=_params(1),
        name="s_mix_out",
    )(*args)


def _s_odd_kernel(x1_ref, o_ref, wo_ref, g_ref, win_ref, wp_ref, bp_ref, sc_ref, wout_ref, gx_ref, wq_ref, pin_ref,
                  x1o_ref, q_ref, pout_ref):
    x = x1_ref[...] + _dot(o_ref[...], wo_ref[0])
    h = _rms(x, g_ref[...]).astype(BF16)
    u = jnp.dot(h, win_ref[:, 0:D_MODEL], preferred_element_type=F32)
    gate = _silu(jnp.dot(h, win_ref[:, D_MODEL:], preferred_element_type=F32))
    means = []
    for gi, win in enumerate(POOL_WINDOWS):
        lo = gi * POOL_GROUP
        acc = u[:, lo:lo + POOL_GROUP]
        for d in range(1, win):
            acc = acc + pin_ref[POOL_BUF - d, :, lo:lo + POOL_GROUP]
        means.append(acc / float(min(PAST_LEN + 1, win)))
    pooled = jnp.concatenate(means, axis=1) - u
    _push_row(pin_ref, pout_ref, u)
    z = _pool_group_linear(pooled, wp_ref, bp_ref) * sc_ref[...] * gate
    x1 = x + jnp.dot(z.astype(BF16), wout_ref[...], preferred_element_type=F32)
    x1o_ref[...] = x1
    q_ref[...] = jnp.dot(_rms(x1, gx_ref[...]).astype(BF16), wq_ref[0], preferred_element_type=F32) * XA_SCALE


def _s_odd(x1, o, wo, g, win, wp, bp, sc, wout, gx, wq, pin, layer):
    n = x1.shape[0]
    rbk = SAMPLE_ROW_BLOCK
    ngrp = len(POOL_WINDOWS)
    const = lambda shape: pl.BlockSpec(shape, lambda i: (0,) * len(shape))
    rows = lambda c: pl.BlockSpec((rbk, c), lambda i: (i, 0))
    hist = pl.BlockSpec((pin.shape[0], rbk, pin.shape[2]), lambda i: (0, i, 0))
    out = jax.ShapeDtypeStruct((n, D_MODEL), F32)
    return pl.pallas_call(
        _s_odd_kernel,
        grid=(n // rbk,),
        in_specs=[rows(D_MODEL), rows(D_MODEL), pl.BlockSpec((1, D_MODEL, D_MODEL), lambda i: (layer - 1, 0, 0)),
                  const((1, D_MODEL)),
                  const((D_MODEL, 2 * D_MODEL)), const((ngrp, POOL_GROUP, POOL_GROUP)), const((ngrp, POOL_GROUP)),
                  const((1, D_MODEL)), const((D_MODEL, D_MODEL)), const((1, D_MODEL)),
                  pl.BlockSpec((1, D_MODEL, D_MODEL), lambda i: (layer, 0, 0)),
                  hist],
        out_specs=[rows(D_MODEL), rows(D_MODEL), hist],
        out_shape=[out, out, jax.ShapeDtypeStruct(pin.shape, F32)],
        compiler_params=_params(1),
        name="s_odd",
    )(x1, o, wo, g, win, wp, bp, sc, wout, gx, wq, pin)


def _s_final_kernel(x1_ref, o_ref, wo_ref, gf_ref, y_ref):
    y_ref[...] = _rms(x1_ref[...] + _dot(o_ref[...], wo_ref[0]), gf_ref[...])


def _s_final(x1, o, wo, gf, layer):
    n = x1.shape[0]
    full = lambda arr: pl.BlockSpec(arr.shape, lambda i: (0,) * arr.ndim)
    args = (x1, o, wo, gf)
    return pl.pallas_call(
        _s_final_kernel,
        grid=(1,),
        in_specs=[full(x1), full(o), pl.BlockSpec((1, D_MODEL, D_MODEL), lambda i: (layer, 0, 0)), full(gf)],
        out_specs=pl.BlockSpec((n, D_MODEL), lambda i: (0, 0)),
        out_shape=jax.ShapeDtypeStruct((n, D_MODEL), F32),
        compiler_params=_params(1),
        name="s_final",
    )(*args)


def _lane_pad(vec, offset):
    return jnp.pad(vec.astype(F32), (offset, LANES - offset - vec.shape[0])).reshape(1, LANES)


def kernel(x_prompt, x_sample, state_conv_a, state_qkv_conv, state_delta, state_pool, cache_mem_k, cache_mem_v, mem_prompt, norm_mix, norm_xattn, norm_final, w_in_even, w_out_even, dw_w, dw_b, ln_a_g, ln_a_b, sc_w, a_log, dt_bias, dn_norm_g, w_in_odd, w_pool, b_pool, pool_scale, w_out_odd, w_xq, w_xk, w_xv, w_xo):
    bp, t, _ = x_prompt.shape
    ns = x_sample.shape[0]
    row = lambda v: v.reshape(1, -1)

    w_in0 = w_in_even[0].astype(BF16)
    w_tail0 = jnp.pad(w_in_even[0][:, COL_TAIL:], ((0, 0), (0, LANES - (EVEN_IN - COL_TAIL)))).astype(BF16)
    w_out0 = w_out_even[0].astype(BF16)
    w_in1 = w_in_odd[0].astype(BF16)
    w_pool1 = w_pool[0].astype(BF16)
    w_out1 = w_out_odd[0].astype(BF16)
    wq = w_xq.astype(BF16)
    wk = w_xk.astype(BF16)
    wv = w_xv.astype(BF16)
    wo = w_xo.astype(BF16)
    alog = _lane_pad(a_log[0], DN_HEADS)
    dtb = _lane_pad(dt_bias[0], DN_HEADS)
    even_small = (dw_w[0], row(dw_b[0]), row(ln_a_g[0]), row(ln_a_b[0]), sc_w[0], alog, dtb)
    dng = row(dn_norm_g[0])

    mk_f, mv_f, mk_b, mv_b = _mem_kv(mem_prompt.reshape(bp * N_MEM, D_MODEL), wk, wv)
    new_mem_k_p = _merge_head_dim(mk_f.reshape(DEPTH, bp, N_MEM, -1, LANES))
    new_mem_v_p = _merge_head_dim(mv_f.reshape(DEPTH, bp, N_MEM, -1, LANES))

    even_small_p = (jnp.repeat(dw_w[0], SUBLANES, axis=0),) + even_small[1:4] + (
        jnp.repeat(sc_w[0], SUBLANES, axis=0),) + even_small[5:]
    a_out, z, gb, u, wqd, kd, qk, new_conv_a_p, new_qkv_conv_p = _even_front(x_prompt, row(norm_mix[0]), w_in0,
                                                                             w_tail0, *even_small_p)

    xs = x_sample.reshape(ns, D_MODEL)
    sa, sq, sk, sv, sgb, sz, cout, qout = _s_even_pre(xs, row(norm_mix[0]), w_in0, w_tail0, *even_small,
                                                      _history_major(state_conv_a), _history_major(state_qkv_conv))
    b_out, new_delta_p, sb, new_delta_s = _delta_scan(u, wqd, kd, qk, gb, z, dng, sq, sk, sv, sgb, sz, state_delta)
    x1, xq = _s_mix_out(sa, sb, xs, w_out0, row(norm_xattn[0]), wq, 0)

    heads = lambda v: _split_head_dim(v.reshape(ns, XA_HEADS, XA_HEAD_DIM))
    unheads = lambda v: _merge_head_dim(v).reshape(ns, D_MODEL)
    cmk, cmv = _split_head_dim(cache_mem_k), _split_head_dim(cache_mem_v)
    x2, o0 = _even_post(a_out, b_out, x_prompt, w_out0, row(norm_xattn[0]), wq, wo, mk_b, mv_b, 0, heads(xq), cmk, cmv)
    x1, xq, pout = _s_odd(x1, unheads(o0), wo, row(norm_mix[1]), w_in1, w_pool1, b_pool[0], row(pool_scale[0]), w_out1,
                          row(norm_xattn[1]), wq, _history_major(state_pool), 1)
    y_prompt, new_pool_p, o1 = _odd(x2, row(norm_mix[1]), w_in1, w_pool1, b_pool[0], row(pool_scale[0]), w_out1,
                                    row(norm_xattn[1]), wq, wo, mk_b, mv_b, row(norm_final), 1, heads(xq), cmk, cmv)
    y_sample = _s_final(x1, unheads(o1), wo, row(norm_final), 1).reshape(ns, 1, D_MODEL)

    new_conv_a_s, new_qkv_conv_s, new_pool_s = (jnp.transpose(st, (1, 0, 2))[None] for st in (cout, qout, pout))
    return (y_prompt, y_sample, new_conv_a_p, new_qkv_conv_p, new_delta_p, new_pool_p, new_mem_k_p,
            new_mem_v_p, new_conv_a_s, new_qkv_conv_s, new_delta_s, new_pool_s)
```
